```python
import math
import jax, jax.numpy as jnp
from jax import lax
import numpy as np

D_MODEL = 4096
BATCH = 2
SEQ = 4096
DEPTH = 2

HD = 128
GRID_W = 64
Q_BLOCK = 128
A_HEADS = 8
NA_ROWS = 8
NA_COLS = 16
B_Q_HEADS = 8
B_KV_HEADS = 2
ROPE_THETA = 10000.0
C_HEADS = 4
D_PATTERNS = ((128, 1), (512, 4), (2048, 16))
D_GROUPS = 3
D_HEADS_PER_GROUP = 4
N_BRANCHES = 4
RMS_EPS = 1e-6
NEG_INF = -1e30

A_W = A_HEADS * HD
B_QW = B_Q_HEADS * HD
B_KVW = B_KV_HEADS * HD
C_QKW = C_HEADS * 2 * HD
C_VW = C_HEADS * 2 * HD
D_QKVW = D_GROUPS * D_HEADS_PER_GROUP * HD
D_OW = D_HEADS_PER_GROUP * HD
SPLIT_WIDTHS = (A_W, A_W, A_W, A_W,
                B_QW, B_KVW, B_KVW, B_QW,
                C_QKW, C_QKW, C_VW, C_VW,
                D_QKVW, D_QKVW, D_QKVW, D_OW) + (D_MODEL,) * N_BRANCHES
N_IN = sum(SPLIT_WIDTHS)

kernel_name = "hybrid_parallel_gated_encoder"


def rms_norm(x, g):
    xf = x.astype(jnp.float32)
    y = xf * lax.rsqrt(jnp.mean(xf * xf, axis=-1, keepdims=True) + RMS_EPS)
    return (y * g.astype(jnp.float32)).astype(x.dtype)


def alibi_slopes(n):
    return jnp.asarray(2.0 ** (-8.0 * np.arange(1, n + 1) / n), dtype=jnp.float32)


def axial_rope(S):
    t = jnp.arange(S)
    row = (t // GRID_W).astype(jnp.float32)
    col = (t % GRID_W).astype(jnp.float32)
    n_pairs = HD // 4
    inv_freq = ROPE_THETA ** (-jnp.arange(n_pairs, dtype=jnp.float32) / n_pairs)
    ang = jnp.concatenate([row[:, None] * inv_freq, col[:, None] * inv_freq], axis=-1)
    return jnp.cos(ang), jnp.sin(ang)


def apply_rope(x, cos, sin):
    xf = x.astype(jnp.float32)
    x1, x2 = xf[..., 0::2], xf[..., 1::2]
    c, s = cos[None, :, None, :], sin[None, :, None, :]
    out = jnp.stack([x1 * c - x2 * s, x1 * s + x2 * c], axis=-1)
    return out.reshape(x.shape).astype(x.dtype)


def neighbourhood_attention(q, k, v, rel_bias, rows):
    B, S, H, _ = q.shape
    win_r = min(NA_ROWS, rows)
    qg = q.reshape(B, rows, GRID_W, H, HD)
    kg = k.reshape(B, rows, GRID_W, H, HD)
    vg = v.reshape(B, rows, GRID_W, H, HD)
    cols = jnp.arange(GRID_W)
    c0 = jnp.clip(cols - NA_COLS // 2, 0, GRID_W - NA_COLS)
    col_idx = c0[:, None] + jnp.arange(NA_COLS)[None, :]
    col_off = col_idx - cols[:, None] + (NA_COLS - 1)

    def row_block(r):
        r0 = jnp.clip(r - NA_ROWS // 2, 0, rows - win_r)
        q_r = lax.dynamic_index_in_dim(qg, r, axis=1, keepdims=False)
        k_r = lax.dynamic_slice_in_dim(kg, r0, win_r, axis=1)
        v_r = lax.dynamic_slice_in_dim(vg, r0, win_r, axis=1)
        k_nb = k_r[:, :, col_idx]
        v_nb = v_r[:, :, col_idx]
        s = jnp.einsum('bchd,bicjhd->bhcij', q_r, k_nb, preferred_element_type=jnp.float32)
        row_off = r0 + jnp.arange(win_r) - r + (NA_ROWS - 1)
        bias = rel_bias.astype(jnp.float32)[:, row_off][:, :, col_off]
        s = s + bias.transpose(0, 2, 1, 3)[None]
        p = jax.nn.softmax(s.reshape(B, H, GRID_W, win_r * NA_COLS), axis=-1)
        p = p.reshape(B, H, GRID_W, win_r, NA_COLS).astype(v.dtype)
        return jnp.einsum('bhcij,bicjhd->bchd', p, v_nb)

    o = lax.map(row_block, jnp.arange(rows))
    return o.transpose(1, 0, 2, 3, 4).reshape(B, S, H * HD)


def gqa_block_attention(q, k, v):
    B, S, Hq, _ = q.shape
    Hkv = k.shape[2]
    G = Hq // Hkv
    nblk = S // Q_BLOCK
    qb = q.reshape(B, nblk, Q_BLOCK, Hkv, G, HD).swapaxes(0, 1)

    def block(qi):
        s = jnp.einsum('bqhgd,bkhd->bhgqk', qi, k, preferred_element_type=jnp.float32)
        p = jax.nn.softmax(s, axis=-1).astype(v.dtype)
        return jnp.einsum('bhgqk,bkhd->bqhgd', p, v)

    o = lax.map(block, qb)
    return o.swapaxes(0, 1).reshape(B, S, Hq * HD)


def diff_block_attention(q, k, v, lam, slopes):
    B, S, H = q.shape[:3]
    nblk = S // Q_BLOCK
    kpos = jnp.arange(S, dtype=jnp.float32)
    qb = q.reshape(B, nblk, Q_BLOCK, H, 2, HD).swapaxes(0, 1)

    def block(args):
        qi, i = args
        qpos = (i * Q_BLOCK + jnp.arange(Q_BLOCK)).astype(jnp.float32)
        dist = jnp.abs(qpos[:, None] - kpos[None, :])
        s = jnp.einsum('bqhmd,bkhmd->bhmqk', qi, k, preferred_element_type=jnp.float32)
        s = s - slopes[None, :, None, None, None] * dist
        p = jax.nn.softmax(s, axis=-1)
        a = (p[:, :, 0] - lam * p[:, :, 1]).astype(v.dtype)
        return jnp.einsum('bhqk,bkhe->bqhe', a, v)

    o = lax.map(block, (qb, jnp.arange(nblk)))
    return o.swapaxes(0, 1).reshape(B, S, H, 2 * HD)


def dilated_attention(q, k, v, slopes):
    B, S = q.shape[:2]
    nblk = S // Q_BLOCK
    outs, lses = [], []
    for g, (window, dilation) in enumerate(D_PATTERNS):
        n_side = window // (2 * dilation)
        offs = dilation * jnp.arange(-n_side, n_side + 1)
        penalty = slopes[g][:, None, None] * jnp.abs(offs).astype(jnp.float32)
        kg, vg = k[:, :, g], v[:, :, g]
        qb = q[:, :, g].reshape(B, nblk, Q_BLOCK, D_HEADS_PER_GROUP, HD).swapaxes(0, 1)

        def block(args, kg=kg, vg=vg, offs=offs, penalty=penalty):
            qi, i = args
            t = i * Q_BLOCK + jnp.arange(Q_BLOCK)
            idx = t[:, None] + offs[None, :]
            valid = (idx >= 0) & (idx < S)
            idx = jnp.clip(idx, 0, S - 1)
            kn, vn = kg[:, idx], vg[:, idx]
            s = jnp.einsum('bqhd,bqkhd->bhqk', qi, kn, preferred_element_type=jnp.float32) - penalty[None]
            s = jnp.where(valid[None, None], s, NEG_INF)
            lse = jax.nn.logsumexp(s, axis=-1)
            p = jnp.exp(s - lse[..., None]).astype(vn.dtype)
            return jnp.einsum('bhqk,bqkhd->bqhd', p, vn), lse

        o, lse = lax.map(block, (qb, jnp.arange(nblk)))
        outs.append(o.swapaxes(0, 1).reshape(B, S, D_HEADS_PER_GROUP, HD))
        lses.append(lse.transpose(1, 0, 3, 2).reshape(B, S, D_HEADS_PER_GROUP))
    o = jnp.stack(outs, axis=0).astype(jnp.float32)
    w = jax.nn.softmax(jnp.stack(lses, axis=0), axis=0)
    out = jnp.sum(w[..., None] * o, axis=0)
    return out.astype(q.dtype).reshape(B, S, D_OW)


def hybrid_layer(x, layer_idx, norm_g, w_in, qk_gain, na_rel_bias, diff_lambda, diff_subln_g,
                 w_branch_a, w_branch_b, w_branch_c, w_branch_d, w_out):
    B, S, _ = x.shape
    rows = S // GRID_W
    scale = HD ** -0.5
    xn = rms_norm(x, norm_g)
    proj = jnp.einsum('bsd,de->bse', xn, w_in)
    split_points = [int(p) for p in np.cumsum(SPLIT_WIDTHS)[:-1]]
    (a_q, a_k, a_v, a_z, b_q, b_k, b_v, b_z, c_q, c_k, c_v, c_z,
     d_q, d_k, d_v, d_z, g_a, g_b, g_c, g_d) = jnp.split(proj, split_points, axis=-1)

    qa = rms_norm(a_q.reshape(B, S, A_HEADS, HD), qk_gain[0, 0]) * scale
    ka = rms_norm(a_k.reshape(B, S, A_HEADS, HD), qk_gain[0, 1])
    va = a_v.reshape(B, S, A_HEADS, HD)
    y_a = neighbourhood_attention(qa, ka, va, na_rel_bias, rows) * jax.nn.silu(a_z)

    cos, sin = axial_rope(S)
    qb = apply_rope(rms_norm(b_q.reshape(B, S, B_Q_HEADS, HD), qk_gain[1, 0]), cos, sin) * scale
    kb = apply_rope(rms_norm(b_k.reshape(B, S, B_KV_HEADS, HD), qk_gain[1, 1]), cos, sin)
    vb = b_v.reshape(B, S, B_KV_HEADS, HD)
    y_b = gqa_block_attention(qb, kb, vb) * jax.nn.silu(b_z)

    lam_init = 0.8 - 0.6 * math.exp(-0.3 * layer_idx)
    lam_p = diff_lambda.astype(jnp.float32)
    lam = jnp.exp(jnp.sum(lam_p[0] * lam_p[1])) - jnp.exp(jnp.sum(lam_p[2] * lam_p[3])) + lam_init
    qc = rms_norm(c_q.reshape(B, S, C_HEADS, 2, HD), qk_gain[2, 0]) * scale
    kc = rms_norm(c_k.reshape(B, S, C_HEADS, 2, HD), qk_gain[2, 1])
    vc = c_v.reshape(B, S, C_HEADS, 2 * HD)
    oc = diff_block_attention(qc, kc, vc, lam, alibi_slopes(C_HEADS))
    oc = rms_norm(oc, diff_subln_g) * (1.0 - lam_init)
    y_c = oc.reshape(B, S, C_VW) * jax.nn.silu(c_z)

    d_slopes = alibi_slopes(D_GROUPS * D_HEADS_PER_GROUP).reshape(D_GROUPS, D_HEADS_PER_GROUP)
    qd = rms_norm(d_q.reshape(B, S, D_GROUPS, D_HEADS_PER_GROUP, HD), qk_gain[3, 0]) * scale
    kd = rms_norm(d_k.reshape(B, S, D_GROUPS, D_HEADS_PER_GROUP, HD), qk_gain[3, 1])
    vd = d_v.reshape(B, S, D_GROUPS, D_HEADS_PER_GROUP, HD)
    y_d = dilated_attention(qd, kd, vd, d_slopes) * jax.nn.silu(d_z)

    merged = (jax.nn.sigmoid(g_a) * jnp.einsum('bse,ed->bsd', y_a, w_branch_a)
              + jax.nn.sigmoid(g_b) * jnp.einsum('bse,ed->bsd', y_b, w_branch_b)
              + jax.nn.sigmoid(g_c) * jnp.einsum('bse,ed->bsd', y_c, w_branch_c)
              + jax.nn.sigmoid(g_d) * jnp.einsum('bse,ed->bsd', y_d, w_branch_d))
    return x + jnp.einsum('bsd,de->bse', merged, w_out)


def setup_inputs(seed: int = 0) -> dict:
    key = jax.random.key(seed)
    ks = jax.random.split(key, 12)
    f32 = jnp.float32
    nrm = jax.random.normal
    return {
        'x': nrm(ks[0], (BATCH, SEQ, D_MODEL), f32),
        'norm_g': 1.0 + 0.01 * nrm(ks[1], (DEPTH, D_MODEL), f32),
        'w_in': nrm(ks[2], (DEPTH, D_MODEL, N_IN), f32) * D_MODEL ** -0.5,
        'qk_gain': 1.0 + 0.01 * nrm(ks[3], (DEPTH, N_BRANCHES, 2, HD), f32),
        'na_rel_bias': 0.1 * nrm(ks[4], (DEPTH, A_HEADS, 2 * NA_ROWS - 1, 2 * NA_COLS - 1), f32),
        'diff_lambda': 0.1 * nrm(ks[5], (DEPTH, 4, HD), f32),
        'diff_subln_g': 1.0 + 0.01 * nrm(ks[6], (DEPTH, 2 * HD), f32),
        'w_branch_a': nrm(ks[7], (DEPTH, A_W, D_MODEL), f32) * A_W ** -0.5,
        'w_branch_b': nrm(ks[8], (DEPTH, B_QW, D_MODEL), f32) * B_QW ** -0.5,
        'w_branch_c': nrm(ks[9], (DEPTH, C_VW, D_MODEL), f32) * C_VW ** -0.5,
        'w_branch_d': nrm(ks[10], (DEPTH, D_OW, D_MODEL), f32) * D_OW ** -0.5,
        'w_out': nrm(ks[11], (DEPTH, D_MODEL, D_MODEL), f32) * D_MODEL ** -0.5,
    }


def reference(x, norm_g, w_in, qk_gain, na_rel_bias, diff_lambda, diff_subln_g,
              w_branch_a, w_branch_b, w_branch_c, w_branch_d, w_out):
    for l in range(DEPTH):
        x = hybrid_layer(x, l, norm_g[l], w_in[l], qk_gain[l], na_rel_bias[l], diff_lambda[l],
                         diff_subln_g[l], w_branch_a[l], w_branch_b[l], w_branch_c[l],
                         w_branch_d[l], w_out[l])
    return x
```

```python
import functools
import math

import jax
import jax.numpy as jnp
import numpy as np
from jax import lax
from jax.experimental import pallas as pl
from jax.experimental.pallas import tpu as pltpu

HD = 128
GRID_W = 64
NA_ROWS, NA_COLS = 8, 16
A_HEADS = 8
B_Q_HEADS, B_KV_HEADS = 8, 2
ROPE_THETA = 10000.0
C_HEADS = 4
D_PATTERNS = ((128, 1), (512, 4), (2048, 16))
D_GROUPS, D_HEADS_PER_GROUP = 3, 4
N_BRANCHES = 4
RMS_EPS = 1e-6
NEG_INF = -1e30

A_W = A_HEADS * HD
B_QW = B_Q_HEADS * HD
B_KVW = B_KV_HEADS * HD
C_QKW = C_HEADS * 2 * HD
C_VW = C_HEADS * 2 * HD
D_QKVW = D_GROUPS * D_HEADS_PER_GROUP * HD
D_OW = D_HEADS_PER_GROUP * HD
_SEG_NAMES = ("a_q", "a_k", "a_v", "a_z", "b_q", "b_k", "b_v", "b_z",
              "c_q", "c_k", "c_v", "c_z", "d_q", "d_k", "d_v", "d_z")
_SEG_WIDTHS = (A_W, A_W, A_W, A_W, B_QW, B_KVW, B_KVW, B_QW,
               C_QKW, C_QKW, C_VW, C_VW, D_QKVW, D_QKVW, D_QKVW, D_OW)
_SEG_START = dict(zip(_SEG_NAMES, np.cumsum((0,) + _SEG_WIDTHS[:-1]).tolist()))
GATE_START = int(sum(_SEG_WIDTHS))
Y_W = max(A_W, B_QW, C_VW, D_OW)

LANES = 128
V7X_VMEM_BYTES = 64 * 1024 * 1024
VMEM_LIMIT = 56 * 1024 * 1024
TM = 1024
TN = 512
TM_NORM = 256
TQ = 256
D_TQ, D_SPAN = 256, 512

MODE_PLAIN, MODE_NORM, MODE_NORM_ROPE, MODE_KV_B, MODE_SILU = range(5)

_F32 = jnp.float32
_BF16 = jnp.bfloat16


def _cparams(*sem):
    return pltpu.CompilerParams(dimension_semantics=sem, vmem_limit_bytes=VMEM_LIMIT)


def _sigmoid(v):
    return 1.0 / (1.0 + jnp.exp(-v))


def _qk_t(q, k):
    return lax.dot_general(q, k, (((1,), (1,)), ((), ())), preferred_element_type=_F32)


def _rmsnorm_kernel(x_ref, g_ref, o_ref):
    x = x_ref[...]
    ms = jnp.mean(x * x, axis=-1, keepdims=True)
    o_ref[...] = (x * lax.rsqrt(ms + RMS_EPS) * g_ref[...]).astype(o_ref.dtype)


def _rmsnorm(x, g):
    m, d = x.shape
    return pl.pallas_call(
        _rmsnorm_kernel,
        grid=(m // TM_NORM,),
        in_specs=[pl.BlockSpec((TM_NORM, d), lambda i: (i, 0)),
                  pl.BlockSpec((1, d), lambda i: (0, 0))],
        out_specs=pl.BlockSpec((TM_NORM, d), lambda i: (i, 0)),
        out_shape=jax.ShapeDtypeStruct((m, d), _BF16),
        compiler_params=_cparams("parallel"),
        name="rmsnorm",
    )(x, g.reshape(1, d))


def _proj_kernel(src_ref, mode_ref, x_ref, w_ref, g_ref, cos_ref, sin_ref, o_ref):
    del src_ref
    mode = mode_ref[pl.program_id(1)]
    acc = jnp.dot(x_ref[...], w_ref[...], preferred_element_type=_F32)
    n_heads = acc.shape[1] // HD

    def head(h):
        return acc[:, h * HD:(h + 1) * HD]

    def norm_head(h):
        blk = head(h)
        ms = jnp.mean(blk * blk, axis=-1, keepdims=True)
        return blk * lax.rsqrt(ms + RMS_EPS) * g_ref[:, h * HD:(h + 1) * HD]

    def rope(y):
        even = (lax.broadcasted_iota(jnp.int32, y.shape, 1) % 2) == 0
        partner = jnp.where(even, pltpu.roll(y, HD - 1, 1), pltpu.roll(y, 1, 1))
        return y * cos_ref[...] + partner * sin_ref[...]

    def put(h, val):
        o_ref[:, h * HD:(h + 1) * HD] = val.astype(o_ref.dtype)

    @pl.when(mode == MODE_PLAIN)
    def _():
        o_ref[...] = acc.astype(o_ref.dtype)

    @pl.when(mode == MODE_NORM)
    def _():
        for h in range(n_heads):
            put(h, norm_head(h))

    @pl.when(mode == MODE_NORM_ROPE)
    def _():
        for h in range(n_heads):
            put(h, rope(norm_head(h)))

    @pl.when(mode == MODE_KV_B)
    def _():
        for h in range(n_heads // 2):
            put(h, rope(norm_head(h)))
        for h in range(n_heads // 2, n_heads):
            put(h, head(h))

    @pl.when(mode == MODE_SILU)
    def _():
        o_ref[...] = (acc * _sigmoid(acc)).astype(o_ref.dtype)


def _projection(xn, w_b, layer, src_tiles, modes, gains, cos_t, sin_t, out_dtype, seq):
    m, d = xn.shape
    n_t = src_tiles.shape[0]
    rope_blocks = seq // TM
    return pl.pallas_call(
        _proj_kernel,
        grid_spec=pltpu.PrefetchScalarGridSpec(
            num_scalar_prefetch=2,
            grid=(m // TM, n_t),
            in_specs=[
                pl.BlockSpec((TM, d), lambda i, j, src, md: (i, 0)),
                pl.BlockSpec((None, d, TN), lambda i, j, src, md: (layer, 0, src[j])),
                pl.BlockSpec((None, 1, TN), lambda i, j, src, md: (j, 0, 0)),
                pl.BlockSpec((TM, HD), lambda i, j, src, md: (i % rope_blocks, 0)),
                pl.BlockSpec((TM, HD), lambda i, j, src, md: (i % rope_blocks, 0)),
            ],
            out_specs=pl.BlockSpec((TM, TN), lambda i, j, src, md: (i, j)),
        ),
        out_shape=jax.ShapeDtypeStruct((m, n_t * TN), out_dtype),
        compiler_params=_cparams("parallel", "arbitrary"),
        name="projection",
    )(src_tiles, modes, xn, w_b, gains, cos_t, sin_t)


def _tile_plan(segments):
    src, modes, gain_keys, out_start = [], [], [], {}
    col = 0
    for name, mode, gain_key in segments:
        if name == "b_kv":
            start, width = _SEG_START["b_k"], 2 * B_KVW
            assert width == TN and _SEG_START["b_v"] == start + B_KVW
            out_start["b_k"], out_start["b_v"] = col, col + B_KVW
        else:
            start, width = _SEG_START[name], _SEG_WIDTHS[_SEG_NAMES.index(name)]
            out_start[name] = col
        assert start % TN == 0 and width % TN == 0, (name, start, width)
        for t in range(width // TN):
            src.append(start // TN + t)
            modes.append(mode)
            gain_keys.append(gain_key)
        col += width
    return np.asarray(src, np.int32), np.asarray(modes, np.int32), gain_keys, out_start


_SCALE = HD ** -0.5
_PLAN_BF16 = _tile_plan((
    ("a_q", MODE_NORM, (0, 0)), ("a_k", MODE_NORM, (0, 1)), ("a_v", MODE_PLAIN, None),
    ("b_q", MODE_NORM_ROPE, (1, 0)), ("b_kv", MODE_KV_B, (1, 1)),
    ("c_q", MODE_NORM, (2, 0)), ("c_k", MODE_NORM, (2, 1)), ("c_v", MODE_PLAIN, None)))
_PLAN_F32 = _tile_plan((
    ("a_z", MODE_SILU, None), ("b_z", MODE_SILU, None), ("c_z", MODE_SILU, None),
    ("d_q", MODE_NORM, (3, 0)), ("d_k", MODE_NORM, (3, 1)), ("d_v", MODE_PLAIN, None),
    ("d_z", MODE_SILU, None)))


def _gain_table(qk_gain, gain_keys):
    rows = []
    for key in gain_keys:
        if key is None:
            rows.append(jnp.ones((TN,), _F32))
        else:
            g = qk_gain[key[0], key[1]].astype(_F32)
            if key[1] == 0:
                g = g * _SCALE
            rows.append(jnp.tile(g, TN // HD))
    return jnp.stack(rows)[:, None, :]


def _rope_tables(seq):
    t = jnp.arange(seq)
    row = (t // GRID_W).astype(_F32)
    col = (t % GRID_W).astype(_F32)
    n_pairs = HD // 4
    inv_freq = ROPE_THETA ** (-jnp.arange(n_pairs, dtype=_F32) / n_pairs)
    ang = jnp.concatenate([row[:, None] * inv_freq, col[:, None] * inv_freq], axis=-1)
    cos, sin = jnp.cos(ang), jnp.sin(ang)
    cos_t = jnp.repeat(cos, 2, axis=-1)
    sin_t = jnp.stack([-sin, sin], axis=-1).reshape(seq, HD)
    return cos_t, sin_t


def _na_bias_kernel(rb_ref, o_ref):
    h, dl = pl.program_id(0), pl.program_id(1)
    n_row_off, n_col_off = 2 * NA_ROWS - 1, 2 * NA_COLS - 1
    c = lax.broadcasted_iota(jnp.int32, (GRID_W, GRID_W), 0)
    kc = lax.broadcasted_iota(jnp.int32, (GRID_W, GRID_W), 1)
    c0 = jnp.clip(c - NA_COLS // 2, 0, GRID_W - NA_COLS)
    valid = (kc >= c0) & (kc < c0 + NA_COLS)
    col_off = kc - c + (NA_COLS - 1)
    for i in range(NA_ROWS):
        row_off = i - dl + (NA_ROWS - 1)
        blk = jnp.full((GRID_W, GRID_W), NEG_INF, _F32)
        for d in range(n_col_off):
            blk = jnp.where(col_off == d, rb_ref[(h * n_row_off + row_off) * n_col_off + d], blk)
        o_ref[:, i * GRID_W:(i + 1) * GRID_W] = jnp.where(valid, blk, NEG_INF)


def _na_bias_table(rel_bias):
    heads = rel_bias.shape[0]
    return pl.pallas_call(
        _na_bias_kernel,
        grid=(heads, NA_ROWS),
        in_specs=[pl.BlockSpec(memory_space=pltpu.SMEM)],
        out_specs=pl.BlockSpec((None, None, GRID_W, NA_ROWS * GRID_W), lambda h, dl: (h, dl, 0, 0)),
        out_shape=jax.ShapeDtypeStruct((heads, NA_ROWS, GRID_W, NA_ROWS * GRID_W), _F32),
        compiler_params=_cparams("parallel", "parallel"),
        name="na_bias_table",
    )(rel_bias.astype(_F32).reshape(-1))


def _na_kernel(q_ref, k_ref, v_ref, z_ref, bias_ref, o_ref, *, rows):
    win = NA_ROWS * GRID_W

    def row_block(r, carry):
        r0 = jnp.clip(r - NA_ROWS // 2, 0, rows - NA_ROWS)
        q0 = pl.multiple_of(r * GRID_W, GRID_W)
        k0 = pl.multiple_of(r0 * GRID_W, GRID_W)
        q = q_ref[pl.ds(q0, GRID_W), :]
        s = _qk_t(q, k_ref[pl.ds(k0, win), :]) + bias_ref[r - r0]
        m = jnp.max(s, axis=-1, keepdims=True)
        p = jnp.exp(s - m)
        l = jnp.sum(p, axis=-1, keepdims=True)
        o = jnp.dot(p.astype(_BF16), v_ref[pl.ds(k0, win), :], preferred_element_type=_F32) / l
        o_ref[pl.ds(q0, GRID_W), :] = (o * z_ref[pl.ds(q0, GRID_W), :]).astype(o_ref.dtype)
        return carry

    lax.fori_loop(0, rows, row_block, 0)


def _neighbourhood(qkv, zf, bias_tab, batch, seq, off_q, off_z):
    rows = seq // GRID_W
    assert rows >= NA_ROWS
    blk = lambda off: pl.BlockSpec((seq, HD), lambda b, h: (b, off // HD + h))
    return pl.pallas_call(
        functools.partial(_na_kernel, rows=rows),
        grid=(batch, A_HEADS),
        in_specs=[blk(off_q["a_q"]), blk(off_q["a_k"]), blk(off_q["a_v"]), blk(off_z["a_z"]),
                  pl.BlockSpec((None, NA_ROWS, GRID_W, NA_ROWS * GRID_W), lambda b, h: (h, 0, 0, 0))],
        out_specs=pl.BlockSpec((seq, HD), lambda b, h: (b, h)),
        out_shape=jax.ShapeDtypeStruct((batch * seq, A_W), _BF16),
        compiler_params=_cparams("parallel", "parallel"),
        name="neighbourhood_attention",
    )(qkv, qkv, qkv, zf, bias_tab)


def _gqa_kernel(q_ref, k_ref, v_ref, z_ref, o_ref):
    k = k_ref[...]
    v = v_ref[...]
    for g in range(q_ref.shape[1] // HD):
        sl = slice(g * HD, (g + 1) * HD)
        s = _qk_t(q_ref[:, sl], k)
        m = jnp.max(s, axis=-1, keepdims=True)
        p = jnp.exp(s - m)
        l = jnp.sum(p, axis=-1, keepdims=True)
        o = jnp.dot(p.astype(_BF16), v, preferred_element_type=_F32) / l
        o_ref[:, sl] = (o * z_ref[:, sl]).astype(o_ref.dtype)


def _gqa(qkv, zf, batch, seq, off_q, off_z):
    group_w = (B_Q_HEADS // B_KV_HEADS) * HD
    nq = seq // TQ
    q_spec = lambda off: pl.BlockSpec((TQ, group_w), lambda b, hk, i: (b * nq + i, off // group_w + hk))
    kv_spec = lambda off: pl.BlockSpec((seq, HD), lambda b, hk, i: (b, off // HD + hk))
    return pl.pallas_call(
        _gqa_kernel,
        grid=(batch, B_KV_HEADS, nq),
        in_specs=[q_spec(off_q["b_q"]), kv_spec(off_q["b_k"]), kv_spec(off_q["b_v"]), q_spec(off_z["b_z"])],
        out_specs=pl.BlockSpec((TQ, group_w), lambda b, hk, i: (b * nq + i, hk)),
        out_shape=jax.ShapeDtypeStruct((batch * seq, B_QW), _BF16),
        compiler_params=_cparams("parallel", "parallel", "parallel"),
        name="gqa_attention",
    )(qkv, qkv, qkv, zf)


def _diff_kernel(slope_ref, q_ref, k_ref, v_ref, z_ref, lam_ref, g_ref, o_ref, dist_ref, *, lam_init):
    i, h = pl.program_id(1), pl.program_id(2)
    tq, seq = dist_ref.shape

    @pl.when(h == 0)
    def _():
        qpos = i * tq + lax.broadcasted_iota(jnp.int32, (tq, seq), 0)
        kpos = lax.broadcasted_iota(jnp.int32, (tq, seq), 1)
        dist_ref[...] = jnp.abs(qpos - kpos).astype(_F32)

    lp = lam_ref[...]
    lam = (jnp.exp(jnp.sum(lp[0:1] * lp[1:2], axis=-1, keepdims=True))
           - jnp.exp(jnp.sum(lp[2:3] * lp[3:4], axis=-1, keepdims=True)) + lam_init)
    pen = slope_ref[h] * dist_ref[...]

    def softmax_map(mi):
        sl = slice(mi * HD, (mi + 1) * HD)
        s = _qk_t(q_ref[:, sl], k_ref[:, sl]) - pen
        m = jnp.max(s, axis=-1, keepdims=True)
        p = jnp.exp(s - m)
        return p, jnp.sum(p, axis=-1, keepdims=True)

    p0, l0 = softmax_map(0)
    p1, l1 = softmax_map(1)
    a = p0 * (1.0 / l0) - p1 * (lam / l1)
    o = jnp.dot(a.astype(_BF16), v_ref[...], preferred_element_type=_F32)
    ms = jnp.mean(o * o, axis=-1, keepdims=True)
    o = o * lax.rsqrt(ms + RMS_EPS) * g_ref[...] * (1.0 - lam_init)
    o_ref[...] = (o * z_ref[...]).astype(o_ref.dtype)


def _diff_attention(qkv, zf, diff_lambda, subln_g, lam_init, batch, seq, off_q, off_z):
    hw = 2 * HD
    nq = seq // TQ
    slopes = jnp.asarray(2.0 ** (-8.0 * np.arange(1, C_HEADS + 1) / C_HEADS), dtype=_F32)
    q_spec = lambda off: pl.BlockSpec((TQ, hw), lambda b, i, h, sl: (b * nq + i, off // hw + h))
    kv_spec = lambda off: pl.BlockSpec((seq, hw), lambda b, i, h, sl: (b, off // hw + h))
    return pl.pallas_call(
        functools.partial(_diff_kernel, lam_init=lam_init),
        grid_spec=pltpu.PrefetchScalarGridSpec(
            num_scalar_prefetch=1,
            grid=(batch, nq, C_HEADS),
            in_specs=[q_spec(off_q["c_q"]), kv_spec(off_q["c_k"]), kv_spec(off_q["c_v"]), q_spec(off_z["c_z"]),
                      pl.BlockSpec((4, HD), lambda b, i, h, sl: (0, 0)),
                      pl.BlockSpec((1, hw), lambda b, i, h, sl: (0, 0))],
            out_specs=pl.BlockSpec((TQ, hw), lambda b, i, h, sl: (b * nq + i, h)),
            scratch_shapes=[pltpu.VMEM((TQ, seq), _F32)],
        ),
        out_shape=jax.ShapeDtypeStruct((batch * seq, C_VW), _BF16),
        compiler_params=_cparams("parallel", "parallel", "arbitrary"),
        name="diff_attention",
    )(slopes, qkv, qkv, qkv, zf, diff_lambda.astype(_F32), subln_g.astype(_F32).reshape(1, hw))


def _dilated_group(gi, window, dil, slope, q_ref, k_ref, v_ref, og_ref, lse_ref):
    seq = q_ref.shape[0]
    length = seq // dil
    tq = min(D_TQ, length)
    span = min(D_SPAN, length)
    n_tiles = length // tq
    n_side = window // (2 * dil)
    unit_pen = slope * float(dil)
    rel0 = (lax.broadcasted_iota(jnp.int32, (tq, span), 1)
            - lax.broadcasted_iota(jnp.int32, (tq, span), 0))

    def rows(start, size):
        return pl.ds(start, size) if dil == 1 else pl.ds(start, size, stride=dil)

    def tile(idx, carry):
        r = lax.div(idx, n_tiles)
        u0 = lax.rem(idx, n_tiles) * tq
        ks = jnp.clip(u0 - (span - tq) // 2, 0, length - span)
        q_rows = rows(r + dil * u0, tq)
        k_rows = rows(r + dil * ks, span)
        s = _qk_t(q_ref[q_rows, :].astype(_BF16), k_ref[k_rows, :].astype(_BF16))
        rel = jnp.abs(rel0 + (ks - u0))
        s = jnp.where(rel <= n_side, s - unit_pen * rel.astype(_F32), NEG_INF)
        m = jnp.max(s, axis=-1, keepdims=True)
        p = jnp.exp(s - m)
        l = jnp.sum(p, axis=-1, keepdims=True)
        o = jnp.dot(p.astype(_BF16), v_ref[k_rows, :].astype(_BF16), preferred_element_type=_F32) / l
        og_ref[gi, q_rows, :] = o
        lse_ref[gi, q_rows, :] = jnp.broadcast_to(m + jnp.log(l), (tq, LANES))
        return carry

    lax.fori_loop(0, dil * n_tiles, tile, 0)


def _dilated_kernel(slope_ref, q_ref, k_ref, v_ref, z_ref, o_ref, og_ref, lse_ref):
    h, g = pl.program_id(1), pl.program_id(2)
    for gi, (window, dil) in enumerate(D_PATTERNS):
        @pl.when(g == gi)
        def _(gi=gi, window=window, dil=dil):
            _dilated_group(gi, window, dil, slope_ref[gi * D_HEADS_PER_GROUP + h],
                           q_ref, k_ref, v_ref, og_ref, lse_ref)

    @pl.when(g == D_GROUPS - 1)
    def _():
        seq = o_ref.shape[0]
        chunk = min(512, seq)

        def merge(ci, carry):
            rs = pl.ds(pl.multiple_of(ci * chunk, chunk), chunk)
            lses = [lse_ref[gi, rs, :] for gi in range(D_GROUPS)]
            top = functools.reduce(jnp.maximum, lses)
            ws = [jnp.exp(v - top) for v in lses]
            num = functools.reduce(jnp.add, [w * og_ref[gi, rs, :] for gi, w in enumerate(ws)])
            out = num / functools.reduce(jnp.add, ws)
            o_ref[rs, :] = (out * z_ref[rs, :]).astype(o_ref.dtype)
            return carry

        lax.fori_loop(0, seq // chunk, merge, 0)


def _dilated(zf, batch, seq, off_z):
    n = D_GROUPS * D_HEADS_PER_GROUP
    slopes = jnp.asarray(2.0 ** (-8.0 * np.arange(1, n + 1) / n), dtype=_F32)
    for window, dil in D_PATTERNS:
        assert seq % dil == 0 and (seq // dil) % min(D_TQ, seq // dil) == 0
        assert D_SPAN - D_TQ >= 2 * (window // (2 * dil)) or seq // dil <= D_SPAN
    qkv_spec = lambda off: pl.BlockSpec(
        (seq, HD), lambda b, h, g, sl: (b, off // HD + g * D_HEADS_PER_GROUP + h))
    return pl.pallas_call(
        _dilated_kernel,
        grid_spec=pltpu.PrefetchScalarGridSpec(
            num_scalar_prefetch=1,
            grid=(batch, D_HEADS_PER_GROUP, D_GROUPS),
            in_specs=[qkv_spec(off_z["d_q"]), qkv_spec(off_z["d_k"]), qkv_spec(off_z["d_v"]),
                      pl.BlockSpec((seq, HD), lambda b, h, g, sl: (b, off_z["d_z"] // HD + h))],
            out_specs=pl.BlockSpec((seq, HD), lambda b, h, g, sl: (b, h)),
            scratch_shapes=[pltpu.VMEM((D_GROUPS, seq, HD), _F32),
                            pltpu.VMEM((D_GROUPS, seq, LANES), _F32)],
        ),
        out_shape=jax.ShapeDtypeStruct((batch * seq, D_OW), _BF16),
        compiler_params=_cparams("parallel", "parallel", "arbitrary"),
        name="dilated_attention",
    )(slopes, zf, zf, zf, zf)


def _merge_kernel(x_ref, wg_ref, y_ref, wb_ref, o_ref, acc_ref, *, branch_widths):
    br = pl.program_id(2)
    gate = _sigmoid(jnp.dot(x_ref[...], wg_ref[...], preferred_element_type=_F32))

    def add(kw):
        term = gate * jnp.dot(y_ref[:, :kw], wb_ref[:kw, :], preferred_element_type=_F32)

        @pl.when(br == 0)
        def _():
            acc_ref[...] = term

        @pl.when(br > 0)
        def _():
            acc_ref[...] += term

    for kw in sorted(set(branch_widths)):
        pred = functools.reduce(jnp.logical_or, [br == b for b, w in enumerate(branch_widths) if w == kw])
        pl.when(pred)(functools.partial(add, kw))

    @pl.when(br == len(branch_widths) - 1)
    def _():
        o_ref[...] = acc_ref[...].astype(o_ref.dtype)


def _gated_merge(xn, w_b, layer, y_stack, wb_stack, branch_widths):
    m, d = xn.shape
    assert GATE_START % TN == 0 and d % TN == 0
    n_br, _, yw = y_stack.shape
    assert wb_stack.shape[1:3] == (n_br, yw)
    gate_tile0 = GATE_START // TN
    tiles_per_branch = d // TN
    return pl.pallas_call(
        functools.partial(_merge_kernel, branch_widths=branch_widths),
        grid=(m // TM, d // TN, n_br),
        in_specs=[
            pl.BlockSpec((TM, d), lambda i, j, br: (i, 0)),
            pl.BlockSpec((None, d, TN), lambda i, j, br: (layer, 0, gate_tile0 + br * tiles_per_branch + j)),
            pl.BlockSpec((None, TM, yw), lambda i, j, br: (br, i, 0)),
            pl.BlockSpec((None, None, yw, TN), lambda i, j, br: (layer, br, 0, j)),
        ],
        out_specs=pl.BlockSpec((TM, TN), lambda i, j, br: (i, j)),
        out_shape=jax.ShapeDtypeStruct((m, d), _BF16),
        scratch_shapes=[pltpu.VMEM((TM, TN), _F32)],
        compiler_params=_cparams("parallel", "parallel", "arbitrary"),
        name="gated_merge",
    )(xn, w_b, y_stack, wb_stack)


def _out_kernel(m_ref, w_ref, x_ref, o_ref):
    o_ref[...] = x_ref[...] + jnp.dot(m_ref[...], w_ref[...], preferred_element_type=_F32)


def _out_projection(merged, w_out_b, layer, x):
    m, d = x.shape
    return pl.pallas_call(
        _out_kernel,
        grid=(m // TM, d // TN),
        in_specs=[pl.BlockSpec((TM, d), lambda i, j: (i, 0)),
                  pl.BlockSpec((None, d, TN), lambda i, j: (layer, 0, j)),
                  pl.BlockSpec((TM, TN), lambda i, j: (i, j))],
        out_specs=pl.BlockSpec((TM, TN), lambda i, j: (i, j)),
        out_shape=jax.ShapeDtypeStruct((m, d), _F32),
        compiler_params=_cparams("parallel", "arbitrary"),
        name="out_projection",
    )(merged, w_out_b, x)


def kernel(x, norm_g, w_in, qk_gain, na_rel_bias, diff_lambda, diff_subln_g,
           w_branch_a, w_branch_b, w_branch_c, w_branch_d, w_out):
    batch, seq, d_model = x.shape
    depth = w_in.shape[0]
    m = batch * seq
    assert w_in.shape[2] == GATE_START + N_BRANCHES * d_model
    assert m % TM == 0 and seq % TM == 0 and seq % TQ == 0 and seq % GRID_W == 0 and d_model % TN == 0

    w_in_b = w_in.astype(_BF16)
    w_out_b = w_out.astype(_BF16)
    branch_ws = (w_branch_a, w_branch_b, w_branch_c, w_branch_d)
    branch_widths = tuple(w.shape[1] for w in branch_ws)
    wb_stack = jnp.stack([jnp.pad(w.astype(_BF16), ((0, 0), (0, Y_W - w.shape[1]), (0, 0))) for w in branch_ws],
                         axis=1)
    cos_t, sin_t = _rope_tables(seq)
    src_b, modes_b, gkeys_b, off_q = _PLAN_BF16
    src_f, modes_f, gkeys_f, off_z = _PLAN_F32

    xf = x.reshape(m, d_model)
    for l in range(depth):
        xn = _rmsnorm(xf, norm_g[l])
        qkv = _projection(xn, w_in_b, l, jnp.asarray(src_b), jnp.asarray(modes_b),
                          _gain_table(qk_gain[l], gkeys_b), cos_t, sin_t, _BF16, seq)
        zf = _projection(xn, w_in_b, l, jnp.asarray(src_f), jnp.asarray(modes_f),
                         _gain_table(qk_gain[l], gkeys_f), cos_t, sin_t, _F32, seq)
        lam_init = 0.8 - 0.6 * math.exp(-0.3 * l)
        y_a = _neighbourhood(qkv, zf, _na_bias_table(na_rel_bias[l]), batch, seq, off_q, off_z)
        y_b = _gqa(qkv, zf, batch, seq, off_q, off_z)
        y_c = _diff_attention(qkv, zf, diff_lambda[l], diff_subln_g[l], lam_init, batch, seq, off_q, off_z)
        y_d = _dilated(zf, batch, seq, off_z)
        y_stack = jnp.stack([jnp.pad(y, ((0, 0), (0, Y_W - y.shape[1]))) for y in (y_a, y_b, y_c, y_d)])
        merged = _gated_merge(xn, w_in_b, l, y_stack, wb_stack, branch_widths)
        xf = _out_projection(merged, w_out_b, l, xf)
    return xf.reshape(batch, seq, d_model)
```

```python
import functools
import math

import jax
import jax.numpy as jnp
import numpy as np
from jax import lax
from jax.experimental import pallas as pl
from jax.experimental.pallas import tpu as pltpu

HD = 128
GRID_W = 64
NA_ROWS, NA_COLS = 8, 16
A_HEADS = 8
B_Q_HEADS, B_KV_HEADS = 8, 2
ROPE_THETA = 10000.0
C_HEADS = 4
D_PATTERNS = ((128, 1), (512, 4), (2048, 16))
D_GROUPS, D_HEADS_PER_GROUP = 3, 4
N_BRANCHES = 4
RMS_EPS = 1e-6
NEG_INF = -1e30

A_W = A_HEADS * HD
B_QW = B_Q_HEADS * HD
B_KVW = B_KV_HEADS * HD
C_QKW = C_HEADS * 2 * HD
C_VW = C_HEADS * 2 * HD
D_QKVW = D_GROUPS * D_HEADS_PER_GROUP * HD
D_OW = D_HEADS_PER_GROUP * HD
_SEG_NAMES = ("a_q", "a_k", "a_v", "a_z", "b_q", "b_k", "b_v", "b_z",
              "c_q", "c_k", "c_v", "c_z", "d_q", "d_k", "d_v", "d_z")
_SEG_WIDTHS = (A_W, A_W, A_W, A_W, B_QW, B_KVW, B_KVW, B_QW,
               C_QKW, C_QKW, C_VW, C_VW, D_QKVW, D_QKVW, D_QKVW, D_OW)
_SEG_START = dict(zip(_SEG_NAMES, np.cumsum((0,) + _SEG_WIDTHS[:-1]).tolist()))
GATE_START = int(sum(_SEG_WIDTHS))

LANES = 128
V7X_VMEM_BYTES = 64 * 1024 * 1024
VMEM_LIMIT = 56 * 1024 * 1024
TM = 1024
TN = 512
ROW_CHUNK = 256
TM_NORM = 256
TQ = 256
D_TQ, D_SPAN = 256, 512

MODE_PLAIN, MODE_NORM, MODE_NORM_ROPE, MODE_KV_B, MODE_SILU = range(5)

_F32 = jnp.float32
_BF16 = jnp.bfloat16


def _cparams(*sem):
    return pltpu.CompilerParams(dimension_semantics=sem, vmem_limit_bytes=VMEM_LIMIT)


def _sigmoid(v):
    return 0.5 * jnp.tanh(0.5 * v) + 0.5


def _qk_t(q, k):
    return lax.dot_general(q, k, (((1,), (1,)), ((), ())), preferred_element_type=_F32)


def _rmsnorm_kernel(x_ref, g_ref, o_ref):
    x = x_ref[...]
    ms = jnp.mean(x * x, axis=-1, keepdims=True)
    o_ref[...] = (x * lax.rsqrt(ms + RMS_EPS) * g_ref[...]).astype(o_ref.dtype)


def _rmsnorm(x, g):
    m, d = x.shape
    return pl.pallas_call(
        _rmsnorm_kernel,
        grid=(m // TM_NORM,),
        in_specs=[pl.BlockSpec((TM_NORM, d), lambda i: (i, 0)),
                  pl.BlockSpec((1, d), lambda i: (0, 0))],
        out_specs=pl.BlockSpec((TM_NORM, d), lambda i: (i, 0)),
        out_shape=jax.ShapeDtypeStruct((m, d), _BF16),
        compiler_params=_cparams("parallel"),
        name="rmsnorm",
    )(x, g.reshape(1, d))


def _proj_kernel(src_ref, mode_ref, x_ref, w_ref, g_ref, cos_ref, sin_ref, o_ref, *, modes_used):
    del src_ref
    mode = mode_ref[pl.program_id(1)]
    tm, tn = o_ref.shape
    n_heads = tn // HD

    def norm_head(acc, h):
        blk = acc[:, h * HD:(h + 1) * HD]
        ms = jnp.mean(blk * blk, axis=-1, keepdims=True)
        return blk * lax.rsqrt(ms + RMS_EPS) * g_ref[:, h * HD:(h + 1) * HD]

    def rope(y, rs):
        even = (lax.broadcasted_iota(jnp.int32, y.shape, 1) % 2) == 0
        partner = jnp.where(even, pltpu.roll(y, HD - 1, 1), pltpu.roll(y, 1, 1))
        return y * cos_ref[rs, :] + partner * sin_ref[rs, :]

    def head_epilogue(mode_id, acc, h, rs):
        is_key_half = h < n_heads // 2
        if mode_id == MODE_NORM:
            return norm_head(acc, h)
        if mode_id == MODE_NORM_ROPE or (mode_id == MODE_KV_B and is_key_half):
            return rope(norm_head(acc, h), rs)
        return acc[:, h * HD:(h + 1) * HD]

    def run(mode_id):
        for c in range(tm // ROW_CHUNK):
            rs = slice(c * ROW_CHUNK, (c + 1) * ROW_CHUNK)
            acc = jnp.dot(x_ref[rs, :], w_ref[...], preferred_element_type=_F32)
            if mode_id == MODE_PLAIN:
                o_ref[rs, :] = acc.astype(o_ref.dtype)
            elif mode_id == MODE_SILU:
                o_ref[rs, :] = (acc * _sigmoid(acc)).astype(o_ref.dtype)
            else:
                for h in range(n_heads):
                    o_ref[rs, h * HD:(h + 1) * HD] = head_epilogue(mode_id, acc, h, rs).astype(o_ref.dtype)

    for mode_id in modes_used:
        pl.when(mode == mode_id)(functools.partial(run, mode_id))


def _projection(xn, w_b, layer, src_tiles, modes, gains, cos_t, sin_t, out_dtype, seq, modes_used):
    m, d = xn.shape
    n_t = src_tiles.shape[0]
    rope_blocks = seq // TM
    return pl.pallas_call(
        functools.partial(_proj_kernel, modes_used=modes_used),
        grid_spec=pltpu.PrefetchScalarGridSpec(
            num_scalar_prefetch=2,
            grid=(m // TM, n_t),
            in_specs=[
                pl.BlockSpec((TM, d), lambda i, j, src, md: (i, 0)),
                pl.BlockSpec((None, d, TN), lambda i, j, src, md: (layer, 0, src[j])),
                pl.BlockSpec((None, 1, TN), lambda i, j, src, md: (j, 0, 0)),
                pl.BlockSpec((TM, HD), lambda i, j, src, md: (i % rope_blocks, 0)),
                pl.BlockSpec((TM, HD), lambda i, j, src, md: (i % rope_blocks, 0)),
            ],
            out_specs=pl.BlockSpec((TM, TN), lambda i, j, src, md: (i, j)),
        ),
        out_shape=jax.ShapeDtypeStruct((m, n_t * TN), out_dtype),
        compiler_params=_cparams("parallel", "arbitrary"),
        name="projection",
    )(src_tiles, modes, xn, w_b, gains, cos_t, sin_t)


def _tile_plan(segments):
    src, modes, gain_keys, out_start = [], [], [], {}
    col = 0
    for name, mode, gain_key in segments:
        if name == "b_kv":
            start, width = _SEG_START["b_k"], 2 * B_KVW
            assert width == TN and _SEG_START["b_v"] == start + B_KVW
            out_start["b_k"], out_start["b_v"] = col, col + B_KVW
        else:
            start, width = _SEG_START[name], _SEG_WIDTHS[_SEG_NAMES.index(name)]
            out_start[name] = col
        assert start % TN == 0 and width % TN == 0, (name, start, width)
        for t in range(width // TN):
            src.append(start // TN + t)
            modes.append(mode)
            gain_keys.append(gain_key)
        col += width
    return np.asarray(src, np.int32), np.asarray(modes, np.int32), gain_keys, out_start


_SCALE = HD ** -0.5
_PLAN_BF16 = _tile_plan((
    ("a_q", MODE_NORM, (0, 0)), ("a_k", MODE_NORM, (0, 1)), ("a_v", MODE_PLAIN, None),
    ("b_q", MODE_NORM_ROPE, (1, 0)), ("b_kv", MODE_KV_B, (1, 1)),
    ("c_q", MODE_NORM, (2, 0)), ("c_k", MODE_NORM, (2, 1)), ("c_v", MODE_PLAIN, None)))
_PLAN_F32 = _tile_plan((
    ("a_z", MODE_SILU, None), ("b_z", MODE_SILU, None), ("c_z", MODE_SILU, None),
    ("d_q", MODE_NORM, (3, 0)), ("d_k", MODE_NORM, (3, 1)), ("d_v", MODE_PLAIN, None),
    ("d_z", MODE_SILU, None)))


def _gain_table(qk_gain, gain_keys):
    rows = []
    for key in gain_keys:
        if key is None:
            rows.append(jnp.ones((TN,), _F32))
        else:
            g = qk_gain[key[0], key[1]].astype(_F32)
            if key[1] == 0:
                g = g * _SCALE
            rows.append(jnp.tile(g, TN // HD))
    return jnp.stack(rows)[:, None, :]


def _rope_tables(seq):
    t = jnp.arange(seq)
    row = (t // GRID_W).astype(_F32)
    col = (t % GRID_W).astype(_F32)
    n_pairs = HD // 4
    inv_freq = ROPE_THETA ** (-jnp.arange(n_pairs, dtype=_F32) / n_pairs)
    ang = jnp.concatenate([row[:, None] * inv_freq, col[:, None] * inv_freq], axis=-1)
    cos, sin = jnp.cos(ang), jnp.sin(ang)
    cos_t = jnp.repeat(cos, 2, axis=-1)
    sin_t = jnp.stack([-sin, sin], axis=-1).reshape(seq, HD)
    return cos_t, sin_t


def _na_bias_kernel(rb_ref, o_ref):
    h, dl = pl.program_id(0), pl.program_id(1)
    n_row_off, n_col_off = 2 * NA_ROWS - 1, 2 * NA_COLS - 1
    c = lax.broadcasted_iota(jnp.int32, (GRID_W, GRID_W), 0)
    kc = lax.broadcasted_iota(jnp.int32, (GRID_W, GRID_W), 1)
    c0 = jnp.clip(c - NA_COLS // 2, 0, GRID_W - NA_COLS)
    valid = (kc >= c0) & (kc < c0 + NA_COLS)
    col_off = kc - c + (NA_COLS - 1)
    for i in range(NA_ROWS):
        row_off = i - dl + (NA_ROWS - 1)
        blk = jnp.full((GRID_W, GRID_W), NEG_INF, _F32)
        for d in range(n_col_off):
            blk = jnp.where(col_off == d, rb_ref[(h * n_row_off + row_off) * n_col_off + d], blk)
        o_ref[:, i * GRID_W:(i + 1) * GRID_W] = jnp.where(valid, blk, NEG_INF)


def _na_bias_table(rel_bias):
    heads = rel_bias.shape[0]
    return pl.pallas_call(
        _na_bias_kernel,
        grid=(heads, NA_ROWS),
        in_specs=[pl.BlockSpec(memory_space=pltpu.SMEM)],
        out_specs=pl.BlockSpec((None, None, GRID_W, NA_ROWS * GRID_W), lambda h, dl: (h, dl, 0, 0)),
        out_shape=jax.ShapeDtypeStruct((heads, NA_ROWS, GRID_W, NA_ROWS * GRID_W), _F32),
        compiler_params=_cparams("parallel", "parallel"),
        name="na_bias_table",
    )(rel_bias.astype(_F32).reshape(-1))


def _na_kernel(q_ref, k_ref, v_ref, z_ref, bias_ref, o_ref, *, rows):
    win = NA_ROWS * GRID_W

    def row_block(r, carry):
        r0 = jnp.clip(r - NA_ROWS // 2, 0, rows - NA_ROWS)
        q0 = pl.multiple_of(r * GRID_W, GRID_W)
        k0 = pl.multiple_of(r0 * GRID_W, GRID_W)
        q = q_ref[pl.ds(q0, GRID_W), :]
        s = _qk_t(q, k_ref[pl.ds(k0, win), :]) + bias_ref[r - r0]
        m = jnp.max(s, axis=-1, keepdims=True)
        p = jnp.exp(s - m)
        l = jnp.sum(p, axis=-1, keepdims=True)
        o = jnp.dot(p.astype(_BF16), v_ref[pl.ds(k0, win), :], preferred_element_type=_F32) / l
        o_ref[pl.ds(q0, GRID_W), :] = (o * z_ref[pl.ds(q0, GRID_W), :]).astype(o_ref.dtype)
        return carry

    lax.fori_loop(0, rows, row_block, 0)


def _neighbourhood(qkv, zf, bias_tab, batch, seq, off_q, off_z):
    rows = seq // GRID_W
    assert rows >= NA_ROWS
    blk = lambda off: pl.BlockSpec((seq, HD), lambda b, h: (b, off // HD + h))
    return pl.pallas_call(
        functools.partial(_na_kernel, rows=rows),
        grid=(batch, A_HEADS),
        in_specs=[blk(off_q["a_q"]), blk(off_q["a_k"]), blk(off_q["a_v"]), blk(off_z["a_z"]),
                  pl.BlockSpec((None, NA_ROWS, GRID_W, NA_ROWS * GRID_W), lambda b, h: (h, 0, 0, 0))],
        out_specs=pl.BlockSpec((seq, HD), lambda b, h: (b, h)),
        out_shape=jax.ShapeDtypeStruct((batch * seq, A_W), _BF16),
        compiler_params=_cparams("parallel", "parallel"),
        name="neighbourhood_attention",
    )(qkv, qkv, qkv, zf, bias_tab)


def _gqa_kernel(q_ref, k_ref, v_ref, z_ref, o_ref):
    k = k_ref[...]
    v = v_ref[...]
    for g in range(q_ref.shape[1] // HD):
        sl = slice(g * HD, (g + 1) * HD)
        s = _qk_t(q_ref[:, sl], k)
        m = jnp.max(s, axis=-1, keepdims=True)
        p = jnp.exp(s - m)
        l = jnp.sum(p, axis=-1, keepdims=True)
        o = jnp.dot(p.astype(_BF16), v, preferred_element_type=_F32) / l
        o_ref[:, sl] = (o * z_ref[:, sl]).astype(o_ref.dtype)


def _gqa(qkv, zf, batch, seq, off_q, off_z):
    group_w = (B_Q_HEADS // B_KV_HEADS) * HD
    nq = seq // TQ
    q_spec = lambda off: pl.BlockSpec((TQ, group_w), lambda b, hk, i: (b * nq + i, off // group_w + hk))
    kv_spec = lambda off: pl.BlockSpec((seq, HD), lambda b, hk, i: (b, off // HD + hk))
    return pl.pallas_call(
        _gqa_kernel,
        grid=(batch, B_KV_HEADS, nq),
        in_specs=[q_spec(off_q["b_q"]), kv_spec(off_q["b_k"]), kv_spec(off_q["b_v"]), q_spec(off_z["b_z"])],
        out_specs=pl.BlockSpec((TQ, group_w), lambda b, hk, i: (b * nq + i, hk)),
        out_shape=jax.ShapeDtypeStruct((batch * seq, B_QW), _BF16),
        compiler_params=_cparams("parallel", "parallel", "parallel"),
        name="gqa_attention",
    )(qkv, qkv, qkv, zf)


def _diff_kernel(slope_ref, q_ref, k_ref, v_ref, z_ref, lam_ref, g_ref, o_ref, dist_ref, *, lam_init):
    i, h = pl.program_id(1), pl.program_id(2)
    tq, seq = dist_ref.shape

    @pl.when(h == 0)
    def _():
        qpos = i * tq + lax.broadcasted_iota(jnp.int32, (tq, seq), 0)
        kpos = lax.broadcasted_iota(jnp.int32, (tq, seq), 1)
        dist_ref[...] = jnp.abs(qpos - kpos).astype(_F32)

    lp = lam_ref[...]
    lam = (jnp.exp(jnp.sum(lp[0:1] * lp[1:2], axis=-1, keepdims=True))
           - jnp.exp(jnp.sum(lp[2:3] * lp[3:4], axis=-1, keepdims=True)) + lam_init)
    pen = slope_ref[h] * dist_ref[...]

    def softmax_map(mi):
        sl = slice(mi * HD, (mi + 1) * HD)
        s = _qk_t(q_ref[:, sl], k_ref[:, sl]) - pen
        m = jnp.max(s, axis=-1, keepdims=True)
        p = jnp.exp(s - m)
        return p, jnp.sum(p, axis=-1, keepdims=True)

    p0, l0 = softmax_map(0)
    p1, l1 = softmax_map(1)
    a = p0 * (1.0 / l0) - p1 * (lam / l1)
    o = jnp.dot(a.astype(_BF16), v_ref[...], preferred_element_type=_F32)
    ms = jnp.mean(o * o, axis=-1, keepdims=True)
    o = o * lax.rsqrt(ms + RMS_EPS) * g_ref[...] * (1.0 - lam_init)
    o_ref[...] = (o * z_ref[...]).astype(o_ref.dtype)


def _diff_attention(qkv, zf, diff_lambda, subln_g, lam_init, batch, seq, off_q, off_z):
    hw = 2 * HD
    nq = seq // TQ
    slopes = jnp.asarray(2.0 ** (-8.0 * np.arange(1, C_HEADS + 1) / C_HEADS), dtype=_F32)
    q_spec = lambda off: pl.BlockSpec((TQ, hw), lambda b, i, h, sl: (b * nq + i, off // hw + h))
    kv_spec = lambda off: pl.BlockSpec((seq, hw), lambda b, i, h, sl: (b, off // hw + h))
    return pl.pallas_call(
        functools.partial(_diff_kernel, lam_init=lam_init),
        grid_spec=pltpu.PrefetchScalarGridSpec(
            num_scalar_prefetch=1,
            grid=(batch, nq, C_HEADS),
            in_specs=[q_spec(off_q["c_q"]), kv_spec(off_q["c_k"]), kv_spec(off_q["c_v"]), q_spec(off_z["c_z"]),
                      pl.BlockSpec((4, HD), lambda b, i, h, sl: (0, 0)),
                      pl.BlockSpec((1, hw), lambda b, i, h, sl: (0, 0))],
            out_specs=pl.BlockSpec((TQ, hw), lambda b, i, h, sl: (b * nq + i, h)),
            scratch_shapes=[pltpu.VMEM((TQ, seq), _F32)],
        ),
        out_shape=jax.ShapeDtypeStruct((batch * seq, C_VW), _BF16),
        compiler_params=_cparams("parallel", "parallel", "arbitrary"),
        name="diff_attention",
    )(slopes, qkv, qkv, qkv, zf, diff_lambda.astype(_F32), subln_g.astype(_F32).reshape(1, hw))


def _dilated_group(gi, window, dil, slope, q_ref, k_ref, v_ref, og_ref, lse_ref):
    seq = q_ref.shape[0]
    length = seq // dil
    tq = min(D_TQ, length)
    span = min(D_SPAN, length)
    n_tiles = length // tq
    n_side = window // (2 * dil)
    unit_pen = slope * float(dil)
    rel0 = (lax.broadcasted_iota(jnp.int32, (tq, span), 1)
            - lax.broadcasted_iota(jnp.int32, (tq, span), 0))

    def rows(start, size):
        return pl.ds(start, size) if dil == 1 else pl.ds(start, size, stride=dil)

    def tile(idx, carry):
        r = lax.div(idx, n_tiles)
        u0 = lax.rem(idx, n_tiles) * tq
        ks = jnp.clip(u0 - (span - tq) // 2, 0, length - span)
        q_rows = rows(r + dil * u0, tq)
        k_rows = rows(r + dil * ks, span)
        s = _qk_t(q_ref[q_rows, :].astype(_BF16), k_ref[k_rows, :].astype(_BF16))
        rel = jnp.abs(rel0 + (ks - u0))
        s = jnp.where(rel <= n_side, s - unit_pen * rel.astype(_F32), NEG_INF)
        m = jnp.max(s, axis=-1, keepdims=True)
        p = jnp.exp(s - m)
        l = jnp.sum(p, axis=-1, keepdims=True)
        o = jnp.dot(p.astype(_BF16), v_ref[k_rows, :].astype(_BF16), preferred_element_type=_F32) / l
        og_ref[gi, q_rows, :] = o
        lse_ref[gi, q_rows, :] = jnp.broadcast_to(m + jnp.log(l), (tq, LANES))
        return carry

    lax.fori_loop(0, dil * n_tiles, tile, 0)


def _dilated_kernel(slope_ref, q_ref, k_ref, v_ref, z_ref, o_ref, og_ref, lse_ref):
    h, g = pl.program_id(1), pl.program_id(2)
    for gi, (window, dil) in enumerate(D_PATTERNS):
        @pl.when(g == gi)
        def _(gi=gi, window=window, dil=dil):
            _dilated_group(gi, window, dil, slope_ref[gi * D_HEADS_PER_GROUP + h],
                           q_ref, k_ref, v_ref, og_ref, lse_ref)

    @pl.when(g == D_GROUPS - 1)
    def _():
        seq = o_ref.shape[0]
        chunk = min(512, seq)

        def merge(ci, carry):
            rs = pl.ds(pl.multiple_of(ci * chunk, chunk), chunk)
            lses = [lse_ref[gi, rs, :] for gi in range(D_GROUPS)]
            top = functools.reduce(jnp.maximum, lses)
            ws = [jnp.exp(v - top) for v in lses]
            num = functools.reduce(jnp.add, [w * og_ref[gi, rs, :] for gi, w in enumerate(ws)])
            out = num / functools.reduce(jnp.add, ws)
            o_ref[rs, :] = (out * z_ref[rs, :]).astype(o_ref.dtype)
            return carry

        lax.fori_loop(0, seq // chunk, merge, 0)


def _dilated(zf, batch, seq, off_z):
    n = D_GROUPS * D_HEADS_PER_GROUP
    slopes = jnp.asarray(2.0 ** (-8.0 * np.arange(1, n + 1) / n), dtype=_F32)
    for window, dil in D_PATTERNS:
        assert seq % dil == 0 and (seq // dil) % min(D_TQ, seq // dil) == 0
        assert D_SPAN - D_TQ >= 2 * (window // (2 * dil)) or seq // dil <= D_SPAN
    qkv_spec = lambda off: pl.BlockSpec(
        (seq, HD), lambda b, h, g, sl: (b, off // HD + g * D_HEADS_PER_GROUP + h))
    return pl.pallas_call(
        _dilated_kernel,
        grid_spec=pltpu.PrefetchScalarGridSpec(
            num_scalar_prefetch=1,
            grid=(batch, D_HEADS_PER_GROUP, D_GROUPS),
            in_specs=[qkv_spec(off_z["d_q"]), qkv_spec(off_z["d_k"]), qkv_spec(off_z["d_v"]),
                      pl.BlockSpec((seq, HD), lambda b, h, g, sl: (b, off_z["d_z"] // HD + h))],
            out_specs=pl.BlockSpec((seq, HD), lambda b, h, g, sl: (b, h)),
            scratch_shapes=[pltpu.VMEM((D_GROUPS, seq, HD), _F32),
                            pltpu.VMEM((D_GROUPS, seq, LANES), _F32)],
        ),
        out_shape=jax.ShapeDtypeStruct((batch * seq, D_OW), _BF16),
        compiler_params=_cparams("parallel", "parallel", "arbitrary"),
        name="dilated_attention",
    )(slopes, zf, zf, zf, zf)


def _merge_kernel(x_ref, wg_ref, *refs, n_br):
    y_refs, wb_refs = refs[:n_br], refs[n_br:2 * n_br]
    o_ref, acc_ref = refs[2 * n_br:]
    br = pl.program_id(2)
    tm = x_ref.shape[0]

    @pl.when(br == 0)
    def _():
        acc_ref[...] = jnp.zeros_like(acc_ref)

    def add_branch(y_ref, wb_ref):
        for c in range(tm // ROW_CHUNK):
            rs = slice(c * ROW_CHUNK, (c + 1) * ROW_CHUNK)
            gate = _sigmoid(jnp.dot(x_ref[rs, :], wg_ref[...], preferred_element_type=_F32))
            acc_ref[rs, :] += gate * jnp.dot(y_ref[rs, :], wb_ref[...], preferred_element_type=_F32)

    for b in range(n_br):
        pl.when(br == b)(functools.partial(add_branch, y_refs[b], wb_refs[b]))

    @pl.when(br == n_br - 1)
    def _():
        o_ref[...] = acc_ref[...].astype(o_ref.dtype)


def _gated_merge(xn, w_b, layer, ys, wbs):
    m, d = xn.shape
    n_br = len(ys)
    assert GATE_START % TN == 0 and d % TN == 0
    gate_tile0 = GATE_START // TN
    tiles_per_branch = d // TN
    y_specs = [pl.BlockSpec((TM, y.shape[1]), lambda i, j, br: (i, 0)) for y in ys]
    wb_specs = [pl.BlockSpec((None, w.shape[1], TN), lambda i, j, br: (layer, 0, j)) for w in wbs]
    return pl.pallas_call(
        functools.partial(_merge_kernel, n_br=n_br),
        grid=(m // TM, d // TN, n_br),
        in_specs=[
            pl.BlockSpec((TM, d), lambda i, j, br: (i, 0)),
            pl.BlockSpec((None, d, TN), lambda i, j, br: (layer, 0, gate_tile0 + br * tiles_per_branch + j)),
            *y_specs, *wb_specs,
        ],
        out_specs=pl.BlockSpec((TM, TN), lambda i, j, br: (i, j)),
        out_shape=jax.ShapeDtypeStruct((m, d), _BF16),
        scratch_shapes=[pltpu.VMEM((TM, TN), _F32)],
        compiler_params=_cparams("parallel", "parallel", "arbitrary"),
        name="gated_merge",
    )(xn, w_b, *ys, *wbs)


def _out_kernel(m_ref, w_ref, x_ref, o_ref):
    for c in range(o_ref.shape[0] // ROW_CHUNK):
        rs = slice(c * ROW_CHUNK, (c + 1) * ROW_CHUNK)
        o_ref[rs, :] = x_ref[rs, :] + jnp.dot(m_ref[rs, :], w_ref[...], preferred_element_type=_F32)


def _out_projection(merged, w_out_b, layer, x):
    m, d = x.shape
    return pl.pallas_call(
        _out_kernel,
        grid=(m // TM, d // TN),
        in_specs=[pl.BlockSpec((TM, d), lambda i, j: (i, 0)),
                  pl.BlockSpec((None, d, TN), lambda i, j: (layer, 0, j)),
                  pl.BlockSpec((TM, TN), lambda i, j: (i, j))],
        out_specs=pl.BlockSpec((TM, TN), lambda i, j: (i, j)),
        out_shape=jax.ShapeDtypeStruct((m, d), _F32),
        compiler_params=_cparams("parallel", "arbitrary"),
        name="out_projection",
    )(merged, w_out_b, x)


def kernel(x, norm_g, w_in, qk_gain, na_rel_bias, diff_lambda, diff_subln_g,
           w_branch_a, w_branch_b, w_branch_c, w_branch_d, w_out):
    batch, seq, d_model = x.shape
    depth = w_in.shape[0]
    m = batch * seq
    assert w_in.shape[2] == GATE_START + N_BRANCHES * d_model
    assert m % TM == 0 and seq % TM == 0 and seq % TQ == 0 and seq % GRID_W == 0 and d_model % TN == 0

    w_in_b = w_in.astype(_BF16)
    w_out_b = w_out.astype(_BF16)
    wbs = tuple(w.astype(_BF16) for w in (w_branch_a, w_branch_b, w_branch_c, w_branch_d))
    cos_t, sin_t = _rope_tables(seq)
    src_b, modes_b, gkeys_b, off_q = _PLAN_BF16
    src_f, modes_f, gkeys_f, off_z = _PLAN_F32

    xf = x.reshape(m, d_model)
    for l in range(depth):
        xn = _rmsnorm(xf, norm_g[l])
        qkv = _projection(xn, w_in_b, l, jnp.asarray(src_b), jnp.asarray(modes_b),
                          _gain_table(qk_gain[l], gkeys_b), cos_t, sin_t, _BF16, seq,
                          tuple(sorted(set(modes_b.tolist()))))
        zf = _projection(xn, w_in_b, l, jnp.asarray(src_f), jnp.asarray(modes_f),
                         _gain_table(qk_gain[l], gkeys_f), cos_t, sin_t, _F32, seq,
                         tuple(sorted(set(modes_f.tolist()))))
        lam_init = 0.8 - 0.6 * math.exp(-0.3 * l)
        y_a = _neighbourhood(qkv, zf, _na_bias_table(na_rel_bias[l]), batch, seq, off_q, off_z)
        y_b = _gqa(qkv, zf, batch, seq, off_q, off_z)
        y_c = _diff_attention(qkv, zf, diff_lambda[l], diff_subln_g[l], lam_init, batch, seq, off_q, off_z)
        y_d = _dilated(zf, batch, seq, off_z)
        merged = _gated_merge(xn, w_in_b, l, (y_a, y_b, y_c, y_d), wbs)
        xf = _out_projection(merged, w_out_b, l, xf)
    return xf.reshape(batch, seq, d_model)
```

```python
import functools
import math

import jax
import jax.numpy as jnp
import numpy as np
from jax import lax
from jax.experimental import pallas as pl
from jax.experimental.pallas import tpu as pltpu

HD = 128
GRID_W = 64
NA_ROWS, NA_COLS = 8, 16
A_HEADS = 8
B_Q_HEADS, B_KV_HEADS = 8, 2
ROPE_THETA = 10000.0
C_HEADS = 4
D_PATTERNS = ((128, 1), (512, 4), (2048, 16))
D_GROUPS, D_HEADS_PER_GROUP = 3, 4
N_BRANCHES = 4
RMS_EPS = 1e-6
NEG_INF = -1e30
LOG2E = math.log2(math.e)

A_W = A_HEADS * HD
B_QW = B_Q_HEADS * HD
B_KVW = B_KV_HEADS * HD
C_QKW = C_HEADS * 2 * HD
C_VW = C_HEADS * 2 * HD
D_QKVW = D_GROUPS * D_HEADS_PER_GROUP * HD
D_OW = D_HEADS_PER_GROUP * HD
_SEG_NAMES = ("a_q", "a_k", "a_v", "a_z", "b_q", "b_k", "b_v", "b_z",
              "c_q", "c_k", "c_v", "c_z", "d_q", "d_k", "d_v", "d_z")
_SEG_WIDTHS = (A_W, A_W, A_W, A_W, B_QW, B_KVW, B_KVW, B_QW,
               C_QKW, C_QKW, C_VW, C_VW, D_QKVW, D_QKVW, D_QKVW, D_OW)
_SEG_START = dict(zip(_SEG_NAMES, np.cumsum((0,) + _SEG_WIDTHS[:-1]).tolist()))
GATE_START = int(sum(_SEG_WIDTHS))

LANES = 128
V7X_VMEM_BYTES = 64 * 1024 * 1024
VMEM_LIMIT = 56 * 1024 * 1024
TM = 1024
TN = 512
ROW_CHUNK = 256
TM_NORM = 256
TQ = 256
D_TQ, D_SPAN = 256, 512
NA_GROUP, NA_SUPER = 4, 16
NA_UNROLL = 2
D_UNROLL = 4

MODE_PLAIN, MODE_NORM, MODE_NORM_ROPE, MODE_KV_B, MODE_SILU = range(5)

_F32 = jnp.float32
_BF16 = jnp.bfloat16


def _cparams(*sem):
    return pltpu.CompilerParams(dimension_semantics=sem, vmem_limit_bytes=VMEM_LIMIT)


def _sigmoid(v):
    return 0.5 * jnp.tanh(0.5 * v) + 0.5


def _qk_t(q, k):
    return lax.dot_general(q, k, (((1,), (1,)), ((), ())), preferred_element_type=_F32)


def _rmsnorm_kernel(x_ref, g_ref, o_ref):
    x = x_ref[...]
    ms = jnp.mean(x * x, axis=-1, keepdims=True)
    o_ref[...] = (x * lax.rsqrt(ms + RMS_EPS) * g_ref[...]).astype(o_ref.dtype)


def _rmsnorm(x, g):
    m, d = x.shape
    return pl.pallas_call(
        _rmsnorm_kernel,
        grid=(m // TM_NORM,),
        in_specs=[pl.BlockSpec((TM_NORM, d), lambda i: (i, 0)),
                  pl.BlockSpec((1, d), lambda i: (0, 0))],
        out_specs=pl.BlockSpec((TM_NORM, d), lambda i: (i, 0)),
        out_shape=jax.ShapeDtypeStruct((m, d), _BF16),
        compiler_params=_cparams("parallel"),
        name="rmsnorm",
    )(x, g.reshape(1, d))


def _proj_kernel(src_ref, mode_ref, x_ref, w_ref, g_ref, cos_ref, sin_ref, o_ref, *, modes_used):
    del src_ref
    mode = mode_ref[pl.program_id(1)]
    tm, tn = o_ref.shape
    n_heads = tn // HD

    def norm_head(acc, h):
        blk = acc[:, h * HD:(h + 1) * HD]
        ms = jnp.mean(blk * blk, axis=-1, keepdims=True)
        return blk * lax.rsqrt(ms + RMS_EPS) * g_ref[:, h * HD:(h + 1) * HD]

    def rope(y, rs):
        even = (lax.broadcasted_iota(jnp.int32, y.shape, 1) % 2) == 0
        partner = jnp.where(even, pltpu.roll(y, HD - 1, 1), pltpu.roll(y, 1, 1))
        return y * cos_ref[rs, :] + partner * sin_ref[rs, :]

    def head_epilogue(mode_id, acc, h, rs):
        is_key_half = h < n_heads // 2
        if mode_id == MODE_NORM:
            return norm_head(acc, h)
        if mode_id == MODE_NORM_ROPE or (mode_id == MODE_KV_B and is_key_half):
            return rope(norm_head(acc, h), rs)
        return acc[:, h * HD:(h + 1) * HD]

    def run(mode_id):
        for c in range(tm // ROW_CHUNK):
            rs = slice(c * ROW_CHUNK, (c + 1) * ROW_CHUNK)
            acc = jnp.dot(x_ref[rs, :], w_ref[...], preferred_element_type=_F32)
            if mode_id == MODE_PLAIN:
                o_ref[rs, :] = acc.astype(o_ref.dtype)
            elif mode_id == MODE_SILU:
                o_ref[rs, :] = (acc * _sigmoid(acc)).astype(o_ref.dtype)
            else:
                for h in range(n_heads):
                    o_ref[rs, h * HD:(h + 1) * HD] = head_epilogue(mode_id, acc, h, rs).astype(o_ref.dtype)

    for mode_id in modes_used:
        pl.when(mode == mode_id)(functools.partial(run, mode_id))


def _projection(xn, w_b, layer, src_tiles, modes, gains, cos_t, sin_t, out_dtype, seq, modes_used):
    m, d = xn.shape
    n_t = src_tiles.shape[0]
    rope_blocks = seq // TM
    return pl.pallas_call(
        functools.partial(_proj_kernel, modes_used=modes_used),
        grid_spec=pltpu.PrefetchScalarGridSpec(
            num_scalar_prefetch=2,
            grid=(m // TM, n_t),
            in_specs=[
                pl.BlockSpec((TM, d), lambda i, j, src, md: (i, 0)),
                pl.BlockSpec((None, d, TN), lambda i, j, src, md: (layer, 0, src[j])),
                pl.BlockSpec((None, 1, TN), lambda i, j, src, md: (j, 0, 0)),
                pl.BlockSpec((TM, HD), lambda i, j, src, md: (i % rope_blocks, 0)),
                pl.BlockSpec((TM, HD), lambda i, j, src, md: (i % rope_blocks, 0)),
            ],
            out_specs=pl.BlockSpec((TM, TN), lambda i, j, src, md: (i, j)),
        ),
        out_shape=jax.ShapeDtypeStruct((m, n_t * TN), out_dtype),
        compiler_params=_cparams("parallel", "arbitrary"),
        name="projection",
    )(src_tiles, modes, xn, w_b, gains, cos_t, sin_t)


def _tile_plan(segments):
    src, modes, gain_keys, out_start = [], [], [], {}
    col = 0
    for name, mode, gain_key in segments:
        if name == "b_kv":
            start, width = _SEG_START["b_k"], 2 * B_KVW
            assert width == TN and _SEG_START["b_v"] == start + B_KVW
            out_start["b_k"], out_start["b_v"] = col, col + B_KVW
        else:
            start, width = _SEG_START[name], _SEG_WIDTHS[_SEG_NAMES.index(name)]
            out_start[name] = col
        assert start % TN == 0 and width % TN == 0, (name, start, width)
        for t in range(width // TN):
            src.append(start // TN + t)
            modes.append(mode)
            gain_keys.append(gain_key)
        col += width
    return np.asarray(src, np.int32), np.asarray(modes, np.int32), gain_keys, out_start


_Q_GAIN = HD ** -0.5 * LOG2E
_PLAN_BF16 = _tile_plan((
    ("a_q", MODE_NORM, (0, 0)), ("a_k", MODE_NORM, (0, 1)), ("a_v", MODE_PLAIN, None),
    ("b_q", MODE_NORM_ROPE, (1, 0)), ("b_kv", MODE_KV_B, (1, 1)),
    ("c_q", MODE_NORM, (2, 0)), ("c_k", MODE_NORM, (2, 1)), ("c_v", MODE_PLAIN, None)))
_PLAN_F32 = _tile_plan((
    ("a_z", MODE_SILU, None), ("b_z", MODE_SILU, None), ("c_z", MODE_SILU, None),
    ("d_q", MODE_NORM, (3, 0)), ("d_k", MODE_NORM, (3, 1)), ("d_v", MODE_PLAIN, None),
    ("d_z", MODE_SILU, None)))


def _gain_table(qk_gain, gain_keys):
    rows = []
    for key in gain_keys:
        if key is None:
            rows.append(jnp.ones((TN,), _F32))
        else:
            g = qk_gain[key[0], key[1]].astype(_F32)
            if key[1] == 0:
                g = g * _Q_GAIN
            rows.append(jnp.tile(g, TN // HD))
    return jnp.stack(rows)[:, None, :]


def _rope_tables(seq):
    t = jnp.arange(seq)
    row = (t // GRID_W).astype(_F32)
    col = (t % GRID_W).astype(_F32)
    n_pairs = HD // 4
    inv_freq = ROPE_THETA ** (-jnp.arange(n_pairs, dtype=_F32) / n_pairs)
    ang = jnp.concatenate([row[:, None] * inv_freq, col[:, None] * inv_freq], axis=-1)
    cos, sin = jnp.cos(ang), jnp.sin(ang)
    cos_t = jnp.repeat(cos, 2, axis=-1)
    sin_t = jnp.stack([-sin, sin], axis=-1).reshape(seq, HD)
    return cos_t, sin_t


def _na_window_starts(rows):
    starts = {}
    for r in range(rows):
        w0 = int(np.clip(NA_GROUP * (r // NA_GROUP) - NA_ROWS // 2, 0, rows - NA_SUPER))
        r0 = int(np.clip(r - NA_ROWS // 2, 0, rows - NA_ROWS))
        assert 0 <= r0 - w0 and r0 - w0 + NA_ROWS <= NA_SUPER and 0 <= r - w0 < NA_SUPER
        assert starts.setdefault(r - w0, r0 - w0) == r0 - w0
    return starts


def _na_bias_kernel(rb_ref, o_ref, base_ref, *, win_starts):
    h = pl.program_id(0)
    n_row_off, n_col_off = 2 * NA_ROWS - 1, 2 * NA_COLS - 1
    c = lax.broadcasted_iota(jnp.int32, (GRID_W, GRID_W), 0)
    kc = lax.broadcasted_iota(jnp.int32, (GRID_W, GRID_W), 1)
    c0 = jnp.clip(c - NA_COLS // 2, 0, GRID_W - NA_COLS)
    valid = (kc >= c0) & (kc < c0 + NA_COLS)
    col_off = kc - c + (NA_COLS - 1)
    masked = jnp.full((GRID_W, GRID_W), NEG_INF, _F32)
    for row_off in range(n_row_off):
        blk = masked
        for d in range(n_col_off):
            blk = jnp.where(col_off == d, rb_ref[(h * n_row_off + row_off) * n_col_off + d] * LOG2E, blk)
        base_ref[row_off] = jnp.where(valid, blk, NEG_INF)
    for dw in range(NA_SUPER):
        ws = win_starts.get(dw, 0)
        for i in range(NA_SUPER):
            in_window = ws <= i < ws + NA_ROWS
            o_ref[dw, :, i * GRID_W:(i + 1) * GRID_W] = base_ref[i - dw + NA_ROWS - 1] if in_window else masked


def _na_bias_table(rel_bias, rows):
    heads = rel_bias.shape[0]
    return pl.pallas_call(
        functools.partial(_na_bias_kernel, win_starts=_na_window_starts(rows)),
        grid=(heads,),
        in_specs=[pl.BlockSpec(memory_space=pltpu.SMEM)],
        out_specs=pl.BlockSpec((None, NA_SUPER, GRID_W, NA_SUPER * GRID_W), lambda h: (h, 0, 0, 0)),
        out_shape=jax.ShapeDtypeStruct((heads, NA_SUPER, GRID_W, NA_SUPER * GRID_W), _F32),
        scratch_shapes=[pltpu.VMEM((2 * NA_ROWS - 1, GRID_W, GRID_W), _F32)],
        compiler_params=_cparams("parallel"),
        name="na_bias_table",
    )(rel_bias.astype(_F32).reshape(-1))


def _na_kernel(q_ref, k_ref, v_ref, z_ref, bias_ref, o_ref, *, rows):
    gq = NA_GROUP * GRID_W
    sw = NA_SUPER * GRID_W

    def row_group(g, carry):
        w0 = jnp.clip(NA_GROUP * g - NA_ROWS // 2, 0, rows - NA_SUPER)
        q0 = pl.multiple_of(g * gq, gq)
        k0 = pl.multiple_of(w0 * GRID_W, GRID_W)
        bias = bias_ref[pl.ds(NA_GROUP * g - w0, NA_GROUP)].reshape(gq, sw)
        s = _qk_t(q_ref[pl.ds(q0, gq), :], k_ref[pl.ds(k0, sw), :]) + bias
        m = jnp.max(s, axis=-1, keepdims=True)
        p = jnp.exp2(s - m)
        l = jnp.sum(p, axis=-1, keepdims=True)
        o = jnp.dot(p.astype(_BF16), v_ref[pl.ds(k0, sw), :], preferred_element_type=_F32) / l
        o_ref[pl.ds(q0, gq), :] = (o * z_ref[pl.ds(q0, gq), :]).astype(o_ref.dtype)
        return carry

    lax.fori_loop(0, rows // NA_GROUP, row_group, 0, unroll=NA_UNROLL)


def _neighbourhood(qkv, zf, bias_tab, batch, seq, off_q, off_z):
    rows = seq // GRID_W
    assert rows >= NA_SUPER and rows % NA_GROUP == 0
    blk = lambda off: pl.BlockSpec((seq, HD), lambda b, h: (b, off // HD + h))
    return pl.pallas_call(
        functools.partial(_na_kernel, rows=rows),
        grid=(batch, A_HEADS),
        in_specs=[blk(off_q["a_q"]), blk(off_q["a_k"]), blk(off_q["a_v"]), blk(off_z["a_z"]),
                  pl.BlockSpec((None, NA_SUPER, GRID_W, NA_SUPER * GRID_W), lambda b, h: (h, 0, 0, 0))],
        out_specs=pl.BlockSpec((seq, HD), lambda b, h: (b, h)),
        out_shape=jax.ShapeDtypeStruct((batch * seq, A_W), _BF16),
        compiler_params=_cparams("parallel", "parallel"),
        name="neighbourhood_attention",
    )(qkv, qkv, qkv, zf, bias_tab)


def _gqa_kernel(q_ref, k_ref, v_ref, z_ref, o_ref):
    k = k_ref[...]
    v = v_ref[...]
    for g in range(q_ref.shape[1] // HD):
        sl = slice(g * HD, (g + 1) * HD)
        s = _qk_t(q_ref[:, sl], k)
        m = jnp.max(s, axis=-1, keepdims=True)
        p = jnp.exp2(s - m)
        l = jnp.sum(p, axis=-1, keepdims=True)
        o = jnp.dot(p.astype(_BF16), v, preferred_element_type=_F32) / l
        o_ref[:, sl] = (o * z_ref[:, sl]).astype(o_ref.dtype)


def _gqa(qkv, zf, batch, seq, off_q, off_z):
    group_w = (B_Q_HEADS // B_KV_HEADS) * HD
    nq = seq // TQ
    q_spec = lambda off: pl.BlockSpec((TQ, group_w), lambda b, hk, i: (b * nq + i, off // group_w + hk))
    kv_spec = lambda off: pl.BlockSpec((seq, HD), lambda b, hk, i: (b, off // HD + hk))
    return pl.pallas_call(
        _gqa_kernel,
        grid=(batch, B_KV_HEADS, nq),
        in_specs=[q_spec(off_q["b_q"]), kv_spec(off_q["b_k"]), kv_spec(off_q["b_v"]), q_spec(off_z["b_z"])],
        out_specs=pl.BlockSpec((TQ, group_w), lambda b, hk, i: (b * nq + i, hk)),
        out_shape=jax.ShapeDtypeStruct((batch * seq, B_QW), _BF16),
        compiler_params=_cparams("parallel", "parallel", "parallel"),
        name="gqa_attention",
    )(qkv, qkv, qkv, zf)


def _diff_kernel(slope_ref, q_ref, k_ref, v_ref, z_ref, lam_ref, g_ref, o_ref, dist_ref, *, lam_init):
    i, h = pl.program_id(1), pl.program_id(2)
    tq, seq = dist_ref.shape

    @pl.when(h == 0)
    def _():
        qpos = i * tq + lax.broadcasted_iota(jnp.int32, (tq, seq), 0)
        kpos = lax.broadcasted_iota(jnp.int32, (tq, seq), 1)
        dist_ref[...] = jnp.abs(qpos - kpos).astype(_F32)

    lp = lam_ref[...]
    lam = (jnp.exp(jnp.sum(lp[0:1] * lp[1:2], axis=-1, keepdims=True))
           - jnp.exp(jnp.sum(lp[2:3] * lp[3:4], axis=-1, keepdims=True)) + lam_init)
    pen = (slope_ref[h] * LOG2E) * dist_ref[...]

    def softmax_map(mi):
        sl = slice(mi * HD, (mi + 1) * HD)
        s = _qk_t(q_ref[:, sl], k_ref[:, sl]) - pen
        m = jnp.max(s, axis=-1, keepdims=True)
        p = jnp.exp2(s - m)
        return p, jnp.sum(p, axis=-1, keepdims=True)

    p0, l0 = softmax_map(0)
    p1, l1 = softmax_map(1)
    a = p0 * (1.0 / l0) - p1 * (lam / l1)
    o = jnp.dot(a.astype(_BF16), v_ref[...], preferred_element_type=_F32)
    ms = jnp.mean(o * o, axis=-1, keepdims=True)
    o = o * lax.rsqrt(ms + RMS_EPS) * g_ref[...] * (1.0 - lam_init)
    o_ref[...] = (o * z_ref[...]).astype(o_ref.dtype)


def _diff_attention(qkv, zf, diff_lambda, subln_g, lam_init, batch, seq, off_q, off_z):
    hw = 2 * HD
    nq = seq // TQ
    slopes = jnp.asarray(2.0 ** (-8.0 * np.arange(1, C_HEADS + 1) / C_HEADS), dtype=_F32)
    q_spec = lambda off: pl.BlockSpec((TQ, hw), lambda b, i, h, sl: (b * nq + i, off // hw + h))
    kv_spec = lambda off: pl.BlockSpec((seq, hw), lambda b, i, h, sl: (b, off // hw + h))
    return pl.pallas_call(
        functools.partial(_diff_kernel, lam_init=lam_init),
        grid_spec=pltpu.PrefetchScalarGridSpec(
            num_scalar_prefetch=1,
            grid=(batch, nq, C_HEADS),
            in_specs=[q_spec(off_q["c_q"]), kv_spec(off_q["c_k"]), kv_spec(off_q["c_v"]), q_spec(off_z["c_z"]),
                      pl.BlockSpec((4, HD), lambda b, i, h, sl: (0, 0)),
                      pl.BlockSpec((1, hw), lambda b, i, h, sl: (0, 0))],
            out_specs=pl.BlockSpec((TQ, hw), lambda b, i, h, sl: (b * nq + i, h)),
            scratch_shapes=[pltpu.VMEM((TQ, seq), _F32)],
        ),
        out_shape=jax.ShapeDtypeStruct((batch * seq, C_VW), _BF16),
        compiler_params=_cparams("parallel", "parallel", "arbitrary"),
        name="diff_attention",
    )(slopes, qkv, qkv, qkv, zf, diff_lambda.astype(_F32), subln_g.astype(_F32).reshape(1, hw))


def _dilated_group(gi, window, dil, slope, q_ref, k_ref, v_ref, og_ref, lse_ref, bias_ref):
    seq = q_ref.shape[0]
    length = seq // dil
    tq = min(D_TQ, length)
    span = min(D_SPAN, length)
    n_tiles = length // tq
    n_side = window // (2 * dil)
    half = (span - tq) // 2
    unit_pen = slope * (float(dil) * LOG2E)
    rel0 = (lax.broadcasted_iota(jnp.int32, (tq, span), 1)
            - lax.broadcasted_iota(jnp.int32, (tq, span), 0))
    for vi in range(3 if n_tiles > 1 else 1):
        rel = jnp.abs(rel0 - vi * half)
        bias_ref[vi, :tq, :span] = jnp.where(rel <= n_side, -unit_pen * rel.astype(_F32), NEG_INF)

    def rows(start, size):
        return pl.ds(start, size) if dil == 1 else pl.ds(start, size, stride=dil)

    def tile(idx, carry):
        r = lax.div(idx, n_tiles)
        u0 = lax.rem(idx, n_tiles) * tq
        ks = jnp.clip(u0 - half, 0, length - span)
        vi = lax.div(u0 - ks, half) if n_tiles > 1 else 0
        q_rows = rows(r + dil * u0, tq)
        k_rows = rows(r + dil * ks, span)
        s = _qk_t(q_ref[q_rows, :].astype(_BF16), k_ref[k_rows, :].astype(_BF16))
        s = s + bias_ref[vi, :tq, :span]
        m = jnp.max(s, axis=-1, keepdims=True)
        p = jnp.exp2(s - m)
        l = jnp.sum(p, axis=-1, keepdims=True)
        o = jnp.dot(p.astype(_BF16), v_ref[k_rows, :].astype(_BF16), preferred_element_type=_F32) / l
        og_ref[gi, q_rows, :] = o
        lse_ref[gi, q_rows, :] = jnp.broadcast_to(m + jnp.log2(l), (tq, LANES))
        return carry

    lax.fori_loop(0, dil * n_tiles, tile, 0, unroll=D_UNROLL)


def _dilated_kernel(slope_ref, q_ref, k_ref, v_ref, z_ref, o_ref, og_ref, lse_ref, bias_ref):
    h, g = pl.program_id(1), pl.program_id(2)
    for gi, (window, dil) in enumerate(D_PATTERNS):
        @pl.when(g == gi)
        def _(gi=gi, window=window, dil=dil):
            _dilated_group(gi, window, dil, slope_ref[gi * D_HEADS_PER_GROUP + h],
                           q_ref, k_ref, v_ref, og_ref, lse_ref, bias_ref)

    @pl.when(g == D_GROUPS - 1)
    def _():
        seq = o_ref.shape[0]
        chunk = min(512, seq)

        def merge(ci, carry):
            rs = pl.ds(pl.multiple_of(ci * chunk, chunk), chunk)
            lses = [lse_ref[gi, rs, :] for gi in range(D_GROUPS)]
            top = functools.reduce(jnp.maximum, lses)
            ws = [jnp.exp2(v - top) for v in lses]
            num = functools.reduce(jnp.add, [w * og_ref[gi, rs, :] for gi, w in enumerate(ws)])
            out = num / functools.reduce(jnp.add, ws)
            o_ref[rs, :] = (out * z_ref[rs, :]).astype(o_ref.dtype)
            return carry

        lax.fori_loop(0, seq // chunk, merge, 0)


def _dilated(zf, batch, seq, off_z):
    n = D_GROUPS * D_HEADS_PER_GROUP
    slopes = jnp.asarray(2.0 ** (-8.0 * np.arange(1, n + 1) / n), dtype=_F32)
    for window, dil in D_PATTERNS:
        assert seq % dil == 0 and (seq // dil) % min(D_TQ, seq // dil) == 0
        assert D_SPAN - D_TQ >= 2 * (window // (2 * dil)) or seq // dil <= D_SPAN
    qkv_spec = lambda off: pl.BlockSpec(
        (seq, HD), lambda b, h, g, sl: (b, off // HD + g * D_HEADS_PER_GROUP + h))
    return pl.pallas_call(
        _dilated_kernel,
        grid_spec=pltpu.PrefetchScalarGridSpec(
            num_scalar_prefetch=1,
            grid=(batch, D_HEADS_PER_GROUP, D_GROUPS),
            in_specs=[qkv_spec(off_z["d_q"]), qkv_spec(off_z["d_k"]), qkv_spec(off_z["d_v"]),
                      pl.BlockSpec((seq, HD), lambda b, h, g, sl: (b, off_z["d_z"] // HD + h))],
            out_specs=pl.BlockSpec((seq, HD), lambda b, h, g, sl: (b, h)),
            scratch_shapes=[pltpu.VMEM((D_GROUPS, seq, HD), _F32),
                            pltpu.VMEM((D_GROUPS, seq, LANES), _F32),
                            pltpu.VMEM((3, D_TQ, D_SPAN), _F32)],
        ),
        out_shape=jax.ShapeDtypeStruct((batch * seq, D_OW), _BF16),
        compiler_params=_cparams("parallel", "parallel", "arbitrary"),
        name="dilated_attention",
    )(slopes, zf, zf, zf, zf)


def _merge_kernel(x_ref, wg_ref, *refs, n_br):
    y_refs, wb_refs = refs[:n_br], refs[n_br:2 * n_br]
    o_ref, acc_ref = refs[2 * n_br:]
    br = pl.program_id(2)
    tm = x_ref.shape[0]

    @pl.when(br == 0)
    def _():
        acc_ref[...] = jnp.zeros_like(acc_ref)

    def add_branch(y_ref, wb_ref):
        for c in range(tm // ROW_CHUNK):
            rs = slice(c * ROW_CHUNK, (c + 1) * ROW_CHUNK)
            gate = _sigmoid(jnp.dot(x_ref[rs, :], wg_ref[...], preferred_element_type=_F32))
            acc_ref[rs, :] += gate * jnp.dot(y_ref[rs, :], wb_ref[...], preferred_element_type=_F32)

    for b in range(n_br):
        pl.when(br == b)(functools.partial(add_branch, y_refs[b], wb_refs[b]))

    @pl.when(br == n_br - 1)
    def _():
        o_ref[...] = acc_ref[...].astype(o_ref.dtype)


def _gated_merge(xn, w_b, layer, ys, wbs):
    m, d = xn.shape
    n_br = len(ys)
    assert GATE_START % TN == 0 and d % TN == 0
    gate_tile0 = GATE_START // TN
    tiles_per_branch = d // TN
    y_specs = [pl.BlockSpec((TM, y.shape[1]), lambda i, j, br: (i, 0)) for y in ys]
    wb_specs = [pl.BlockSpec((None, w.shape[1], TN), lambda i, j, br: (layer, 0, j)) for w in wbs]
    return pl.pallas_call(
        functools.partial(_merge_kernel, n_br=n_br),
        grid=(m // TM, d // TN, n_br),
        in_specs=[
            pl.BlockSpec((TM, d), lambda i, j, br: (i, 0)),
            pl.BlockSpec((None, d, TN), lambda i, j, br: (layer, 0, gate_tile0 + br * tiles_per_branch + j)),
            *y_specs, *wb_specs,
        ],
        out_specs=pl.BlockSpec((TM, TN), lambda i, j, br: (i, j)),
        out_shape=jax.ShapeDtypeStruct((m, d), _BF16),
        scratch_shapes=[pltpu.VMEM((TM, TN), _F32)],
        compiler_params=_cparams("parallel", "parallel", "arbitrary"),
        name="gated_merge",
    )(xn, w_b, *ys, *wbs)


def _out_kernel(m_ref, w_ref, x_ref, o_ref):
    for c in range(o_ref.shape[0] // ROW_CHUNK):
        rs = slice(c * ROW_CHUNK, (c + 1) * ROW_CHUNK)
        o_ref[rs, :] = x_ref[rs, :] + jnp.dot(m_ref[rs, :], w_ref[...], preferred_element_type=_F32)


def _out_projection(merged, w_out_b, layer, x):
    m, d = x.shape
    return pl.pallas_call(
        _out_kernel,
        grid=(m // TM, d // TN),
        in_specs=[pl.BlockSpec((TM, d), lambda i, j: (i, 0)),
                  pl.BlockSpec((None, d, TN), lambda i, j: (layer, 0, j)),
                  pl.BlockSpec((TM, TN), lambda i, j: (i, j))],
        out_specs=pl.BlockSpec((TM, TN), lambda i, j: (i, j)),
        out_shape=jax.ShapeDtypeStruct((m, d), _F32),
        compiler_params=_cparams("parallel", "arbitrary"),
        name="out_projection",
    )(merged, w_out_b, x)


def kernel(x, norm_g, w_in, qk_gain, na_rel_bias, diff_lambda, diff_subln_g,
           w_branch_a, w_branch_b, w_branch_c, w_branch_d, w_out):
    batch, seq, d_model = x.shape
    depth = w_in.shape[0]
    m = batch * seq
    assert w_in.shape[2] == GATE_START + N_BRANCHES * d_model
    assert m % TM == 0 and seq % TM == 0 and seq % TQ == 0 and seq % GRID_W == 0 and d_model % TN == 0

    w_in_b = w_in.astype(_BF16)
    w_out_b = w_out.astype(_BF16)
    wbs = tuple(w.astype(_BF16) for w in (w_branch_a, w_branch_b, w_branch_c, w_branch_d))
    cos_t, sin_t = _rope_tables(seq)
    src_b, modes_b, gkeys_b, off_q = _PLAN_BF16
    src_f, modes_f, gkeys_f, off_z = _PLAN_F32

    xf = x.reshape(m, d_model)
    for l in range(depth):
        xn = _rmsnorm(xf, norm_g[l])
        qkv = _projection(xn, w_in_b, l, jnp.asarray(src_b), jnp.asarray(modes_b),
                          _gain_table(qk_gain[l], gkeys_b), cos_t, sin_t, _BF16, seq,
                          tuple(sorted(set(modes_b.tolist()))))
        zf = _projection(xn, w_in_b, l, jnp.asarray(src_f), jnp.asarray(modes_f),
                         _gain_table(qk_gain[l], gkeys_f), cos_t, sin_t, _F32, seq,
                         tuple(sorted(set(modes_f.tolist()))))
        lam_init = 0.8 - 0.6 * math.exp(-0.3 * l)
        y_a = _neighbourhood(qkv, zf, _na_bias_table(na_rel_bias[l], seq // GRID_W), batch, seq, off_q, off_z)
        y_b = _gqa(qkv, zf, batch, seq, off_q, off_z)
        y_c = _diff_attention(qkv, zf, diff_lambda[l], diff_subln_g[l], lam_init, batch, seq, off_q, off_z)
        y_d = _dilated(zf, batch, seq, off_z)
        merged = _gated_merge(xn, w_in_b, l, (y_a, y_b, y_c, y_d), wbs)
        xf = _out_projection(merged, w_out_b, l, xf)
    return xf.reshape(batch, seq, d_model)
```

```python
import functools
import math

import jax
import jax.numpy as jnp
import numpy as np
from jax import lax
from jax.experimental import pallas as pl
from jax.experimental.pallas import tpu as pltpu

HD = 128
GRID_W = 64
NA_ROWS, NA_COLS = 8, 16
A_HEADS = 8
B_Q_HEADS, B_KV_HEADS = 8, 2
ROPE_THETA = 10000.0
C_HEADS = 4
D_PATTERNS = ((128, 1), (512, 4), (2048, 16))
D_GROUPS, D_HEADS_PER_GROUP = 3, 4
N_BRANCHES = 4
RMS_EPS = 1e-6
NEG_INF = -1e30
LOG2E = math.log2(math.e)

A_W = A_HEADS * HD
B_QW = B_Q_HEADS * HD
B_KVW = B_KV_HEADS * HD
C_QKW = C_HEADS * 2 * HD
C_VW = C_HEADS * 2 * HD
D_QKVW = D_GROUPS * D_HEADS_PER_GROUP * HD
D_OW = D_HEADS_PER_GROUP * HD
_SEG_NAMES = ("a_q", "a_k", "a_v", "a_z", "b_q", "b_k", "b_v", "b_z",
              "c_q", "c_k", "c_v", "c_z", "d_q", "d_k", "d_v", "d_z")
_SEG_WIDTHS = (A_W, A_W, A_W, A_W, B_QW, B_KVW, B_KVW, B_QW,
               C_QKW, C_QKW, C_VW, C_VW, D_QKVW, D_QKVW, D_QKVW, D_OW)
_SEG_START = dict(zip(_SEG_NAMES, np.cumsum((0,) + _SEG_WIDTHS[:-1]).tolist()))
GATE_START = int(sum(_SEG_WIDTHS))

LANES = 128
V7X_VMEM_BYTES = 64 * 1024 * 1024
VMEM_LIMIT = 56 * 1024 * 1024
TM = 1024
TN = 512
ROW_CHUNK = 256
TM_NORM = 256
TQ = 256
KV_CHUNK = 512
ONES_ROWS = 16
D_TQ, D_SPAN = 256, 512
NA_GROUP, NA_SUPER = 4, 16
NA_UNROLL = 2
D_UNROLL = 4

MODE_PLAIN, MODE_NORM, MODE_NORM_ROPE, MODE_KV_B, MODE_SILU = range(5)

_F32 = jnp.float32
_BF16 = jnp.bfloat16


def _cparams(*sem):
    return pltpu.CompilerParams(dimension_semantics=sem, vmem_limit_bytes=VMEM_LIMIT)


def _sigmoid(v):
    return 0.5 * jnp.tanh(0.5 * v) + 0.5


def _qk_t(q, k):
    return lax.dot_general(q, k, (((1,), (1,)), ((), ())), preferred_element_type=_F32)


def _rmsnorm_kernel(x_ref, g_ref, o_ref):
    x = x_ref[...]
    ms = jnp.mean(x * x, axis=-1, keepdims=True)
    o_ref[...] = (x * lax.rsqrt(ms + RMS_EPS) * g_ref[...]).astype(o_ref.dtype)


def _rmsnorm(x, g):
    m, d = x.shape
    return pl.pallas_call(
        _rmsnorm_kernel,
        grid=(m // TM_NORM,),
        in_specs=[pl.BlockSpec((TM_NORM, d), lambda i: (i, 0)),
                  pl.BlockSpec((1, d), lambda i: (0, 0))],
        out_specs=pl.BlockSpec((TM_NORM, d), lambda i: (i, 0)),
        out_shape=jax.ShapeDtypeStruct((m, d), _BF16),
        compiler_params=_cparams("parallel"),
        name="rmsnorm",
    )(x, g.reshape(1, d))


def _proj_kernel(src_ref, mode_ref, x_ref, w_ref, g_ref, cos_ref, sin_ref, o_ref, *, modes_used):
    del src_ref
    mode = mode_ref[pl.program_id(1)]
    tm, tn = o_ref.shape
    n_heads = tn // HD

    def norm_head(acc, h):
        blk = acc[:, h * HD:(h + 1) * HD]
        ms = jnp.mean(blk * blk, axis=-1, keepdims=True)
        return blk * lax.rsqrt(ms + RMS_EPS) * g_ref[:, h * HD:(h + 1) * HD]

    def rope(y, rs):
        even = (lax.broadcasted_iota(jnp.int32, y.shape, 1) % 2) == 0
        partner = jnp.where(even, pltpu.roll(y, HD - 1, 1), pltpu.roll(y, 1, 1))
        return y * cos_ref[rs, :] + partner * sin_ref[rs, :]

    def head_epilogue(mode_id, acc, h, rs):
        is_key_half = h < n_heads // 2
        if mode_id == MODE_NORM:
            return norm_head(acc, h)
        if mode_id == MODE_NORM_ROPE or (mode_id == MODE_KV_B and is_key_half):
            return rope(norm_head(acc, h), rs)
        return acc[:, h * HD:(h + 1) * HD]

    def run(mode_id):
        for c in range(tm // ROW_CHUNK):
            rs = slice(c * ROW_CHUNK, (c + 1) * ROW_CHUNK)
            acc = jnp.dot(x_ref[rs, :], w_ref[...], preferred_element_type=_F32)
            if mode_id == MODE_PLAIN:
                o_ref[rs, :] = acc.astype(o_ref.dtype)
            elif mode_id == MODE_SILU:
                o_ref[rs, :] = (acc * _sigmoid(acc)).astype(o_ref.dtype)
            else:
                for h in range(n_heads):
                    o_ref[rs, h * HD:(h + 1) * HD] = head_epilogue(mode_id, acc, h, rs).astype(o_ref.dtype)

    for mode_id in modes_used:
        pl.when(mode == mode_id)(functools.partial(run, mode_id))


def _projection(xn, w_b, layer, src_tiles, modes, gains, cos_t, sin_t, out_dtype, seq, modes_used):
    m, d = xn.shape
    n_t = src_tiles.shape[0]
    rope_blocks = seq // TM
    return pl.pallas_call(
        functools.partial(_proj_kernel, modes_used=modes_used),
        grid_spec=pltpu.PrefetchScalarGridSpec(
            num_scalar_prefetch=2,
            grid=(m // TM, n_t),
            in_specs=[
                pl.BlockSpec((TM, d), lambda i, j, src, md: (i, 0)),
                pl.BlockSpec((None, d, TN), lambda i, j, src, md: (layer, 0, src[j])),
                pl.BlockSpec((None, 1, TN), lambda i, j, src, md: (j, 0, 0)),
                pl.BlockSpec((TM, HD), lambda i, j, src, md: (i % rope_blocks, 0)),
                pl.BlockSpec((TM, HD), lambda i, j, src, md: (i % rope_blocks, 0)),
            ],
            out_specs=pl.BlockSpec((TM, TN), lambda i, j, src, md: (i, j)),
        ),
        out_shape=jax.ShapeDtypeStruct((m, n_t * TN), out_dtype),
        compiler_params=_cparams("parallel", "arbitrary"),
        name="projection",
    )(src_tiles, modes, xn, w_b, gains, cos_t, sin_t)


def _tile_plan(segments):
    src, modes, gain_keys, out_start = [], [], [], {}
    col = 0
    for name, mode, gain_key in segments:
        if name == "b_kv":
            start, width = _SEG_START["b_k"], 2 * B_KVW
            assert width == TN and _SEG_START["b_v"] == start + B_KVW
            out_start["b_k"], out_start["b_v"] = col, col + B_KVW
        else:
            start, width = _SEG_START[name], _SEG_WIDTHS[_SEG_NAMES.index(name)]
            out_start[name] = col
        assert start % TN == 0 and width % TN == 0, (name, start, width)
        for t in range(width // TN):
            src.append(start // TN + t)
            modes.append(mode)
            gain_keys.append(gain_key)
        col += width
    return np.asarray(src, np.int32), np.asarray(modes, np.int32), gain_keys, out_start


_Q_GAIN = HD ** -0.5 * LOG2E
_PLAN_BF16 = _tile_plan((
    ("a_q", MODE_NORM, (0, 0)), ("a_k", MODE_NORM, (0, 1)), ("a_v", MODE_PLAIN, None),
    ("b_q", MODE_NORM_ROPE, (1, 0)), ("b_kv", MODE_KV_B, (1, 1)),
    ("c_q", MODE_NORM, (2, 0)), ("c_k", MODE_NORM, (2, 1)), ("c_v", MODE_PLAIN, None)))
_PLAN_F32 = _tile_plan((
    ("a_z", MODE_SILU, None), ("b_z", MODE_SILU, None), ("c_z", MODE_SILU, None),
    ("d_q", MODE_NORM, (3, 0)), ("d_k", MODE_NORM, (3, 1)), ("d_v", MODE_PLAIN, None),
    ("d_z", MODE_SILU, None)))


def _gain_table(qk_gain, gain_keys):
    rows = []
    for key in gain_keys:
        if key is None:
            rows.append(jnp.ones((TN,), _F32))
        else:
            g = qk_gain[key[0], key[1]].astype(_F32)
            if key[1] == 0:
                g = g * _Q_GAIN
            rows.append(jnp.tile(g, TN // HD))
    return jnp.stack(rows)[:, None, :]


def _rope_tables(seq):
    t = jnp.arange(seq)
    row = (t // GRID_W).astype(_F32)
    col = (t % GRID_W).astype(_F32)
    n_pairs = HD // 4
    inv_freq = ROPE_THETA ** (-jnp.arange(n_pairs, dtype=_F32) / n_pairs)
    ang = jnp.concatenate([row[:, None] * inv_freq, col[:, None] * inv_freq], axis=-1)
    cos, sin = jnp.cos(ang), jnp.sin(ang)
    cos_t = jnp.repeat(cos, 2, axis=-1)
    sin_t = jnp.stack([-sin, sin], axis=-1).reshape(seq, HD)
    return cos_t, sin_t


def _na_window_starts(rows):
    starts = {}
    for r in range(rows):
        w0 = int(np.clip(NA_GROUP * (r // NA_GROUP) - NA_ROWS // 2, 0, rows - NA_SUPER))
        r0 = int(np.clip(r - NA_ROWS // 2, 0, rows - NA_ROWS))
        assert 0 <= r0 - w0 and r0 - w0 + NA_ROWS <= NA_SUPER and 0 <= r - w0 < NA_SUPER
        assert starts.setdefault(r - w0, r0 - w0) == r0 - w0
    return starts


def _na_bias_kernel(rb_ref, o_ref, base_ref, *, win_starts):
    h = pl.program_id(0)
    n_row_off, n_col_off = 2 * NA_ROWS - 1, 2 * NA_COLS - 1
    c = lax.broadcasted_iota(jnp.int32, (GRID_W, GRID_W), 0)
    kc = lax.broadcasted_iota(jnp.int32, (GRID_W, GRID_W), 1)
    c0 = jnp.clip(c - NA_COLS // 2, 0, GRID_W - NA_COLS)
    valid = (kc >= c0) & (kc < c0 + NA_COLS)
    col_off = kc - c + (NA_COLS - 1)
    masked = jnp.full((GRID_W, GRID_W), NEG_INF, _F32)
    for row_off in range(n_row_off):
        blk = masked
        for d in range(n_col_off):
            blk = jnp.where(col_off == d, rb_ref[(h * n_row_off + row_off) * n_col_off + d] * LOG2E, blk)
        base_ref[row_off] = jnp.where(valid, blk, NEG_INF)
    for dw in range(NA_SUPER):
        ws = win_starts.get(dw, 0)
        for i in range(NA_SUPER):
            in_window = ws <= i < ws + NA_ROWS
            o_ref[dw, :, i * GRID_W:(i + 1) * GRID_W] = base_ref[i - dw + NA_ROWS - 1] if in_window else masked


def _na_bias_table(rel_bias, rows):
    heads = rel_bias.shape[0]
    return pl.pallas_call(
        functools.partial(_na_bias_kernel, win_starts=_na_window_starts(rows)),
        grid=(heads,),
        in_specs=[pl.BlockSpec(memory_space=pltpu.SMEM)],
        out_specs=pl.BlockSpec((None, NA_SUPER, GRID_W, NA_SUPER * GRID_W), lambda h: (h, 0, 0, 0)),
        out_shape=jax.ShapeDtypeStruct((heads, NA_SUPER, GRID_W, NA_SUPER * GRID_W), _F32),
        scratch_shapes=[pltpu.VMEM((2 * NA_ROWS - 1, GRID_W, GRID_W), _F32)],
        compiler_params=_cparams("parallel"),
        name="na_bias_table",
    )(rel_bias.astype(_F32).reshape(-1))


def _na_kernel(q_ref, k_ref, v_ref, z_ref, bias_ref, o_ref, *, rows):
    gq = NA_GROUP * GRID_W
    sw = NA_SUPER * GRID_W

    def row_group(g, carry):
        w0 = jnp.clip(NA_GROUP * g - NA_ROWS // 2, 0, rows - NA_SUPER)
        q0 = pl.multiple_of(g * gq, gq)
        k0 = pl.multiple_of(w0 * GRID_W, GRID_W)
        bias = bias_ref[pl.ds(NA_GROUP * g - w0, NA_GROUP)].reshape(gq, sw)
        s = _qk_t(q_ref[pl.ds(q0, gq), :], k_ref[pl.ds(k0, sw), :]) + bias
        m = jnp.max(s, axis=-1, keepdims=True)
        p = jnp.exp2(s - m)
        l = jnp.sum(p, axis=-1, keepdims=True)
        o = jnp.dot(p.astype(_BF16), v_ref[pl.ds(k0, sw), :], preferred_element_type=_F32) / l
        o_ref[pl.ds(q0, gq), :] = (o * z_ref[pl.ds(q0, gq), :]).astype(o_ref.dtype)
        return carry

    lax.fori_loop(0, rows // NA_GROUP, row_group, 0, unroll=NA_UNROLL)


def _neighbourhood(qkv, zf, bias_tab, batch, seq, off_q, off_z):
    rows = seq // GRID_W
    assert rows >= NA_SUPER and rows % NA_GROUP == 0
    blk = lambda off: pl.BlockSpec((seq, HD), lambda b, h: (b, off // HD + h))
    return pl.pallas_call(
        functools.partial(_na_kernel, rows=rows),
        grid=(batch, A_HEADS),
        in_specs=[blk(off_q["a_q"]), blk(off_q["a_k"]), blk(off_q["a_v"]), blk(off_z["a_z"]),
                  pl.BlockSpec((None, NA_SUPER, GRID_W, NA_SUPER * GRID_W), lambda b, h: (h, 0, 0, 0))],
        out_specs=pl.BlockSpec((seq, HD), lambda b, h: (b, h)),
        out_shape=jax.ShapeDtypeStruct((batch * seq, A_W), _BF16),
        compiler_params=_cparams("parallel", "parallel"),
        name="neighbourhood_attention",
    )(qkv, qkv, qkv, zf, bias_tab)


def _store_transposed(vt_ref, v_ref):
    seq, d = v_ref.shape
    for c in range(seq // KV_CHUNK):
        cs = slice(c * KV_CHUNK, (c + 1) * KV_CHUNK)
        vt_ref[:d, cs] = v_ref[cs, :].astype(_F32).T.astype(vt_ref.dtype)
    vt_ref[d:, :] = jnp.ones((vt_ref.shape[0] - d, seq), vt_ref.dtype)


class _OnlineSoftmax:
    def __init__(self, d_v, n_q):
        self.d_v = d_v
        self.m = jnp.full((1, n_q), NEG_INF, _F32)
        self.acc = jnp.zeros((d_v + ONES_ROWS, n_q), _F32)

    def update(self, s, vt_chunk):
        m_new = jnp.maximum(self.m, jnp.max(s, axis=0, keepdims=True))
        p = jnp.exp2(s - m_new).astype(_BF16)
        self.acc = (jnp.exp2(self.m - m_new) * self.acc
                    + jnp.dot(vt_chunk, p, preferred_element_type=_F32))
        self.m = m_new

    def result(self):
        return self.acc[:self.d_v] / self.acc[self.d_v:self.d_v + 1]


def _gqa_kernel(q_ref, k_ref, v_ref, z_ref, o_ref, vt_ref):
    seq = k_ref.shape[0]
    tq = q_ref.shape[0]

    @pl.when(pl.program_id(2) == 0)
    def _():
        _store_transposed(vt_ref, v_ref)

    n_g = q_ref.shape[1] // HD
    q_all = jnp.concatenate([q_ref[:, g * HD:(g + 1) * HD] for g in range(n_g)], axis=0)
    state = _OnlineSoftmax(HD, n_g * tq)
    for c in range(seq // KV_CHUNK):
        cs = slice(c * KV_CHUNK, (c + 1) * KV_CHUNK)
        state.update(_qk_t(k_ref[cs, :], q_all), vt_ref[:, cs])
    o_t = state.result()
    for g in range(n_g):
        sl = slice(g * HD, (g + 1) * HD)
        o_ref[:, sl] = (o_t[:, g * tq:(g + 1) * tq].T * z_ref[:, sl]).astype(o_ref.dtype)


def _gqa(qkv, zf, batch, seq, off_q, off_z):
    group_w = (B_Q_HEADS // B_KV_HEADS) * HD
    nq = seq // TQ
    q_spec = lambda off: pl.BlockSpec((TQ, group_w), lambda b, hk, i: (b * nq + i, off // group_w + hk))
    kv_spec = lambda off: pl.BlockSpec((seq, HD), lambda b, hk, i: (b, off // HD + hk))
    return pl.pallas_call(
        _gqa_kernel,
        grid=(batch, B_KV_HEADS, nq),
        in_specs=[q_spec(off_q["b_q"]), kv_spec(off_q["b_k"]), kv_spec(off_q["b_v"]), q_spec(off_z["b_z"])],
        out_specs=pl.BlockSpec((TQ, group_w), lambda b, hk, i: (b * nq + i, hk)),
        out_shape=jax.ShapeDtypeStruct((batch * seq, B_QW), _BF16),
        scratch_shapes=[pltpu.VMEM((HD + ONES_ROWS, seq), _BF16)],
        compiler_params=_cparams("parallel", "parallel", "arbitrary"),
        name="gqa_attention",
    )(qkv, qkv, qkv, zf)


def _diff_kernel(slope_ref, q_ref, k_ref, v_ref, z_ref, lam_ref, g_ref, o_ref, vt_ref, pen_ref, *, lam_init):
    h, i = pl.program_id(1), pl.program_id(2)
    seq = k_ref.shape[0]
    tq, hw = q_ref.shape
    key0 = seq - tq

    @pl.when(i == 0)
    def _():
        _store_transposed(vt_ref, v_ref)
        slope = slope_ref[h] * LOG2E
        n_rows = pen_ref.shape[0]
        for y0 in range(0, n_rows, KV_CHUNK):
            size = min(KV_CHUNK, n_rows - y0)
            row_minus_col = (lax.broadcasted_iota(jnp.int32, (size, tq), 0)
                             - lax.broadcasted_iota(jnp.int32, (size, tq), 1))
            pen_ref[y0:y0 + size, :] = slope * jnp.abs(row_minus_col + (y0 - key0)).astype(_F32)

    lp = lam_ref[...]
    lam = (jnp.exp(jnp.sum(lp[0:1] * lp[1:2], axis=-1, keepdims=True))
           - jnp.exp(jnp.sum(lp[2:3] * lp[3:4], axis=-1, keepdims=True)) + lam_init)

    states = [_OnlineSoftmax(hw, tq) for _ in range(2)]
    pen0 = pl.multiple_of(key0 - i * tq, tq)
    for c in range(seq // KV_CHUNK):
        cs = slice(c * KV_CHUNK, (c + 1) * KV_CHUNK)
        pen = pen_ref[pl.ds(pen0 + c * KV_CHUNK, KV_CHUNK), :]
        for mi, state in enumerate(states):
            sl = slice(mi * HD, (mi + 1) * HD)
            state.update(_qk_t(k_ref[cs, sl], q_ref[:, sl]) - pen, vt_ref[:, cs])
    o = (states[0].result() - lam * states[1].result()).T
    ms = jnp.mean(o * o, axis=-1, keepdims=True)
    o = o * lax.rsqrt(ms + RMS_EPS) * g_ref[...] * (1.0 - lam_init)
    o_ref[...] = (o * z_ref[...]).astype(o_ref.dtype)


def _diff_attention(qkv, zf, diff_lambda, subln_g, lam_init, batch, seq, off_q, off_z):
    hw = 2 * HD
    nq = seq // TQ
    slopes = jnp.asarray(2.0 ** (-8.0 * np.arange(1, C_HEADS + 1) / C_HEADS), dtype=_F32)
    q_spec = lambda off: pl.BlockSpec((TQ, hw), lambda b, h, i, sl: (b * nq + i, off // hw + h))
    kv_spec = lambda off: pl.BlockSpec((seq, hw), lambda b, h, i, sl: (b, off // hw + h))
    return pl.pallas_call(
        functools.partial(_diff_kernel, lam_init=lam_init),
        grid_spec=pltpu.PrefetchScalarGridSpec(
            num_scalar_prefetch=1,
            grid=(batch, C_HEADS, nq),
            in_specs=[q_spec(off_q["c_q"]), kv_spec(off_q["c_k"]), kv_spec(off_q["c_v"]), q_spec(off_z["c_z"]),
                      pl.BlockSpec((4, HD), lambda b, h, i, sl: (0, 0)),
                      pl.BlockSpec((1, hw), lambda b, h, i, sl: (0, 0))],
            out_specs=pl.BlockSpec((TQ, hw), lambda b, h, i, sl: (b * nq + i, h)),
            scratch_shapes=[pltpu.VMEM((hw + ONES_ROWS, seq), _BF16),
                            pltpu.VMEM((2 * seq - TQ, TQ), _F32)],
        ),
        out_shape=jax.ShapeDtypeStruct((batch * seq, C_VW), _BF16),
        compiler_params=_cparams("parallel", "parallel", "arbitrary"),
        name="diff_attention",
    )(slopes, qkv, qkv, qkv, zf, diff_lambda.astype(_F32), subln_g.astype(_F32).reshape(1, hw))


def _dilated_group(gi, window, dil, slope, q_ref, k_ref, v_ref, og_ref, lse_ref, bias_ref):
    seq = q_ref.shape[0]
    length = seq // dil
    tq = min(D_TQ, length)
    span = min(D_SPAN, length)
    n_tiles = length // tq
    n_side = window // (2 * dil)
    half = (span - tq) // 2
    unit_pen = slope * (float(dil) * LOG2E)
    rel0 = (lax.broadcasted_iota(jnp.int32, (tq, span), 1)
            - lax.broadcasted_iota(jnp.int32, (tq, span), 0))
    for vi in range(3 if n_tiles > 1 else 1):
        rel = jnp.abs(rel0 - vi * half)
        bias_ref[vi, :tq, :span] = jnp.where(rel <= n_side, -unit_pen * rel.astype(_F32), NEG_INF)

    def rows(start, size):
        return pl.ds(start, size) if dil == 1 else pl.ds(start, size, stride=dil)

    def tile(idx, carry):
        r = lax.div(idx, n_tiles)
        u0 = lax.rem(idx, n_tiles) * tq
        ks = jnp.clip(u0 - half, 0, length - span)
        vi = lax.div(u0 - ks, half) if n_tiles > 1 else 0
        q_rows = rows(r + dil * u0, tq)
        k_rows = rows(r + dil * ks, span)
        s = _qk_t(q_ref[q_rows, :].astype(_BF16), k_ref[k_rows, :].astype(_BF16))
        s = s + bias_ref[vi, :tq, :span]
        m = jnp.max(s, axis=-1, keepdims=True)
        p = jnp.exp2(s - m)
        l = jnp.sum(p, axis=-1, keepdims=True)
        o = jnp.dot(p.astype(_BF16), v_ref[k_rows, :].astype(_BF16), preferred_element_type=_F32) / l
        og_ref[gi, q_rows, :] = o
        lse_ref[gi, q_rows, :] = jnp.broadcast_to(m + jnp.log2(l), (tq, LANES))
        return carry

    lax.fori_loop(0, dil * n_tiles, tile, 0, unroll=D_UNROLL)


def _dilated_kernel(slope_ref, q_ref, k_ref, v_ref, z_ref, o_ref, og_ref, lse_ref, bias_ref):
    h, g = pl.program_id(1), pl.program_id(2)
    for gi, (window, dil) in enumerate(D_PATTERNS):
        @pl.when(g == gi)
        def _(gi=gi, window=window, dil=dil):
            _dilated_group(gi, window, dil, slope_ref[gi * D_HEADS_PER_GROUP + h],
                           q_ref, k_ref, v_ref, og_ref, lse_ref, bias_ref)

    @pl.when(g == D_GROUPS - 1)
    def _():
        seq = o_ref.shape[0]
        chunk = min(512, seq)

        def merge(ci, carry):
            rs = pl.ds(pl.multiple_of(ci * chunk, chunk), chunk)
            lses = [lse_ref[gi, rs, :] for gi in range(D_GROUPS)]
            top = functools.reduce(jnp.maximum, lses)
            ws = [jnp.exp2(v - top) for v in lses]
            num = functools.reduce(jnp.add, [w * og_ref[gi, rs, :] for gi, w in enumerate(ws)])
            out = num / functools.reduce(jnp.add, ws)
            o_ref[rs, :] = (out * z_ref[rs, :]).astype(o_ref.dtype)
            return carry

        lax.fori_loop(0, seq // chunk, merge, 0)


def _dilated(zf, batch, seq, off_z):
    n = D_GROUPS * D_HEADS_PER_GROUP
    slopes = jnp.asarray(2.0 ** (-8.0 * np.arange(1, n + 1) / n), dtype=_F32)
    for window, dil in D_PATTERNS:
        assert seq % dil == 0 and (seq // dil) % min(D_TQ, seq // dil) == 0
        assert D_SPAN - D_TQ >= 2 * (window // (2 * dil)) or seq // dil <= D_SPAN
    qkv_spec = lambda off: pl.BlockSpec(
        (seq, HD), lambda b, h, g, sl: (b, off // HD + g * D_HEADS_PER_GROUP + h))
    return pl.pallas_call(
        _dilated_kernel,
        grid_spec=pltpu.PrefetchScalarGridSpec(
            num_scalar_prefetch=1,
            grid=(batch, D_HEADS_PER_GROUP, D_GROUPS),
            in_specs=[qkv_spec(off_z["d_q"]), qkv_spec(off_z["d_k"]), qkv_spec(off_z["d_v"]),
                      pl.BlockSpec((seq, HD), lambda b, h, g, sl: (b, off_z["d_z"] // HD + h))],
            out_specs=pl.BlockSpec((seq, HD), lambda b, h, g, sl: (b, h)),
            scratch_shapes=[pltpu.VMEM((D_GROUPS, seq, HD), _F32),
                            pltpu.VMEM((D_GROUPS, seq, LANES), _F32),
                            pltpu.VMEM((3, D_TQ, D_SPAN), _F32)],
        ),
        out_shape=jax.ShapeDtypeStruct((batch * seq, D_OW), _BF16),
        compiler_params=_cparams("parallel", "parallel", "arbitrary"),
        name="dilated_attention",
    )(slopes, zf, zf, zf, zf)


def _merge_kernel(x_ref, wg_ref, *refs, n_br):
    y_refs, wb_refs = refs[:n_br], refs[n_br:2 * n_br]
    o_ref, acc_ref = refs[2 * n_br:]
    br = pl.program_id(2)
    tm = x_ref.shape[0]

    @pl.when(br == 0)
    def _():
        acc_ref[...] = jnp.zeros_like(acc_ref)

    def add_branch(y_ref, wb_ref):
        for c in range(tm // ROW_CHUNK):
            rs = slice(c * ROW_CHUNK, (c + 1) * ROW_CHUNK)
            gate = _sigmoid(jnp.dot(x_ref[rs, :], wg_ref[...], preferred_element_type=_F32))
            acc_ref[rs, :] += gate * jnp.dot(y_ref[rs, :], wb_ref[...], preferred_element_type=_F32)

    for b in range(n_br):
        pl.when(br == b)(functools.partial(add_branch, y_refs[b], wb_refs[b]))

    @pl.when(br == n_br - 1)
    def _():
        o_ref[...] = acc_ref[...].astype(o_ref.dtype)


def _gated_merge(xn, w_b, layer, ys, wbs):
    m, d = xn.shape
    n_br = len(ys)
    assert GATE_START % TN == 0 and d % TN == 0
    gate_tile0 = GATE_START // TN
    tiles_per_branch = d // TN
    y_specs = [pl.BlockSpec((TM, y.shape[1]), lambda i, j, br: (i, 0)) for y in ys]
    wb_specs = [pl.BlockSpec((None, w.shape[1], TN), lambda i, j, br: (layer, 0, j)) for w in wbs]
    return pl.pallas_call(
        functools.partial(_merge_kernel, n_br=n_br),
        grid=(m // TM, d // TN, n_br),
        in_specs=[
            pl.BlockSpec((TM, d), lambda i, j, br: (i, 0)),
            pl.BlockSpec((None, d, TN), lambda i, j, br: (layer, 0, gate_tile0 + br * tiles_per_branch + j)),
            *y_specs, *wb_specs,
        ],
        out_specs=pl.BlockSpec((TM, TN), lambda i, j, br: (i, j)),
        out_shape=jax.ShapeDtypeStruct((m, d), _BF16),
        scratch_shapes=[pltpu.VMEM((TM, TN), _F32)],
        compiler_params=_cparams("parallel", "parallel", "arbitrary"),
        name="gated_merge",
    )(xn, w_b, *ys, *wbs)


def _out_kernel(m_ref, w_ref, x_ref, o_ref):
    for c in range(o_ref.shape[0] // ROW_CHUNK):
        rs = slice(c * ROW_CHUNK, (c + 1) * ROW_CHUNK)
        o_ref[rs, :] = x_ref[rs, :] + jnp.dot(m_ref[rs, :], w_ref[...], preferred_element_type=_F32)


def _out_projection(merged, w_out_b, layer, x):
    m, d = x.shape
    return pl.pallas_call(
        _out_kernel,
        grid=(m // TM, d // TN),
        in_specs=[pl.BlockSpec((TM, d), lambda i, j: (i, 0)),
                  pl.BlockSpec((None, d, TN), lambda i, j: (layer, 0, j)),
                  pl.BlockSpec((TM, TN), lambda i, j: (i, j))],
        out_specs=pl.BlockSpec((TM, TN), lambda i, j: (i, j)),
        out_shape=jax.ShapeDtypeStruct((m, d), _F32),
        compiler_params=_cparams("parallel", "arbitrary"),
        name="out_projection",
    )(merged, w_out_b, x)


def kernel(x, norm_g, w_in, qk_gain, na_rel_bias, diff_lambda, diff_subln_g,
           w_branch_a, w_branch_b, w_branch_c, w_branch_d, w_out):
    batch, seq, d_model = x.shape
    depth = w_in.shape[0]
    m = batch * seq
    assert w_in.shape[2] == GATE_START + N_BRANCHES * d_model
    assert m % TM == 0 and seq % TM == 0 and seq % TQ == 0 and seq % GRID_W == 0 and d_model % TN == 0

    w_in_b = w_in.astype(_BF16)
    w_out_b = w_out.astype(_BF16)
    wbs = tuple(w.astype(_BF16) for w in (w_branch_a, w_branch_b, w_branch_c, w_branch_d))
    cos_t, sin_t = _rope_tables(seq)
    src_b, modes_b, gkeys_b, off_q = _PLAN_BF16
    src_f, modes_f, gkeys_f, off_z = _PLAN_F32

    xf = x.reshape(m, d_model)
    for l in range(depth):
        xn = _rmsnorm(xf, norm_g[l])
        qkv = _projection(xn, w_in_b, l, jnp.asarray(src_b), jnp.asarray(modes_b),
                          _gain_table(qk_gain[l], gkeys_b), cos_t, sin_t, _BF16, seq,
                          tuple(sorted(set(modes_b.tolist()))))
        zf = _projection(xn, w_in_b, l, jnp.asarray(src_f), jnp.asarray(modes_f),
                         _gain_table(qk_gain[l], gkeys_f), cos_t, sin_t, _F32, seq,
                         tuple(sorted(set(modes_f.tolist()))))
        lam_init = 0.8 - 0.6 * math.exp(-0.3 * l)
        y_a = _neighbourhood(qkv, zf, _na_bias_table(na_rel_bias[l], seq // GRID_W), batch, seq, off_q, off_z)
        y_b = _gqa(qkv, zf, batch, seq, off_q, off_z)
        y_c = _diff_attention(qkv, zf, diff_lambda[l], diff_subln_g[l], lam_init, batch, seq, off_q, off_z)
        y_d = _dilated(zf, batch, seq, off_z)
        merged = _gated_merge(xn, w_in_b, l, (y_a, y_b, y_c, y_d), wbs)
        xf = _out_projection(merged, w_out_b, l, xf)
    return xf.reshape(batch, seq, d_model)
```

```python
import functools
import math

import jax
import jax.numpy as jnp
import numpy as np
from jax import lax
from jax.experimental import pallas as pl
from jax.experimental.pallas import tpu as pltpu

HD = 128
GRID_W = 64
NA_ROWS, NA_COLS = 8, 16
A_HEADS = 8
B_Q_HEADS, B_KV_HEADS = 8, 2
ROPE_THETA = 10000.0
C_HEADS = 4
D_PATTERNS = ((128, 1), (512, 4), (2048, 16))
D_GROUPS, D_HEADS_PER_GROUP = 3, 4
N_BRANCHES = 4
RMS_EPS = 1e-6
NEG_INF = -1e30
LOG2E = math.log2(math.e)

A_W = A_HEADS * HD
B_QW = B_Q_HEADS * HD
B_KVW = B_KV_HEADS * HD
C_QKW = C_HEADS * 2 * HD
C_VW = C_HEADS * 2 * HD
D_QKVW = D_GROUPS * D_HEADS_PER_GROUP * HD
D_OW = D_HEADS_PER_GROUP * HD
_SEG_NAMES = ("a_q", "a_k", "a_v", "a_z", "b_q", "b_k", "b_v", "b_z",
              "c_q", "c_k", "c_v", "c_z", "d_q", "d_k", "d_v", "d_z")
_SEG_WIDTHS = (A_W, A_W, A_W, A_W, B_QW, B_KVW, B_KVW, B_QW,
               C_QKW, C_QKW, C_VW, C_VW, D_QKVW, D_QKVW, D_QKVW, D_OW)
_SEG_START = dict(zip(_SEG_NAMES, np.cumsum((0,) + _SEG_WIDTHS[:-1]).tolist()))
GATE_START = int(sum(_SEG_WIDTHS))

LANES = 128
V7X_VMEM_BYTES = 64 * 1024 * 1024
VMEM_LIMIT = 56 * 1024 * 1024
TM = 1024
TN = 512
ROW_CHUNK = 256
TM_NORM = 256
TQ = 256
KV_CHUNK = 512
ONES_ROWS = 16
D_TQ, D_SPAN = 256, 512
NA_GROUP, NA_SUPER = 4, 16
NA_UNROLL = 2
D_UNROLL = 4

MODE_PLAIN, MODE_NORM, MODE_NORM_ROPE, MODE_KV_B, MODE_SILU = range(5)

_F32 = jnp.float32
_BF16 = jnp.bfloat16


def _cparams(*sem):
    return pltpu.CompilerParams(dimension_semantics=sem, vmem_limit_bytes=VMEM_LIMIT)


def _sigmoid(v):
    return 0.5 * jnp.tanh(0.5 * v) + 0.5


def _qk_t(q, k):
    return lax.dot_general(q, k, (((1,), (1,)), ((), ())), preferred_element_type=_F32)


def _rmsnorm_kernel(x_ref, g_ref, o_ref):
    x = x_ref[...]
    ms = jnp.mean(x * x, axis=-1, keepdims=True)
    o_ref[...] = (x * lax.rsqrt(ms + RMS_EPS) * g_ref[...]).astype(o_ref.dtype)


def _rmsnorm(x, g):
    m, d = x.shape
    return pl.pallas_call(
        _rmsnorm_kernel,
        grid=(m // TM_NORM,),
        in_specs=[pl.BlockSpec((TM_NORM, d), lambda i: (i, 0)),
                  pl.BlockSpec((1, d), lambda i: (0, 0))],
        out_specs=pl.BlockSpec((TM_NORM, d), lambda i: (i, 0)),
        out_shape=jax.ShapeDtypeStruct((m, d), _BF16),
        compiler_params=_cparams("parallel"),
        name="rmsnorm",
    )(x, g.reshape(1, d))


def _proj_kernel(src_ref, mode_ref, x_ref, w32_ref, g_ref, cos_ref, sin_ref, o_ref, w_ref, *, modes_used):
    del src_ref
    mode = mode_ref[pl.program_id(0)]
    tm, tn = o_ref.shape
    n_heads = tn // HD

    @pl.when(pl.program_id(1) == 0)
    def _():
        w_ref[...] = w32_ref[...].astype(w_ref.dtype)

    def norm_head(acc, h):
        blk = acc[:, h * HD:(h + 1) * HD]
        ms = jnp.mean(blk * blk, axis=-1, keepdims=True)
        return blk * lax.rsqrt(ms + RMS_EPS) * g_ref[:, h * HD:(h + 1) * HD]

    def rope(y, rs):
        even = (lax.broadcasted_iota(jnp.int32, y.shape, 1) % 2) == 0
        partner = jnp.where(even, pltpu.roll(y, HD - 1, 1), pltpu.roll(y, 1, 1))
        return y * cos_ref[rs, :] + partner * sin_ref[rs, :]

    def head_epilogue(mode_id, acc, h, rs):
        is_key_half = h < n_heads // 2
        if mode_id == MODE_NORM:
            return norm_head(acc, h)
        if mode_id == MODE_NORM_ROPE or (mode_id == MODE_KV_B and is_key_half):
            return rope(norm_head(acc, h), rs)
        return acc[:, h * HD:(h + 1) * HD]

    def run(mode_id):
        for c in range(tm // ROW_CHUNK):
            rs = slice(c * ROW_CHUNK, (c + 1) * ROW_CHUNK)
            acc = jnp.dot(x_ref[rs, :], w_ref[...], preferred_element_type=_F32)
            if mode_id == MODE_PLAIN:
                o_ref[rs, :] = acc.astype(o_ref.dtype)
            elif mode_id == MODE_SILU:
                o_ref[rs, :] = (acc * _sigmoid(acc)).astype(o_ref.dtype)
            else:
                for h in range(n_heads):
                    o_ref[rs, h * HD:(h + 1) * HD] = head_epilogue(mode_id, acc, h, rs).astype(o_ref.dtype)

    for mode_id in modes_used:
        pl.when(mode == mode_id)(functools.partial(run, mode_id))


def _projection(xn, w_in, layer, src_tiles, modes, gains, cos_t, sin_t, out_dtype, seq, modes_used):
    m, d = xn.shape
    n_t = src_tiles.shape[0]
    rope_blocks = seq // TM
    return pl.pallas_call(
        functools.partial(_proj_kernel, modes_used=modes_used),
        grid_spec=pltpu.PrefetchScalarGridSpec(
            num_scalar_prefetch=2,
            grid=(n_t, m // TM),
            in_specs=[
                pl.BlockSpec((TM, d), lambda j, i, src, md: (i, 0)),
                pl.BlockSpec((None, d, TN), lambda j, i, src, md: (layer, 0, src[j])),
                pl.BlockSpec((None, 1, TN), lambda j, i, src, md: (j, 0, 0)),
                pl.BlockSpec((TM, HD), lambda j, i, src, md: (i % rope_blocks, 0)),
                pl.BlockSpec((TM, HD), lambda j, i, src, md: (i % rope_blocks, 0)),
            ],
            out_specs=pl.BlockSpec((TM, TN), lambda j, i, src, md: (i, j)),
            scratch_shapes=[pltpu.VMEM((d, TN), _BF16)],
        ),
        out_shape=jax.ShapeDtypeStruct((m, n_t * TN), out_dtype),
        compiler_params=_cparams("parallel", "arbitrary"),
        name="projection",
    )(src_tiles, modes, xn, w_in, gains, cos_t, sin_t)


def _tile_plan(segments):
    src, modes, gain_keys, out_start = [], [], [], {}
    col = 0
    for name, mode, gain_key in segments:
        if name == "b_kv":
            start, width = _SEG_START["b_k"], 2 * B_KVW
            assert width == TN and _SEG_START["b_v"] == start + B_KVW
            out_start["b_k"], out_start["b_v"] = col, col + B_KVW
        else:
            start, width = _SEG_START[name], _SEG_WIDTHS[_SEG_NAMES.index(name)]
            out_start[name] = col
        assert start % TN == 0 and width % TN == 0, (name, start, width)
        for t in range(width // TN):
            src.append(start // TN + t)
            modes.append(mode)
            gain_keys.append(gain_key)
        col += width
    return np.asarray(src, np.int32), np.asarray(modes, np.int32), gain_keys, out_start


_Q_GAIN = HD ** -0.5 * LOG2E
_PLAN_BF16 = _tile_plan((
    ("a_q", MODE_NORM, (0, 0)), ("a_k", MODE_NORM, (0, 1)), ("a_v", MODE_PLAIN, None),
    ("b_q", MODE_NORM_ROPE, (1, 0)), ("b_kv", MODE_KV_B, (1, 1)),
    ("c_q", MODE_NORM, (2, 0)), ("c_k", MODE_NORM, (2, 1)), ("c_v", MODE_PLAIN, None)))
_PLAN_F32 = _tile_plan((
    ("a_z", MODE_SILU, None), ("b_z", MODE_SILU, None), ("c_z", MODE_SILU, None),
    ("d_q", MODE_NORM, (3, 0)), ("d_k", MODE_NORM, (3, 1)), ("d_v", MODE_PLAIN, None),
    ("d_z", MODE_SILU, None)))


def _gain_table(qk_gain, gain_keys):
    rows = []
    for key in gain_keys:
        if key is None:
            rows.append(jnp.ones((TN,), _F32))
        else:
            g = qk_gain[key[0], key[1]].astype(_F32)
            if key[1] == 0:
                g = g * _Q_GAIN
            rows.append(jnp.tile(g, TN // HD))
    return jnp.stack(rows)[:, None, :]


def _rope_tables(seq):
    t = jnp.arange(seq)
    row = (t // GRID_W).astype(_F32)
    col = (t % GRID_W).astype(_F32)
    n_pairs = HD // 4
    inv_freq = ROPE_THETA ** (-jnp.arange(n_pairs, dtype=_F32) / n_pairs)
    ang = jnp.concatenate([row[:, None] * inv_freq, col[:, None] * inv_freq], axis=-1)
    cos, sin = jnp.cos(ang), jnp.sin(ang)
    cos_t = jnp.repeat(cos, 2, axis=-1)
    sin_t = jnp.stack([-sin, sin], axis=-1).reshape(seq, HD)
    return cos_t, sin_t


def _na_window_starts(rows):
    starts = {}
    for r in range(rows):
        w0 = int(np.clip(NA_GROUP * (r // NA_GROUP) - NA_ROWS // 2, 0, rows - NA_SUPER))
        r0 = int(np.clip(r - NA_ROWS // 2, 0, rows - NA_ROWS))
        assert 0 <= r0 - w0 and r0 - w0 + NA_ROWS <= NA_SUPER and 0 <= r - w0 < NA_SUPER
        assert starts.setdefault(r - w0, r0 - w0) == r0 - w0
    return starts


def _na_bias_kernel(rb_ref, o_ref, base_ref, *, win_starts):
    h = pl.program_id(0)
    n_row_off, n_col_off = 2 * NA_ROWS - 1, 2 * NA_COLS - 1
    c = lax.broadcasted_iota(jnp.int32, (GRID_W, GRID_W), 0)
    kc = lax.broadcasted_iota(jnp.int32, (GRID_W, GRID_W), 1)
    c0 = jnp.clip(c - NA_COLS // 2, 0, GRID_W - NA_COLS)
    valid = (kc >= c0) & (kc < c0 + NA_COLS)
    col_off = kc - c + (NA_COLS - 1)
    masked = jnp.full((GRID_W, GRID_W), NEG_INF, _F32)
    for row_off in range(n_row_off):
        blk = masked
        for d in range(n_col_off):
            blk = jnp.where(col_off == d, rb_ref[(h * n_row_off + row_off) * n_col_off + d] * LOG2E, blk)
        base_ref[row_off] = jnp.where(valid, blk, NEG_INF)
    for dw in range(NA_SUPER):
        ws = win_starts.get(dw, 0)
        for i in range(NA_SUPER):
            in_window = ws <= i < ws + NA_ROWS
            o_ref[dw, :, i * GRID_W:(i + 1) * GRID_W] = base_ref[i - dw + NA_ROWS - 1] if in_window else masked


def _na_bias_table(rel_bias, rows):
    heads = rel_bias.shape[0]
    return pl.pallas_call(
        functools.partial(_na_bias_kernel, win_starts=_na_window_starts(rows)),
        grid=(heads,),
        in_specs=[pl.BlockSpec(memory_space=pltpu.SMEM)],
        out_specs=pl.BlockSpec((None, NA_SUPER, GRID_W, NA_SUPER * GRID_W), lambda h: (h, 0, 0, 0)),
        out_shape=jax.ShapeDtypeStruct((heads, NA_SUPER, GRID_W, NA_SUPER * GRID_W), _F32),
        scratch_shapes=[pltpu.VMEM((2 * NA_ROWS - 1, GRID_W, GRID_W), _F32)],
        compiler_params=_cparams("parallel"),
        name="na_bias_table",
    )(rel_bias.astype(_F32).reshape(-1))


def _na_kernel(q_ref, k_ref, v_ref, z_ref, bias_ref, o_ref, *, rows):
    gq = NA_GROUP * GRID_W
    sw = NA_SUPER * GRID_W

    def row_group(g, carry):
        w0 = jnp.clip(NA_GROUP * g - NA_ROWS // 2, 0, rows - NA_SUPER)
        q0 = pl.multiple_of(g * gq, gq)
        k0 = pl.multiple_of(w0 * GRID_W, GRID_W)
        bias = bias_ref[pl.ds(NA_GROUP * g - w0, NA_GROUP)].reshape(gq, sw)
        s = _qk_t(q_ref[pl.ds(q0, gq), :], k_ref[pl.ds(k0, sw), :]) + bias
        m = jnp.max(s, axis=-1, keepdims=True)
        p = jnp.exp2(s - m)
        l = jnp.sum(p, axis=-1, keepdims=True)
        o = jnp.dot(p.astype(_BF16), v_ref[pl.ds(k0, sw), :], preferred_element_type=_F32) / l
        o_ref[pl.ds(q0, gq), :] = (o * z_ref[pl.ds(q0, gq), :]).astype(o_ref.dtype)
        return carry

    lax.fori_loop(0, rows // NA_GROUP, row_group, 0, unroll=NA_UNROLL)


def _neighbourhood(qkv, zf, bias_tab, batch, seq, off_q, off_z):
    rows = seq // GRID_W
    assert rows >= NA_SUPER and rows % NA_GROUP == 0
    blk = lambda off: pl.BlockSpec((seq, HD), lambda b, h: (b, off // HD + h))
    return pl.pallas_call(
        functools.partial(_na_kernel, rows=rows),
        grid=(batch, A_HEADS),
        in_specs=[blk(off_q["a_q"]), blk(off_q["a_k"]), blk(off_q["a_v"]), blk(off_z["a_z"]),
                  pl.BlockSpec((None, NA_SUPER, GRID_W, NA_SUPER * GRID_W), lambda b, h: (h, 0, 0, 0))],
        out_specs=pl.BlockSpec((seq, HD), lambda b, h: (b, h)),
        out_shape=jax.ShapeDtypeStruct((batch * seq, A_W), _BF16),
        compiler_params=_cparams("parallel", "parallel"),
        name="neighbourhood_attention",
    )(qkv, qkv, qkv, zf, bias_tab)


def _store_transposed(vt_ref, v_ref):
    seq, d = v_ref.shape
    for c in range(seq // KV_CHUNK):
        cs = slice(c * KV_CHUNK, (c + 1) * KV_CHUNK)
        vt_ref[:d, cs] = v_ref[cs, :].astype(_F32).T.astype(vt_ref.dtype)
    vt_ref[d:, :] = jnp.ones((vt_ref.shape[0] - d, seq), vt_ref.dtype)


class _OnlineSoftmax:
    def __init__(self, d_v, n_q):
        self.d_v = d_v
        self.m = jnp.full((1, n_q), NEG_INF, _F32)
        self.acc = jnp.zeros((d_v + ONES_ROWS, n_q), _F32)

    def update(self, s, vt_chunk):
        m_new = jnp.maximum(self.m, jnp.max(s, axis=0, keepdims=True))
        p = jnp.exp2(s - m_new).astype(_BF16)
        self.acc = (jnp.exp2(self.m - m_new) * self.acc
                    + jnp.dot(vt_chunk, p, preferred_element_type=_F32))
        self.m = m_new

    def result(self):
        return self.acc[:self.d_v] / self.acc[self.d_v:self.d_v + 1]


def _gqa_kernel(q_ref, k_ref, v_ref, z_ref, o_ref):
    k = k_ref[...]
    v = v_ref[...]
    for g in range(q_ref.shape[1] // HD):
        sl = slice(g * HD, (g + 1) * HD)
        s = _qk_t(q_ref[:, sl], k)
        m = jnp.max(s, axis=-1, keepdims=True)
        p = jnp.exp2(s - m)
        l = jnp.sum(p, axis=-1, keepdims=True)
        o = jnp.dot(p.astype(_BF16), v, preferred_element_type=_F32) / l
        o_ref[:, sl] = (o * z_ref[:, sl]).astype(o_ref.dtype)


def _gqa(qkv, zf, batch, seq, off_q, off_z):
    group_w = (B_Q_HEADS // B_KV_HEADS) * HD
    nq = seq // TQ
    q_spec = lambda off: pl.BlockSpec((TQ, group_w), lambda b, hk, i: (b * nq + i, off // group_w + hk))
    kv_spec = lambda off: pl.BlockSpec((seq, HD), lambda b, hk, i: (b, off // HD + hk))
    return pl.pallas_call(
        _gqa_kernel,
        grid=(batch, B_KV_HEADS, nq),
        in_specs=[q_spec(off_q["b_q"]), kv_spec(off_q["b_k"]), kv_spec(off_q["b_v"]), q_spec(off_z["b_z"])],
        out_specs=pl.BlockSpec((TQ, group_w), lambda b, hk, i: (b * nq + i, hk)),
        out_shape=jax.ShapeDtypeStruct((batch * seq, B_QW), _BF16),
        compiler_params=_cparams("parallel", "parallel", "parallel"),
        name="gqa_attention",
    )(qkv, qkv, qkv, zf)


def _diff_kernel(slope_ref, q_ref, k_ref, v_ref, z_ref, lam_ref, g_ref, o_ref, vt_ref, pen_ref, *, lam_init):
    h, i = pl.program_id(1), pl.program_id(2)
    seq = k_ref.shape[0]
    tq, hw = q_ref.shape
    key0 = seq - tq

    @pl.when(i == 0)
    def _():
        _store_transposed(vt_ref, v_ref)
        slope = slope_ref[h] * LOG2E
        n_rows = pen_ref.shape[0]
        for y0 in range(0, n_rows, KV_CHUNK):
            size = min(KV_CHUNK, n_rows - y0)
            row_minus_col = (lax.broadcasted_iota(jnp.int32, (size, tq), 0)
                             - lax.broadcasted_iota(jnp.int32, (size, tq), 1))
            pen_ref[y0:y0 + size, :] = slope * jnp.abs(row_minus_col + (y0 - key0)).astype(_F32)

    lp = lam_ref[...]
    lam = (jnp.exp(jnp.sum(lp[0:1] * lp[1:2], axis=-1, keepdims=True))
           - jnp.exp(jnp.sum(lp[2:3] * lp[3:4], axis=-1, keepdims=True)) + lam_init)

    states = [_OnlineSoftmax(hw, tq) for _ in range(2)]
    pen0 = pl.multiple_of(key0 - i * tq, tq)
    for c in range(seq // KV_CHUNK):
        cs = slice(c * KV_CHUNK, (c + 1) * KV_CHUNK)
        pen = pen_ref[pl.ds(pen0 + c * KV_CHUNK, KV_CHUNK), :]
        for mi, state in enumerate(states):
            sl = slice(mi * HD, (mi + 1) * HD)
            state.update(_qk_t(k_ref[cs, sl], q_ref[:, sl]) - pen, vt_ref[:, cs])
    o = (states[0].result() - lam * states[1].result()).T
    ms = jnp.mean(o * o, axis=-1, keepdims=True)
    o = o * lax.rsqrt(ms + RMS_EPS) * g_ref[...] * (1.0 - lam_init)
    o_ref[...] = (o * z_ref[...]).astype(o_ref.dtype)


def _diff_attention(qkv, zf, diff_lambda, subln_g, lam_init, batch, seq, off_q, off_z):
    hw = 2 * HD
    nq = seq // TQ
    slopes = jnp.asarray(2.0 ** (-8.0 * np.arange(1, C_HEADS + 1) / C_HEADS), dtype=_F32)
    q_spec = lambda off: pl.BlockSpec((TQ, hw), lambda b, h, i, sl: (b * nq + i, off // hw + h))
    kv_spec = lambda off: pl.BlockSpec((seq, hw), lambda b, h, i, sl: (b, off // hw + h))
    return pl.pallas_call(
        functools.partial(_diff_kernel, lam_init=lam_init),
        grid_spec=pltpu.PrefetchScalarGridSpec(
            num_scalar_prefetch=1,
            grid=(batch, C_HEADS, nq),
            in_specs=[q_spec(off_q["c_q"]), kv_spec(off_q["c_k"]), kv_spec(off_q["c_v"]), q_spec(off_z["c_z"]),
                      pl.BlockSpec((4, HD), lambda b, h, i, sl: (0, 0)),
                      pl.BlockSpec((1, hw), lambda b, h, i, sl: (0, 0))],
            out_specs=pl.BlockSpec((TQ, hw), lambda b, h, i, sl: (b * nq + i, h)),
            scratch_shapes=[pltpu.VMEM((hw + ONES_ROWS, seq), _BF16),
                            pltpu.VMEM((2 * seq - TQ, TQ), _F32)],
        ),
        out_shape=jax.ShapeDtypeStruct((batch * seq, C_VW), _BF16),
        compiler_params=_cparams("parallel", "parallel", "arbitrary"),
        name="diff_attention",
    )(slopes, qkv, qkv, qkv, zf, diff_lambda.astype(_F32), subln_g.astype(_F32).reshape(1, hw))


def _dilated_group(gi, window, dil, slope, q_ref, k_ref, v_ref, og_ref, lse_ref, bias_ref):
    seq = q_ref.shape[0]
    length = seq // dil
    tq = min(D_TQ, length)
    span = min(D_SPAN, length)
    n_tiles = length // tq
    n_side = window // (2 * dil)
    half = (span - tq) // 2
    unit_pen = slope * (float(dil) * LOG2E)
    rel0 = (lax.broadcasted_iota(jnp.int32, (tq, span), 1)
            - lax.broadcasted_iota(jnp.int32, (tq, span), 0))
    for vi in range(3 if n_tiles > 1 else 1):
        rel = jnp.abs(rel0 - vi * half)
        bias_ref[vi, :tq, :span] = jnp.where(rel <= n_side, -unit_pen * rel.astype(_F32), NEG_INF)

    def rows(start, size):
        return pl.ds(start, size) if dil == 1 else pl.ds(start, size, stride=dil)

    def tile(idx, carry):
        r = lax.div(idx, n_tiles)
        u0 = lax.rem(idx, n_tiles) * tq
        ks = jnp.clip(u0 - half, 0, length - span)
        vi = lax.div(u0 - ks, half) if n_tiles > 1 else 0
        q_rows = rows(r + dil * u0, tq)
        k_rows = rows(r + dil * ks, span)
        s = _qk_t(q_ref[q_rows, :].astype(_BF16), k_ref[k_rows, :].astype(_BF16))
        s = s + bias_ref[vi, :tq, :span]
        m = jnp.max(s, axis=-1, keepdims=True)
        p = jnp.exp2(s - m)
        l = jnp.sum(p, axis=-1, keepdims=True)
        o = jnp.dot(p.astype(_BF16), v_ref[k_rows, :].astype(_BF16), preferred_element_type=_F32) / l
        og_ref[gi, q_rows, :] = o
        lse_ref[gi, q_rows, :] = jnp.broadcast_to(m + jnp.log2(l), (tq, LANES))
        return carry

    lax.fori_loop(0, dil * n_tiles, tile, 0, unroll=D_UNROLL)


def _dilated_kernel(slope_ref, q_ref, k_ref, v_ref, z_ref, o_ref, og_ref, lse_ref, bias_ref):
    h, g = pl.program_id(1), pl.program_id(2)
    for gi, (window, dil) in enumerate(D_PATTERNS):
        @pl.when(g == gi)
        def _(gi=gi, window=window, dil=dil):
            _dilated_group(gi, window, dil, slope_ref[gi * D_HEADS_PER_GROUP + h],
                           q_ref, k_ref, v_ref, og_ref, lse_ref, bias_ref)

    @pl.when(g == D_GROUPS - 1)
    def _():
        seq = o_ref.shape[0]
        chunk = min(512, seq)

        def merge(ci, carry):
            rs = pl.ds(pl.multiple_of(ci * chunk, chunk), chunk)
            lses = [lse_ref[gi, rs, :] for gi in range(D_GROUPS)]
            top = functools.reduce(jnp.maximum, lses)
            ws = [jnp.exp2(v - top) for v in lses]
            num = functools.reduce(jnp.add, [w * og_ref[gi, rs, :] for gi, w in enumerate(ws)])
            out = num / functools.reduce(jnp.add, ws)
            o_ref[rs, :] = (out * z_ref[rs, :]).astype(o_ref.dtype)
            return carry

        lax.fori_loop(0, seq // chunk, merge, 0)


def _dilated(zf, batch, seq, off_z):
    n = D_GROUPS * D_HEADS_PER_GROUP
    slopes = jnp.asarray(2.0 ** (-8.0 * np.arange(1, n + 1) / n), dtype=_F32)
    for window, dil in D_PATTERNS:
        assert seq % dil == 0 and (seq // dil) % min(D_TQ, seq // dil) == 0
        assert D_SPAN - D_TQ >= 2 * (window // (2 * dil)) or seq // dil <= D_SPAN
    qkv_spec = lambda off: pl.BlockSpec(
        (seq, HD), lambda b, h, g, sl: (b, off // HD + g * D_HEADS_PER_GROUP + h))
    return pl.pallas_call(
        _dilated_kernel,
        grid_spec=pltpu.PrefetchScalarGridSpec(
            num_scalar_prefetch=1,
            grid=(batch, D_HEADS_PER_GROUP, D_GROUPS),
            in_specs=[qkv_spec(off_z["d_q"]), qkv_spec(off_z["d_k"]), qkv_spec(off_z["d_v"]),
                      pl.BlockSpec((seq, HD), lambda b, h, g, sl: (b, off_z["d_z"] // HD + h))],
            out_specs=pl.BlockSpec((seq, HD), lambda b, h, g, sl: (b, h)),
            scratch_shapes=[pltpu.VMEM((D_GROUPS, seq, HD), _F32),
                            pltpu.VMEM((D_GROUPS, seq, LANES), _F32),
                            pltpu.VMEM((3, D_TQ, D_SPAN), _F32)],
        ),
        out_shape=jax.ShapeDtypeStruct((batch * seq, D_OW), _BF16),
        compiler_params=_cparams("parallel", "parallel", "arbitrary"),
        name="dilated_attention",
    )(slopes, zf, zf, zf, zf)


def _merge_kernel(x_ref, wg_ref, *refs, n_br):
    y_refs, wb_refs = refs[:n_br], refs[n_br:2 * n_br]
    o_ref, acc_ref = refs[2 * n_br:]
    br = pl.program_id(2)
    tm = x_ref.shape[0]

    @pl.when(br == 0)
    def _():
        acc_ref[...] = jnp.zeros_like(acc_ref)

    def add_branch(y_ref, wb_ref):
        for c in range(tm // ROW_CHUNK):
            rs = slice(c * ROW_CHUNK, (c + 1) * ROW_CHUNK)
            gate = _sigmoid(jnp.dot(x_ref[rs, :], wg_ref[...], preferred_element_type=_F32))
            acc_ref[rs, :] += gate * jnp.dot(y_ref[rs, :], wb_ref[...], preferred_element_type=_F32)

    for b in range(n_br):
        pl.when(br == b)(functools.partial(add_branch, y_refs[b], wb_refs[b]))

    @pl.when(br == n_br - 1)
    def _():
        o_ref[...] = acc_ref[...].astype(o_ref.dtype)


def _gated_merge(xn, w_b, layer, ys, wbs, gate_start):
    m, d = xn.shape
    n_br = len(ys)
    assert gate_start % TN == 0 and d % TN == 0
    gate_tile0 = gate_start // TN
    tiles_per_branch = d // TN
    y_specs = [pl.BlockSpec((TM, y.shape[1]), lambda i, j, br: (i, 0)) for y in ys]
    wb_specs = [pl.BlockSpec((None, w.shape[1], TN), lambda i, j, br: (layer, 0, j)) for w in wbs]
    return pl.pallas_call(
        functools.partial(_merge_kernel, n_br=n_br),
        grid=(m // TM, d // TN, n_br),
        in_specs=[
            pl.BlockSpec((TM, d), lambda i, j, br: (i, 0)),
            pl.BlockSpec((None, d, TN), lambda i, j, br: (layer, 0, gate_tile0 + br * tiles_per_branch + j)),
            *y_specs, *wb_specs,
        ],
        out_specs=pl.BlockSpec((TM, TN), lambda i, j, br: (i, j)),
        out_shape=jax.ShapeDtypeStruct((m, d), _BF16),
        scratch_shapes=[pltpu.VMEM((TM, TN), _F32)],
        compiler_params=_cparams("parallel", "parallel", "arbitrary"),
        name="gated_merge",
    )(xn, w_b, *ys, *wbs)


def _out_kernel(m_ref, w32_ref, x_ref, o_ref, w_ref):
    @pl.when(pl.program_id(1) == 0)
    def _():
        w_ref[...] = w32_ref[...].astype(w_ref.dtype)

    for c in range(o_ref.shape[0] // ROW_CHUNK):
        rs = slice(c * ROW_CHUNK, (c + 1) * ROW_CHUNK)
        o_ref[rs, :] = x_ref[rs, :] + jnp.dot(m_ref[rs, :], w_ref[...], preferred_element_type=_F32)


def _out_projection(merged, w_out, layer, x):
    m, d = x.shape
    return pl.pallas_call(
        _out_kernel,
        grid=(d // TN, m // TM),
        in_specs=[pl.BlockSpec((TM, d), lambda j, i: (i, 0)),
                  pl.BlockSpec((None, d, TN), lambda j, i: (layer, 0, j)),
                  pl.BlockSpec((TM, TN), lambda j, i: (i, j))],
        out_specs=pl.BlockSpec((TM, TN), lambda j, i: (i, j)),
        out_shape=jax.ShapeDtypeStruct((m, d), _F32),
        scratch_shapes=[pltpu.VMEM((d, TN), _BF16)],
        compiler_params=_cparams("parallel", "arbitrary"),
        name="out_projection",
    )(merged, w_out, x)


def kernel(x, norm_g, w_in, qk_gain, na_rel_bias, diff_lambda, diff_subln_g,
           w_branch_a, w_branch_b, w_branch_c, w_branch_d, w_out):
    batch, seq, d_model = x.shape
    depth = w_in.shape[0]
    m = batch * seq
    assert w_in.shape[2] == GATE_START + N_BRANCHES * d_model
    assert m % TM == 0 and seq % TM == 0 and seq % TQ == 0 and seq % GRID_W == 0 and d_model % TN == 0

    w_gate_b = w_in[:, :, GATE_START:].astype(_BF16)
    wbs = tuple(w.astype(_BF16) for w in (w_branch_a, w_branch_b, w_branch_c, w_branch_d))
    cos_t, sin_t = _rope_tables(seq)
    src_b, modes_b, gkeys_b, off_q = _PLAN_BF16
    src_f, modes_f, gkeys_f, off_z = _PLAN_F32

    xf = x.reshape(m, d_model)
    for l in range(depth):
        xn = _rmsnorm(xf, norm_g[l])
        qkv = _projection(xn, w_in, l, jnp.asarray(src_b), jnp.asarray(modes_b),
                          _gain_table(qk_gain[l], gkeys_b), cos_t, sin_t, _BF16, seq,
                          tuple(sorted(set(modes_b.tolist()))))
        zf = _projection(xn, w_in, l, jnp.asarray(src_f), jnp.asarray(modes_f),
                         _gain_table(qk_gain[l], gkeys_f), cos_t, sin_t, _F32, seq,
                         tuple(sorted(set(modes_f.tolist()))))
        lam_init = 0.8 - 0.6 * math.exp(-0.3 * l)
        y_a = _neighbourhood(qkv, zf, _na_bias_table(na_rel_bias[l], seq // GRID_W), batch, seq, off_q, off_z)
        y_b = _gqa(qkv, zf, batch, seq, off_q, off_z)
        y_c = _diff_attention(qkv, zf, diff_lambda[l], diff_subln_g[l], lam_init, batch, seq, off_q, off_z)
        y_d = _dilated(zf, batch, seq, off_z)
        merged = _gated_merge(xn, w_gate_b, l, (y_a, y_b, y_c, y_d), wbs, 0)
        xf = _out_projection(merged, w_out, l, xf)
    return xf.reshape(batch, seq, d_model)
```

```python
import functools
import math

import jax
import jax.numpy as jnp
import numpy as np
from jax import lax
from jax.experimental import pallas as pl
from jax.experimental.pallas import tpu as pltpu

HD = 128
GRID_W = 64
NA_ROWS, NA_COLS = 8, 16
A_HEADS = 8
B_Q_HEADS, B_KV_HEADS = 8, 2
ROPE_THETA = 10000.0
C_HEADS = 4
D_PATTERNS = ((128, 1), (512, 4), (2048, 16))
D_GROUPS, D_HEADS_PER_GROUP = 3, 4
N_BRANCHES = 4
RMS_EPS = 1e-6
NEG_INF = -1e30
LOG2E = math.log2(math.e)

A_W = A_HEADS * HD
B_QW = B_Q_HEADS * HD
B_KVW = B_KV_HEADS * HD
C_QKW = C_HEADS * 2 * HD
C_VW = C_HEADS * 2 * HD
D_QKVW = D_GROUPS * D_HEADS_PER_GROUP * HD
D_OW = D_HEADS_PER_GROUP * HD
_SEG_NAMES = ("a_q", "a_k", "a_v", "a_z", "b_q", "b_k", "b_v", "b_z",
              "c_q", "c_k", "c_v", "c_z", "d_q", "d_k", "d_v", "d_z")
_SEG_WIDTHS = (A_W, A_W, A_W, A_W, B_QW, B_KVW, B_KVW, B_QW,
               C_QKW, C_QKW, C_VW, C_VW, D_QKVW, D_QKVW, D_QKVW, D_OW)
_SEG_START = dict(zip(_SEG_NAMES, np.cumsum((0,) + _SEG_WIDTHS[:-1]).tolist()))
GATE_START = int(sum(_SEG_WIDTHS))

LANES = 128
V7X_VMEM_BYTES = 64 * 1024 * 1024
VMEM_LIMIT = 56 * 1024 * 1024
TM = 1024
TN = 512
ROW_CHUNK = 256
TM_NORM = 256
TQ = 256
KV_CHUNK = 512
ONES_ROWS = 16
D_TQ, D_SPAN = 256, 512
NA_GROUP, NA_SUPER = 4, 16
NA_UNROLL = 4
D_UNROLL = 8

MODE_PLAIN, MODE_NORM, MODE_NORM_ROPE, MODE_KV_B, MODE_SILU = range(5)

_F32 = jnp.float32
_BF16 = jnp.bfloat16


def _cparams(*sem):
    return pltpu.CompilerParams(dimension_semantics=sem, vmem_limit_bytes=VMEM_LIMIT)


def _sigmoid(v):
    return 0.5 * jnp.tanh(0.5 * v) + 0.5


def _qk_t(q, k):
    return lax.dot_general(q, k, (((1,), (1,)), ((), ())), preferred_element_type=_F32)


def _rmsnorm_kernel(x_ref, g_ref, o_ref):
    x = x_ref[...]
    ms = jnp.mean(x * x, axis=-1, keepdims=True)
    o_ref[...] = (x * lax.rsqrt(ms + RMS_EPS) * g_ref[...]).astype(o_ref.dtype)


def _rmsnorm(x, g):
    m, d = x.shape
    return pl.pallas_call(
        _rmsnorm_kernel,
        grid=(m // TM_NORM,),
        in_specs=[pl.BlockSpec((TM_NORM, d), lambda i: (i, 0)),
                  pl.BlockSpec((1, d), lambda i: (0, 0))],
        out_specs=pl.BlockSpec((TM_NORM, d), lambda i: (i, 0)),
        out_shape=jax.ShapeDtypeStruct((m, d), _BF16),
        compiler_params=_cparams("parallel"),
        name="rmsnorm",
    )(x, g.reshape(1, d))


def _proj_kernel(src_ref, mode_ref, x_ref, w32_ref, g_ref, cos_ref, sin_ref, o_ref, w_ref, *, modes_used):
    del src_ref
    mode = mode_ref[pl.program_id(0)]
    tm, tn = o_ref.shape
    n_heads = tn // HD

    @pl.when(pl.program_id(1) == 0)
    def _():
        w_ref[...] = w32_ref[...].astype(w_ref.dtype)

    def norm_head(acc, h):
        blk = acc[:, h * HD:(h + 1) * HD]
        ms = jnp.mean(blk * blk, axis=-1, keepdims=True)
        return blk * lax.rsqrt(ms + RMS_EPS) * g_ref[:, h * HD:(h + 1) * HD]

    def rope(y, rs):
        even = (lax.broadcasted_iota(jnp.int32, y.shape, 1) % 2) == 0
        partner = jnp.where(even, pltpu.roll(y, HD - 1, 1), pltpu.roll(y, 1, 1))
        return y * cos_ref[rs, :] + partner * sin_ref[rs, :]

    def head_epilogue(mode_id, acc, h, rs):
        is_key_half = h < n_heads // 2
        if mode_id == MODE_NORM:
            return norm_head(acc, h)
        if mode_id == MODE_NORM_ROPE or (mode_id == MODE_KV_B and is_key_half):
            return rope(norm_head(acc, h), rs)
        return acc[:, h * HD:(h + 1) * HD]

    def run(mode_id):
        for c in range(tm // ROW_CHUNK):
            rs = slice(c * ROW_CHUNK, (c + 1) * ROW_CHUNK)
            acc = jnp.dot(x_ref[rs, :], w_ref[...], preferred_element_type=_F32)
            if mode_id == MODE_PLAIN:
                o_ref[rs, :] = acc.astype(o_ref.dtype)
            elif mode_id == MODE_SILU:
                o_ref[rs, :] = (acc * _sigmoid(acc)).astype(o_ref.dtype)
            else:
                for h in range(n_heads):
                    o_ref[rs, h * HD:(h + 1) * HD] = head_epilogue(mode_id, acc, h, rs).astype(o_ref.dtype)

    for mode_id in modes_used:
        pl.when(mode == mode_id)(functools.partial(run, mode_id))


def _projection(xn, w_in, layer, src_tiles, modes, gains, cos_t, sin_t, out_dtype, seq, modes_used):
    m, d = xn.shape
    n_t = src_tiles.shape[0]
    rope_blocks = seq // TM
    return pl.pallas_call(
        functools.partial(_proj_kernel, modes_used=modes_used),
        grid_spec=pltpu.PrefetchScalarGridSpec(
            num_scalar_prefetch=2,
            grid=(n_t, m // TM),
            in_specs=[
                pl.BlockSpec((TM, d), lambda j, i, src, md: (i, 0)),
                pl.BlockSpec((None, d, TN), lambda j, i, src, md: (layer, 0, src[j])),
                pl.BlockSpec((None, 1, TN), lambda j, i, src, md: (j, 0, 0)),
                pl.BlockSpec((TM, HD), lambda j, i, src, md: (i % rope_blocks, 0)),
                pl.BlockSpec((TM, HD), lambda j, i, src, md: (i % rope_blocks, 0)),
            ],
            out_specs=pl.BlockSpec((TM, TN), lambda j, i, src, md: (i, j)),
            scratch_shapes=[pltpu.VMEM((d, TN), _BF16)],
        ),
        out_shape=jax.ShapeDtypeStruct((m, n_t * TN), out_dtype),
        compiler_params=_cparams("parallel", "arbitrary"),
        name="projection",
    )(src_tiles, modes, xn, w_in, gains, cos_t, sin_t)


def _tile_plan(segments):
    src, modes, gain_keys, out_start = [], [], [], {}
    col = 0
    for name, mode, gain_key in segments:
        if name == "b_kv":
            start, width = _SEG_START["b_k"], 2 * B_KVW
            assert width == TN and _SEG_START["b_v"] == start + B_KVW
            out_start["b_k"], out_start["b_v"] = col, col + B_KVW
        else:
            start, width = _SEG_START[name], _SEG_WIDTHS[_SEG_NAMES.index(name)]
            out_start[name] = col
        assert start % TN == 0 and width % TN == 0, (name, start, width)
        for t in range(width // TN):
            src.append(start // TN + t)
            modes.append(mode)
            gain_keys.append(gain_key)
        col += width
    return np.asarray(src, np.int32), np.asarray(modes, np.int32), gain_keys, out_start


_Q_GAIN = HD ** -0.5 * LOG2E
_PLAN_BF16 = _tile_plan((
    ("a_q", MODE_NORM, (0, 0)), ("a_k", MODE_NORM, (0, 1)), ("a_v", MODE_PLAIN, None),
    ("b_q", MODE_NORM_ROPE, (1, 0)), ("b_kv", MODE_KV_B, (1, 1)),
    ("c_q", MODE_NORM, (2, 0)), ("c_k", MODE_NORM, (2, 1)), ("c_v", MODE_PLAIN, None)))
_PLAN_F32 = _tile_plan((
    ("a_z", MODE_SILU, None), ("b_z", MODE_SILU, None), ("c_z", MODE_SILU, None),
    ("d_q", MODE_NORM, (3, 0)), ("d_k", MODE_NORM, (3, 1)), ("d_v", MODE_PLAIN, None),
    ("d_z", MODE_SILU, None)))


def _gain_table(qk_gain, gain_keys):
    rows = []
    for key in gain_keys:
        if key is None:
            rows.append(jnp.ones((TN,), _F32))
        else:
            g = qk_gain[key[0], key[1]].astype(_F32)
            if key[1] == 0:
                g = g * _Q_GAIN
            rows.append(jnp.tile(g, TN // HD))
    return jnp.stack(rows)[:, None, :]


def _rope_tables(seq):
    t = jnp.arange(seq)
    row = (t // GRID_W).astype(_F32)
    col = (t % GRID_W).astype(_F32)
    n_pairs = HD // 4
    inv_freq = ROPE_THETA ** (-jnp.arange(n_pairs, dtype=_F32) / n_pairs)
    ang = jnp.concatenate([row[:, None] * inv_freq, col[:, None] * inv_freq], axis=-1)
    cos, sin = jnp.cos(ang), jnp.sin(ang)
    cos_t = jnp.repeat(cos, 2, axis=-1)
    sin_t = jnp.stack([-sin, sin], axis=-1).reshape(seq, HD)
    return cos_t, sin_t


def _na_window_starts(rows):
    starts = {}
    for r in range(rows):
        w0 = int(np.clip(NA_GROUP * (r // NA_GROUP) - NA_ROWS // 2, 0, rows - NA_SUPER))
        r0 = int(np.clip(r - NA_ROWS // 2, 0, rows - NA_ROWS))
        assert 0 <= r0 - w0 and r0 - w0 + NA_ROWS <= NA_SUPER and 0 <= r - w0 < NA_SUPER
        assert starts.setdefault(r - w0, r0 - w0) == r0 - w0
    return starts


def _na_bias_kernel(rb_ref, o_ref, base_ref, *, win_starts):
    h = pl.program_id(0)
    n_row_off, n_col_off = 2 * NA_ROWS - 1, 2 * NA_COLS - 1
    c = lax.broadcasted_iota(jnp.int32, (GRID_W, GRID_W), 0)
    kc = lax.broadcasted_iota(jnp.int32, (GRID_W, GRID_W), 1)
    c0 = jnp.clip(c - NA_COLS // 2, 0, GRID_W - NA_COLS)
    valid = (kc >= c0) & (kc < c0 + NA_COLS)
    col_off = kc - c + (NA_COLS - 1)
    masked = jnp.full((GRID_W, GRID_W), NEG_INF, _F32)
    for row_off in range(n_row_off):
        blk = masked
        for d in range(n_col_off):
            blk = jnp.where(col_off == d, rb_ref[(h * n_row_off + row_off) * n_col_off + d] * LOG2E, blk)
        base_ref[row_off] = jnp.where(valid, blk, NEG_INF)
    for dw in range(NA_SUPER):
        ws = win_starts.get(dw, 0)
        for i in range(NA_SUPER):
            in_window = ws <= i < ws + NA_ROWS
            o_ref[dw, :, i * GRID_W:(i + 1) * GRID_W] = base_ref[i - dw + NA_ROWS - 1] if in_window else masked


def _na_bias_table(rel_bias, rows):
    heads = rel_bias.shape[0]
    return pl.pallas_call(
        functools.partial(_na_bias_kernel, win_starts=_na_window_starts(rows)),
        grid=(heads,),
        in_specs=[pl.BlockSpec(memory_space=pltpu.SMEM)],
        out_specs=pl.BlockSpec((None, NA_SUPER, GRID_W, NA_SUPER * GRID_W), lambda h: (h, 0, 0, 0)),
        out_shape=jax.ShapeDtypeStruct((heads, NA_SUPER, GRID_W, NA_SUPER * GRID_W), _F32),
        scratch_shapes=[pltpu.VMEM((2 * NA_ROWS - 1, GRID_W, GRID_W), _F32)],
        compiler_params=_cparams("parallel"),
        name="na_bias_table",
    )(rel_bias.astype(_F32).reshape(-1))


def _na_kernel(q_ref, k_ref, v_ref, z_ref, bias_ref, o_ref, *, rows):
    gq = NA_GROUP * GRID_W
    sw = NA_SUPER * GRID_W

    def row_group(g, carry):
        w0 = jnp.clip(NA_GROUP * g - NA_ROWS // 2, 0, rows - NA_SUPER)
        q0 = pl.multiple_of(g * gq, gq)
        k0 = pl.multiple_of(w0 * GRID_W, GRID_W)
        bias = bias_ref[pl.ds(NA_GROUP * g - w0, NA_GROUP)].reshape(gq, sw)
        s = _qk_t(q_ref[pl.ds(q0, gq), :], k_ref[pl.ds(k0, sw), :]) + bias
        m = jnp.max(s, axis=-1, keepdims=True)
        p = jnp.exp2(s - m)
        l = jnp.sum(p, axis=-1, keepdims=True)
        o = jnp.dot(p.astype(_BF16), v_ref[pl.ds(k0, sw), :], preferred_element_type=_F32) / l
        o_ref[pl.ds(q0, gq), :] = (o * z_ref[pl.ds(q0, gq), :]).astype(o_ref.dtype)
        return carry

    lax.fori_loop(0, rows // NA_GROUP, row_group, 0, unroll=NA_UNROLL)


def _neighbourhood(qkv, zf, bias_tab, batch, seq, off_q, off_z):
    rows = seq // GRID_W
    assert rows >= NA_SUPER and rows % NA_GROUP == 0
    blk = lambda off: pl.BlockSpec((seq, HD), lambda b, h: (b, off // HD + h))
    return pl.pallas_call(
        functools.partial(_na_kernel, rows=rows),
        grid=(batch, A_HEADS),
        in_specs=[blk(off_q["a_q"]), blk(off_q["a_k"]), blk(off_q["a_v"]), blk(off_z["a_z"]),
                  pl.BlockSpec((None, NA_SUPER, GRID_W, NA_SUPER * GRID_W), lambda b, h: (h, 0, 0, 0))],
        out_specs=pl.BlockSpec((seq, HD), lambda b, h: (b, h)),
        out_shape=jax.ShapeDtypeStruct((batch * seq, A_W), _BF16),
        compiler_params=_cparams("parallel", "parallel"),
        name="neighbourhood_attention",
    )(qkv, qkv, qkv, zf, bias_tab)


def _store_transposed(vt_ref, v_ref):
    seq, d = v_ref.shape
    for c in range(seq // KV_CHUNK):
        cs = slice(c * KV_CHUNK, (c + 1) * KV_CHUNK)
        vt_ref[:d, cs] = v_ref[cs, :].astype(_F32).T.astype(vt_ref.dtype)
    vt_ref[d:, :] = jnp.ones((vt_ref.shape[0] - d, seq), vt_ref.dtype)


class _OnlineSoftmax:
    def __init__(self, d_v, n_q):
        self.d_v = d_v
        self.m = jnp.full((1, n_q), NEG_INF, _F32)
        self.acc = jnp.zeros((d_v + ONES_ROWS, n_q), _F32)

    def update(self, s, vt_chunk):
        m_new = jnp.maximum(self.m, jnp.max(s, axis=0, keepdims=True))
        p = jnp.exp2(s - m_new).astype(_BF16)
        self.acc = (jnp.exp2(self.m - m_new) * self.acc
                    + jnp.dot(vt_chunk, p, preferred_element_type=_F32))
        self.m = m_new

    def result(self):
        return self.acc[:self.d_v] / self.acc[self.d_v:self.d_v + 1]


def _gqa_kernel(q_ref, k_ref, v_ref, z_ref, o_ref):
    n_heads = q_ref.shape[1] // HD
    group = n_heads // (k_ref.shape[1] // HD)
    head_slice = lambda h: slice(h * HD, (h + 1) * HD)
    scores = lambda h: _qk_t(q_ref[:, head_slice(h)], k_ref[:, head_slice(h // group)])
    s_next = scores(0)
    for h in range(n_heads):
        s = s_next
        if h + 1 < n_heads:
            s_next = scores(h + 1)
        m = jnp.max(s, axis=-1, keepdims=True)
        p = jnp.exp2(s - m)
        l = jnp.sum(p, axis=-1, keepdims=True)
        o = jnp.dot(p.astype(_BF16), v_ref[:, head_slice(h // group)], preferred_element_type=_F32) / l
        o_ref[:, head_slice(h)] = (o * z_ref[:, head_slice(h)]).astype(o_ref.dtype)


def _gqa(qkv, zf, batch, seq, off_q, off_z):
    nq = seq // TQ
    assert off_q["b_q"] % B_QW == 0 and off_z["b_z"] % B_QW == 0 and off_q["b_k"] % B_KVW == 0
    q_spec = lambda off: pl.BlockSpec((TQ, B_QW), lambda b, i: (b * nq + i, off // B_QW))
    kv_spec = lambda off: pl.BlockSpec((seq, B_KVW), lambda b, i: (b, off // B_KVW))
    return pl.pallas_call(
        _gqa_kernel,
        grid=(batch, nq),
        in_specs=[q_spec(off_q["b_q"]), kv_spec(off_q["b_k"]), kv_spec(off_q["b_v"]), q_spec(off_z["b_z"])],
        out_specs=pl.BlockSpec((TQ, B_QW), lambda b, i: (b * nq + i, 0)),
        out_shape=jax.ShapeDtypeStruct((batch * seq, B_QW), _BF16),
        compiler_params=_cparams("parallel", "parallel"),
        name="gqa_attention",
    )(qkv, qkv, qkv, zf)


def _diff_kernel(slope_ref, q_ref, k_ref, v_ref, z_ref, lam_ref, g_ref, o_ref, vt_ref, pen_ref, *, lam_init):
    h, i = pl.program_id(1), pl.program_id(2)
    seq = k_ref.shape[0]
    tq, hw = q_ref.shape
    key0 = seq - tq

    @pl.when(i == 0)
    def _():
        _store_transposed(vt_ref, v_ref)
        slope = slope_ref[h] * LOG2E
        n_rows = pen_ref.shape[0]
        for y0 in range(0, n_rows, KV_CHUNK):
            size = min(KV_CHUNK, n_rows - y0)
            row_minus_col = (lax.broadcasted_iota(jnp.int32, (size, tq), 0)
                             - lax.broadcasted_iota(jnp.int32, (size, tq), 1))
            pen_ref[y0:y0 + size, :] = slope * jnp.abs(row_minus_col + (y0 - key0)).astype(_F32)

    lp = lam_ref[...]
    lam = (jnp.exp(jnp.sum(lp[0:1] * lp[1:2], axis=-1, keepdims=True))
           - jnp.exp(jnp.sum(lp[2:3] * lp[3:4], axis=-1, keepdims=True)) + lam_init)

    states = [_OnlineSoftmax(hw, tq) for _ in range(2)]
    pen0 = pl.multiple_of(key0 - i * tq, tq)
    for c in range(seq // KV_CHUNK):
        cs = slice(c * KV_CHUNK, (c + 1) * KV_CHUNK)
        pen = pen_ref[pl.ds(pen0 + c * KV_CHUNK, KV_CHUNK), :]
        for mi, state in enumerate(states):
            sl = slice(mi * HD, (mi + 1) * HD)
            state.update(_qk_t(k_ref[cs, sl], q_ref[:, sl]) - pen, vt_ref[:, cs])
    o = (states[0].result() - lam * states[1].result()).T
    ms = jnp.mean(o * o, axis=-1, keepdims=True)
    o = o * lax.rsqrt(ms + RMS_EPS) * g_ref[...] * (1.0 - lam_init)
    o_ref[...] = (o * z_ref[...]).astype(o_ref.dtype)


def _diff_attention(qkv, zf, diff_lambda, subln_g, lam_init, batch, seq, off_q, off_z):
    hw = 2 * HD
    nq = seq // TQ
    slopes = jnp.asarray(2.0 ** (-8.0 * np.arange(1, C_HEADS + 1) / C_HEADS), dtype=_F32)
    q_spec = lambda off: pl.BlockSpec((TQ, hw), lambda b, h, i, sl: (b * nq + i, off // hw + h))
    kv_spec = lambda off: pl.BlockSpec((seq, hw), lambda b, h, i, sl: (b, off // hw + h))
    return pl.pallas_call(
        functools.partial(_diff_kernel, lam_init=lam_init),
        grid_spec=pltpu.PrefetchScalarGridSpec(
            num_scalar_prefetch=1,
            grid=(batch, C_HEADS, nq),
            in_specs=[q_spec(off_q["c_q"]), kv_spec(off_q["c_k"]), kv_spec(off_q["c_v"]), q_spec(off_z["c_z"]),
                      pl.BlockSpec((4, HD), lambda b, h, i, sl: (0, 0)),
                      pl.BlockSpec((1, hw), lambda b, h, i, sl: (0, 0))],
            out_specs=pl.BlockSpec((TQ, hw), lambda b, h, i, sl: (b * nq + i, h)),
            scratch_shapes=[pltpu.VMEM((hw + ONES_ROWS, seq), _BF16),
                            pltpu.VMEM((2 * seq - TQ, TQ), _F32)],
        ),
        out_shape=jax.ShapeDtypeStruct((batch * seq, C_VW), _BF16),
        compiler_params=_cparams("parallel", "parallel", "arbitrary"),
        name="diff_attention",
    )(slopes, qkv, qkv, qkv, zf, diff_lambda.astype(_F32), subln_g.astype(_F32).reshape(1, hw))


def _dilated_group(gi, window, dil, slope, q_ref, k_ref, v_ref, og_ref, lse_ref, bias_ref):
    seq = q_ref.shape[0]
    length = seq // dil
    tq = min(D_TQ, length)
    span = min(D_SPAN, length)
    n_tiles = length // tq
    n_side = window // (2 * dil)
    half = (span - tq) // 2
    unit_pen = slope * (float(dil) * LOG2E)
    rel0 = (lax.broadcasted_iota(jnp.int32, (tq, span), 1)
            - lax.broadcasted_iota(jnp.int32, (tq, span), 0))
    for vi in range(3 if n_tiles > 1 else 1):
        rel = jnp.abs(rel0 - vi * half)
        bias_ref[vi, :tq, :span] = jnp.where(rel <= n_side, -unit_pen * rel.astype(_F32), NEG_INF)

    def rows(start, size):
        return pl.ds(start, size) if dil == 1 else pl.ds(start, size, stride=dil)

    def tile(idx, carry):
        r = lax.div(idx, n_tiles)
        u0 = lax.rem(idx, n_tiles) * tq
        ks = jnp.clip(u0 - half, 0, length - span)
        vi = lax.div(u0 - ks, half) if n_tiles > 1 else 0
        q_rows = rows(r + dil * u0, tq)
        k_rows = rows(r + dil * ks, span)
        s = _qk_t(q_ref[q_rows, :].astype(_BF16), k_ref[k_rows, :].astype(_BF16))
        s = s + bias_ref[vi, :tq, :span]
        m = jnp.max(s, axis=-1, keepdims=True)
        p = jnp.exp2(s - m)
        l = jnp.sum(p, axis=-1, keepdims=True)
        o = jnp.dot(p.astype(_BF16), v_ref[k_rows, :].astype(_BF16), preferred_element_type=_F32) / l
        og_ref[gi, q_rows, :] = o
        lse_ref[gi, q_rows, :] = jnp.broadcast_to(m + jnp.log2(l), (tq, LANES))
        return carry

    lax.fori_loop(0, dil * n_tiles, tile, 0, unroll=D_UNROLL)


def _dilated_kernel(slope_ref, q_ref, k_ref, v_ref, z_ref, o_ref, og_ref, lse_ref, bias_ref):
    h, g = pl.program_id(1), pl.program_id(2)
    for gi, (window, dil) in enumerate(D_PATTERNS):
        @pl.when(g == gi)
        def _(gi=gi, window=window, dil=dil):
            _dilated_group(gi, window, dil, slope_ref[gi * D_HEADS_PER_GROUP + h],
                           q_ref, k_ref, v_ref, og_ref, lse_ref, bias_ref)

    @pl.when(g == D_GROUPS - 1)
    def _():
        seq = o_ref.shape[0]
        chunk = min(512, seq)

        def merge(ci, carry):
            rs = pl.ds(pl.multiple_of(ci * chunk, chunk), chunk)
            lses = [lse_ref[gi, rs, :] for gi in range(D_GROUPS)]
            top = functools.reduce(jnp.maximum, lses)
            ws = [jnp.exp2(v - top) for v in lses]
            num = functools.reduce(jnp.add, [w * og_ref[gi, rs, :] for gi, w in enumerate(ws)])
            out = num / functools.reduce(jnp.add, ws)
            o_ref[rs, :] = (out * z_ref[rs, :]).astype(o_ref.dtype)
            return carry

        lax.fori_loop(0, seq // chunk, merge, 0)


def _dilated(zf, batch, seq, off_z):
    n = D_GROUPS * D_HEADS_PER_GROUP
    slopes = jnp.asarray(2.0 ** (-8.0 * np.arange(1, n + 1) / n), dtype=_F32)
    for window, dil in D_PATTERNS:
        assert seq % dil == 0 and (seq // dil) % min(D_TQ, seq // dil) == 0
        assert D_SPAN - D_TQ >= 2 * (window // (2 * dil)) or seq // dil <= D_SPAN
    qkv_spec = lambda off: pl.BlockSpec(
        (seq, HD), lambda b, h, g, sl: (b, off // HD + g * D_HEADS_PER_GROUP + h))
    return pl.pallas_call(
        _dilated_kernel,
        grid_spec=pltpu.PrefetchScalarGridSpec(
            num_scalar_prefetch=1,
            grid=(batch, D_HEADS_PER_GROUP, D_GROUPS),
            in_specs=[qkv_spec(off_z["d_q"]), qkv_spec(off_z["d_k"]), qkv_spec(off_z["d_v"]),
                      pl.BlockSpec((seq, HD), lambda b, h, g, sl: (b, off_z["d_z"] // HD + h))],
            out_specs=pl.BlockSpec((seq, HD), lambda b, h, g, sl: (b, h)),
            scratch_shapes=[pltpu.VMEM((D_GROUPS, seq, HD), _F32),
                            pltpu.VMEM((D_GROUPS, seq, LANES), _F32),
                            pltpu.VMEM((3, D_TQ, D_SPAN), _F32)],
        ),
        out_shape=jax.ShapeDtypeStruct((batch * seq, D_OW), _BF16),
        compiler_params=_cparams("parallel", "parallel", "arbitrary"),
        name="dilated_attention",
    )(slopes, zf, zf, zf, zf)


def _merge_kernel(x_ref, wg_ref, *refs, n_br):
    y_refs, wb_refs = refs[:n_br], refs[n_br:2 * n_br]
    o_ref, acc_ref = refs[2 * n_br:]
    br = pl.program_id(2)
    tm = x_ref.shape[0]

    @pl.when(br == 0)
    def _():
        acc_ref[...] = jnp.zeros_like(acc_ref)

    def add_branch(y_ref, wb_ref):
        for c in range(tm // ROW_CHUNK):
            rs = slice(c * ROW_CHUNK, (c + 1) * ROW_CHUNK)
            gate = _sigmoid(jnp.dot(x_ref[rs, :], wg_ref[...], preferred_element_type=_F32))
            acc_ref[rs, :] += gate * jnp.dot(y_ref[rs, :], wb_ref[...], preferred_element_type=_F32)

    for b in range(n_br):
        pl.when(br == b)(functools.partial(add_branch, y_refs[b], wb_refs[b]))

    @pl.when(br == n_br - 1)
    def _():
        o_ref[...] = acc_ref[...].astype(o_ref.dtype)


def _gated_merge(xn, w_b, layer, ys, wbs, gate_start):
    m, d = xn.shape
    n_br = len(ys)
    assert gate_start % TN == 0 and d % TN == 0
    gate_tile0 = gate_start // TN
    tiles_per_branch = d // TN
    y_specs = [pl.BlockSpec((TM, y.shape[1]), lambda i, j, br: (i, 0)) for y in ys]
    wb_specs = [pl.BlockSpec((None, w.shape[1], TN), lambda i, j, br: (layer, 0, j)) for w in wbs]
    return pl.pallas_call(
        functools.partial(_merge_kernel, n_br=n_br),
        grid=(m // TM, d // TN, n_br),
        in_specs=[
            pl.BlockSpec((TM, d), lambda i, j, br: (i, 0)),
            pl.BlockSpec((None, d, TN), lambda i, j, br: (layer, 0, gate_tile0 + br * tiles_per_branch + j)),
            *y_specs, *wb_specs,
        ],
        out_specs=pl.BlockSpec((TM, TN), lambda i, j, br: (i, j)),
        out_shape=jax.ShapeDtypeStruct((m, d), _BF16),
        scratch_shapes=[pltpu.VMEM((TM, TN), _F32)],
        compiler_params=_cparams("parallel", "parallel", "arbitrary"),
        name="gated_merge",
    )(xn, w_b, *ys, *wbs)


def _out_kernel(m_ref, w32_ref, x_ref, o_ref, w_ref):
    @pl.when(pl.program_id(1) == 0)
    def _():
        w_ref[...] = w32_ref[...].astype(w_ref.dtype)

    for c in range(o_ref.shape[0] // ROW_CHUNK):
        rs = slice(c * ROW_CHUNK, (c + 1) * ROW_CHUNK)
        o_ref[rs, :] = x_ref[rs, :] + jnp.dot(m_ref[rs, :], w_ref[...], preferred_element_type=_F32)


def _out_projection(merged, w_out, layer, x):
    m, d = x.shape
    return pl.pallas_call(
        _out_kernel,
        grid=(d // TN, m // TM),
        in_specs=[pl.BlockSpec((TM, d), lambda j, i: (i, 0)),
                  pl.BlockSpec((None, d, TN), lambda j, i: (layer, 0, j)),
                  pl.BlockSpec((TM, TN), lambda j, i: (i, j))],
        out_specs=pl.BlockSpec((TM, TN), lambda j, i: (i, j)),
        out_shape=jax.ShapeDtypeStruct((m, d), _F32),
        scratch_shapes=[pltpu.VMEM((d, TN), _BF16)],
        compiler_params=_cparams("parallel", "arbitrary"),
        name="out_projection",
    )(merged, w_out, x)


def kernel(x, norm_g, w_in, qk_gain, na_rel_bias, diff_lambda, diff_subln_g,
           w_branch_a, w_branch_b, w_branch_c, w_branch_d, w_out):
    batch, seq, d_model = x.shape
    depth = w_in.shape[0]
    m = batch * seq
    assert w_in.shape[2] == GATE_START + N_BRANCHES * d_model
    assert m % TM == 0 and seq % TM == 0 and seq % TQ == 0 and seq % GRID_W == 0 and d_model % TN == 0

    w_gate_b = w_in[:, :, GATE_START:].astype(_BF16)
    wbs = tuple(w.astype(_BF16) for w in (w_branch_a, w_branch_b, w_branch_c, w_branch_d))
    cos_t, sin_t = _rope_tables(seq)
    src_b, modes_b, gkeys_b, off_q = _PLAN_BF16
    src_f, modes_f, gkeys_f, off_z = _PLAN_F32

    xf = x.reshape(m, d_model)
    for l in range(depth):
        xn = _rmsnorm(xf, norm_g[l])
        qkv = _projection(xn, w_in, l, jnp.asarray(src_b), jnp.asarray(modes_b),
                          _gain_table(qk_gain[l], gkeys_b), cos_t, sin_t, _BF16, seq,
                          tuple(sorted(set(modes_b.tolist()))))
        zf = _projection(xn, w_in, l, jnp.asarray(src_f), jnp.asarray(modes_f),
                         _gain_table(qk_gain[l], gkeys_f), cos_t, sin_t, _F32, seq,
                         tuple(sorted(set(modes_f.tolist()))))
        lam_init = 0.8 - 0.6 * math.exp(-0.3 * l)
        y_a = _neighbourhood(qkv, zf, _na_bias_table(na_rel_bias[l], seq // GRID_W), batch, seq, off_q, off_z)
        y_b = _gqa(qkv, zf, batch, seq, off_q, off_z)
        y_c = _diff_attention(qkv, zf, diff_lambda[l], diff_subln_g[l], lam_init, batch, seq, off_q, off_z)
        y_d = _dilated(zf, batch, seq, off_z)
        merged = _gated_merge(xn, w_gate_b, l, (y_a, y_b, y_c, y_d), wbs, 0)
        xf = _out_projection(merged, w_out, l, xf)
    return xf.reshape(batch, seq, d_model)
```

```python
import functools
import math

import jax
import jax.numpy as jnp
import numpy as np
from jax import lax
from jax.experimental import pallas as pl
from jax.experimental.pallas import tpu as pltpu

HD = 128
GRID_W = 64
NA_ROWS, NA_COLS = 8, 16
A_HEADS = 8
B_Q_HEADS, B_KV_HEADS = 8, 2
ROPE_THETA = 10000.0
C_HEADS = 4
D_PATTERNS = ((128, 1), (512, 4), (2048, 16))
D_GROUPS, D_HEADS_PER_GROUP = 3, 4
N_BRANCHES = 4
RMS_EPS = 1e-6
NEG_INF = -1e30
LOG2E = math.log2(math.e)

A_W = A_HEADS * HD
B_QW = B_Q_HEADS * HD
B_KVW = B_KV_HEADS * HD
C_QKW = C_HEADS * 2 * HD
C_VW = C_HEADS * 2 * HD
D_QKVW = D_GROUPS * D_HEADS_PER_GROUP * HD
D_OW = D_HEADS_PER_GROUP * HD
_SEG_NAMES = ("a_q", "a_k", "a_v", "a_z", "b_q", "b_k", "b_v", "b_z",
              "c_q", "c_k", "c_v", "c_z", "d_q", "d_k", "d_v", "d_z")
_SEG_WIDTHS = (A_W, A_W, A_W, A_W, B_QW, B_KVW, B_KVW, B_QW,
               C_QKW, C_QKW, C_VW, C_VW, D_QKVW, D_QKVW, D_QKVW, D_OW)
_SEG_START = dict(zip(_SEG_NAMES, np.cumsum((0,) + _SEG_WIDTHS[:-1]).tolist()))
GATE_START = int(sum(_SEG_WIDTHS))

LANES = 128
V7X_VMEM_BYTES = 64 * 1024 * 1024
VMEM_LIMIT = 56 * 1024 * 1024
TM = 1024
TN = 512
ROW_CHUNK = 256
TM_NORM = 256
TQ = 256
KV_CHUNK = 512
ONES_ROWS = 16
D_TQ, D_SPAN = 256, 512
NA_GROUP, NA_SUPER = 4, 12
NA_BATCH, NA_UNROLL = 4, 1
D_GROUP, D_UNROLL = 4, 2

MODE_PLAIN, MODE_NORM, MODE_NORM_ROPE, MODE_KV_B, MODE_SILU = range(5)

_F32 = jnp.float32
_BF16 = jnp.bfloat16


def _cparams(*sem):
    return pltpu.CompilerParams(dimension_semantics=sem, vmem_limit_bytes=VMEM_LIMIT)


def _sigmoid(v):
    return 0.5 * jnp.tanh(0.5 * v) + 0.5


def _qk_t(q, k):
    return lax.dot_general(q, k, (((1,), (1,)), ((), ())), preferred_element_type=_F32)


def _rmsnorm_kernel(x_ref, g_ref, o_ref):
    x = x_ref[...]
    ms = jnp.mean(x * x, axis=-1, keepdims=True)
    o_ref[...] = (x * lax.rsqrt(ms + RMS_EPS) * g_ref[...]).astype(o_ref.dtype)


def _rmsnorm(x, g):
    m, d = x.shape
    return pl.pallas_call(
        _rmsnorm_kernel,
        grid=(m // TM_NORM,),
        in_specs=[pl.BlockSpec((TM_NORM, d), lambda i: (i, 0)),
                  pl.BlockSpec((1, d), lambda i: (0, 0))],
        out_specs=pl.BlockSpec((TM_NORM, d), lambda i: (i, 0)),
        out_shape=jax.ShapeDtypeStruct((m, d), _BF16),
        compiler_params=_cparams("parallel"),
        name="rmsnorm",
    )(x, g.reshape(1, d))


def _proj_kernel(src_ref, mode_ref, x_ref, w32_ref, g_ref, cos_ref, sin_ref, o_ref, w_ref, *, modes_used):
    del src_ref
    mode = mode_ref[pl.program_id(0)]
    tm, tn = o_ref.shape
    n_heads = tn // HD

    @pl.when(pl.program_id(1) == 0)
    def _():
        w_ref[...] = w32_ref[...].astype(w_ref.dtype)

    def norm_head(acc, h):
        blk = acc[:, h * HD:(h + 1) * HD]
        ms = jnp.mean(blk * blk, axis=-1, keepdims=True)
        return blk * lax.rsqrt(ms + RMS_EPS) * g_ref[:, h * HD:(h + 1) * HD]

    def rope(y, rs):
        even = (lax.broadcasted_iota(jnp.int32, y.shape, 1) % 2) == 0
        partner = jnp.where(even, pltpu.roll(y, HD - 1, 1), pltpu.roll(y, 1, 1))
        return y * cos_ref[rs, :] + partner * sin_ref[rs, :]

    def head_epilogue(mode_id, acc, h, rs):
        is_key_half = h < n_heads // 2
        if mode_id == MODE_NORM:
            return norm_head(acc, h)
        if mode_id == MODE_NORM_ROPE or (mode_id == MODE_KV_B and is_key_half):
            return rope(norm_head(acc, h), rs)
        return acc[:, h * HD:(h + 1) * HD]

    def run(mode_id):
        for c in range(tm // ROW_CHUNK):
            rs = slice(c * ROW_CHUNK, (c + 1) * ROW_CHUNK)
            acc = jnp.dot(x_ref[rs, :], w_ref[...], preferred_element_type=_F32)
            if mode_id == MODE_PLAIN:
                o_ref[rs, :] = acc.astype(o_ref.dtype)
            elif mode_id == MODE_SILU:
                o_ref[rs, :] = (acc * _sigmoid(acc)).astype(o_ref.dtype)
            else:
                for h in range(n_heads):
                    o_ref[rs, h * HD:(h + 1) * HD] = head_epilogue(mode_id, acc, h, rs).astype(o_ref.dtype)

    for mode_id in modes_used:
        pl.when(mode == mode_id)(functools.partial(run, mode_id))


def _projection(xn, w_in, layer, src_tiles, modes, gains, cos_t, sin_t, out_dtype, seq, modes_used):
    m, d = xn.shape
    n_t = src_tiles.shape[0]
    rope_blocks = seq // TM
    return pl.pallas_call(
        functools.partial(_proj_kernel, modes_used=modes_used),
        grid_spec=pltpu.PrefetchScalarGridSpec(
            num_scalar_prefetch=2,
            grid=(n_t, m // TM),
            in_specs=[
                pl.BlockSpec((TM, d), lambda j, i, src, md: (i, 0)),
                pl.BlockSpec((None, d, TN), lambda j, i, src, md: (layer, 0, src[j])),
                pl.BlockSpec((None, 1, TN), lambda j, i, src, md: (j, 0, 0)),
                pl.BlockSpec((TM, HD), lambda j, i, src, md: (i % rope_blocks, 0)),
                pl.BlockSpec((TM, HD), lambda j, i, src, md: (i % rope_blocks, 0)),
            ],
            out_specs=pl.BlockSpec((TM, TN), lambda j, i, src, md: (i, j)),
            scratch_shapes=[pltpu.VMEM((d, TN), _BF16)],
        ),
        out_shape=jax.ShapeDtypeStruct((m, n_t * TN), out_dtype),
        compiler_params=_cparams("parallel", "arbitrary"),
        name="projection",
    )(src_tiles, modes, xn, w_in, gains, cos_t, sin_t)


def _tile_plan(segments):
    src, modes, gain_keys, out_start = [], [], [], {}
    col = 0
    for name, mode, gain_key in segments:
        if name == "b_kv":
            start, width = _SEG_START["b_k"], 2 * B_KVW
            assert width == TN and _SEG_START["b_v"] == start + B_KVW
            out_start["b_k"], out_start["b_v"] = col, col + B_KVW
        else:
            start, width = _SEG_START[name], _SEG_WIDTHS[_SEG_NAMES.index(name)]
            out_start[name] = col
        assert start % TN == 0 and width % TN == 0, (name, start, width)
        for t in range(width // TN):
            src.append(start // TN + t)
            modes.append(mode)
            gain_keys.append(gain_key)
        col += width
    return np.asarray(src, np.int32), np.asarray(modes, np.int32), gain_keys, out_start


_Q_GAIN = HD ** -0.5 * LOG2E
_PLAN_BF16 = _tile_plan((
    ("a_q", MODE_NORM, (0, 0)), ("a_k", MODE_NORM, (0, 1)), ("a_v", MODE_PLAIN, None),
    ("b_q", MODE_NORM_ROPE, (1, 0)), ("b_kv", MODE_KV_B, (1, 1)),
    ("c_q", MODE_NORM, (2, 0)), ("c_k", MODE_NORM, (2, 1)), ("c_v", MODE_PLAIN, None)))
_PLAN_F32 = _tile_plan((
    ("a_z", MODE_SILU, None), ("b_z", MODE_SILU, None), ("c_z", MODE_SILU, None),
    ("d_q", MODE_NORM, (3, 0)), ("d_k", MODE_NORM, (3, 1)), ("d_v", MODE_PLAIN, None),
    ("d_z", MODE_SILU, None)))


def _gain_table(qk_gain, gain_keys):
    rows = []
    for key in gain_keys:
        if key is None:
            rows.append(jnp.ones((TN,), _F32))
        else:
            g = qk_gain[key[0], key[1]].astype(_F32)
            if key[1] == 0:
                g = g * _Q_GAIN
            rows.append(jnp.tile(g, TN // HD))
    return jnp.stack(rows)[:, None, :]


def _rope_tables(seq):
    t = jnp.arange(seq)
    row = (t // GRID_W).astype(_F32)
    col = (t % GRID_W).astype(_F32)
    n_pairs = HD // 4
    inv_freq = ROPE_THETA ** (-jnp.arange(n_pairs, dtype=_F32) / n_pairs)
    ang = jnp.concatenate([row[:, None] * inv_freq, col[:, None] * inv_freq], axis=-1)
    cos, sin = jnp.cos(ang), jnp.sin(ang)
    cos_t = jnp.repeat(cos, 2, axis=-1)
    sin_t = jnp.stack([-sin, sin], axis=-1).reshape(seq, HD)
    return cos_t, sin_t


def _na_window_starts(rows):
    starts = {}
    for r in range(rows):
        w0 = int(np.clip(NA_GROUP * (r // NA_GROUP) - NA_ROWS // 2, 0, rows - NA_SUPER))
        r0 = int(np.clip(r - NA_ROWS // 2, 0, rows - NA_ROWS))
        assert 0 <= r0 - w0 and r0 - w0 + NA_ROWS <= NA_SUPER and 0 <= r - w0 < NA_SUPER
        assert starts.setdefault(r - w0, r0 - w0) == r0 - w0
    return starts


def _na_bias_kernel(rb_ref, o_ref, base_ref, *, win_starts):
    h = pl.program_id(0)
    n_row_off, n_col_off = 2 * NA_ROWS - 1, 2 * NA_COLS - 1
    c = lax.broadcasted_iota(jnp.int32, (GRID_W, GRID_W), 0)
    kc = lax.broadcasted_iota(jnp.int32, (GRID_W, GRID_W), 1)
    c0 = jnp.clip(c - NA_COLS // 2, 0, GRID_W - NA_COLS)
    valid = (kc >= c0) & (kc < c0 + NA_COLS)
    col_off = kc - c + (NA_COLS - 1)
    masked = jnp.full((GRID_W, GRID_W), NEG_INF, _F32)
    for row_off in range(n_row_off):
        blk = masked
        for d in range(n_col_off):
            blk = jnp.where(col_off == d, rb_ref[(h * n_row_off + row_off) * n_col_off + d] * LOG2E, blk)
        base_ref[row_off] = jnp.where(valid, blk, NEG_INF)
    for dw in range(NA_SUPER):
        ws = win_starts.get(dw, 0)
        for i in range(NA_SUPER):
            in_window = ws <= i < ws + NA_ROWS
            o_ref[dw, :, i * GRID_W:(i + 1) * GRID_W] = base_ref[i - dw + NA_ROWS - 1] if in_window else masked


def _na_bias_table(rel_bias, rows):
    heads = rel_bias.shape[0]
    return pl.pallas_call(
        functools.partial(_na_bias_kernel, win_starts=_na_window_starts(rows)),
        grid=(heads,),
        in_specs=[pl.BlockSpec(memory_space=pltpu.SMEM)],
        out_specs=pl.BlockSpec((None, NA_SUPER, GRID_W, NA_SUPER * GRID_W), lambda h: (h, 0, 0, 0)),
        out_shape=jax.ShapeDtypeStruct((heads, NA_SUPER, GRID_W, NA_SUPER * GRID_W), _F32),
        scratch_shapes=[pltpu.VMEM((2 * NA_ROWS - 1, GRID_W, GRID_W), _F32)],
        compiler_params=_cparams("parallel"),
        name="na_bias_table",
    )(rel_bias.astype(_F32).reshape(-1))


def _na_kernel(q_ref, k_ref, v_ref, z_ref, bias_ref, o_ref, *, rows):
    gq = NA_GROUP * GRID_W
    sw = NA_SUPER * GRID_W

    def row_groups(t, carry):
        addr = []
        for k in range(NA_BATCH):
            g = t * NA_BATCH + k
            w0 = jnp.clip(NA_GROUP * g - NA_ROWS // 2, 0, rows - NA_SUPER)
            addr.append((pl.ds(pl.multiple_of(g * gq, gq), gq), pl.ds(pl.multiple_of(w0 * GRID_W, GRID_W), sw),
                         NA_GROUP * g - w0))
        scores = [_qk_t(q_ref[qs, :], k_ref[ks, :]) + bias_ref[pl.ds(dw, NA_GROUP)].reshape(gq, sw)
                  for qs, ks, dw in addr]
        stats = []
        for s in scores:
            p = jnp.exp2(s - jnp.max(s, axis=-1, keepdims=True))
            stats.append((jnp.sum(p, axis=-1, keepdims=True), p.astype(_BF16)))
        for (qs, ks, _), (l, p) in zip(addr, stats):
            o = jnp.dot(p, v_ref[ks, :], preferred_element_type=_F32) / l
            o_ref[qs, :] = (o * z_ref[qs, :]).astype(o_ref.dtype)
        return carry

    assert (rows // NA_GROUP) % NA_BATCH == 0
    lax.fori_loop(0, rows // NA_GROUP // NA_BATCH, row_groups, 0, unroll=NA_UNROLL)


def _neighbourhood(qkv, zf, bias_tab, batch, seq, off_q, off_z):
    rows = seq // GRID_W
    assert rows >= NA_SUPER and rows % NA_GROUP == 0
    blk = lambda off: pl.BlockSpec((seq, HD), lambda b, h: (b, off // HD + h))
    return pl.pallas_call(
        functools.partial(_na_kernel, rows=rows),
        grid=(batch, A_HEADS),
        in_specs=[blk(off_q["a_q"]), blk(off_q["a_k"]), blk(off_q["a_v"]), blk(off_z["a_z"]),
                  pl.BlockSpec((None, NA_SUPER, GRID_W, NA_SUPER * GRID_W), lambda b, h: (h, 0, 0, 0))],
        out_specs=pl.BlockSpec((seq, HD), lambda b, h: (b, h)),
        out_shape=jax.ShapeDtypeStruct((batch * seq, A_W), _BF16),
        compiler_params=_cparams("parallel", "parallel"),
        name="neighbourhood_attention",
    )(qkv, qkv, qkv, zf, bias_tab)


def _gqa_kernel(q_ref, k_ref, v_ref, z_ref, o_ref):
    n_heads = q_ref.shape[1] // HD
    group = n_heads // (k_ref.shape[1] // HD)
    head_slice = lambda h: slice(h * HD, (h + 1) * HD)
    scores = lambda h: _qk_t(q_ref[:, head_slice(h)], k_ref[:, head_slice(h // group)])
    s_next = scores(0)
    for h in range(n_heads):
        s = s_next
        if h + 1 < n_heads:
            s_next = scores(h + 1)
        m = jnp.max(s, axis=-1, keepdims=True)
        p = jnp.exp2(s - m)
        l = jnp.sum(p, axis=-1, keepdims=True)
        o = jnp.dot(p.astype(_BF16), v_ref[:, head_slice(h // group)], preferred_element_type=_F32) / l
        o_ref[:, head_slice(h)] = (o * z_ref[:, head_slice(h)]).astype(o_ref.dtype)


def _gqa(qkv, zf, batch, seq, off_q, off_z):
    nq = seq // TQ
    assert off_q["b_q"] % B_QW == 0 and off_z["b_z"] % B_QW == 0 and off_q["b_k"] % B_KVW == 0
    q_spec = lambda off: pl.BlockSpec((TQ, B_QW), lambda b, i: (b * nq + i, off // B_QW))
    kv_spec = lambda off: pl.BlockSpec((seq, B_KVW), lambda b, i: (b, off // B_KVW))
    return pl.pallas_call(
        _gqa_kernel,
        grid=(batch, nq),
        in_specs=[q_spec(off_q["b_q"]), kv_spec(off_q["b_k"]), kv_spec(off_q["b_v"]), q_spec(off_z["b_z"])],
        out_specs=pl.BlockSpec((TQ, B_QW), lambda b, i: (b * nq + i, 0)),
        out_shape=jax.ShapeDtypeStruct((batch * seq, B_QW), _BF16),
        compiler_params=_cparams("parallel", "parallel"),
        name="gqa_attention",
    )(qkv, qkv, qkv, zf)


def _store_transposed(vt_ref, v_ref):
    seq, d = v_ref.shape
    for c in range(seq // KV_CHUNK):
        cs = slice(c * KV_CHUNK, (c + 1) * KV_CHUNK)
        vt_ref[:d, cs] = v_ref[cs, :].astype(_F32).T.astype(vt_ref.dtype)
    vt_ref[d:, :] = jnp.ones((vt_ref.shape[0] - d, seq), vt_ref.dtype)


class _OnlineSoftmax:
    def __init__(self, d_v, n_q):
        self.d_v = d_v
        self.m = jnp.full((1, n_q), NEG_INF, _F32)
        self.acc = jnp.zeros((d_v + ONES_ROWS, n_q), _F32)

    def update(self, s, vt_chunk):
        m_new = jnp.maximum(self.m, jnp.max(s, axis=0, keepdims=True))
        p = jnp.exp2(s - m_new).astype(_BF16)
        self.acc = (jnp.exp2(self.m - m_new) * self.acc
                    + jnp.dot(vt_chunk, p, preferred_element_type=_F32))
        self.m = m_new

    def result(self):
        return self.acc[:self.d_v] / self.acc[self.d_v:self.d_v + 1]


def _diff_kernel(slope_ref, q_ref, k_ref, v_ref, z_ref, lam_ref, g_ref, o_ref, vt_ref, pen_ref, *, lam_init):
    h, i = pl.program_id(1), pl.program_id(2)
    seq = k_ref.shape[0]
    tq, hw = q_ref.shape
    key0 = seq - tq

    @pl.when(i == 0)
    def _():
        _store_transposed(vt_ref, v_ref)
        slope = slope_ref[h] * LOG2E
        n_rows = pen_ref.shape[0]
        for y0 in range(0, n_rows, KV_CHUNK):
            size = min(KV_CHUNK, n_rows - y0)
            row_minus_col = (lax.broadcasted_iota(jnp.int32, (size, tq), 0)
                             - lax.broadcasted_iota(jnp.int32, (size, tq), 1))
            pen_ref[y0:y0 + size, :] = slope * jnp.abs(row_minus_col + (y0 - key0)).astype(_F32)

    lp = lam_ref[...]
    lam = (jnp.exp(jnp.sum(lp[0:1] * lp[1:2], axis=-1, keepdims=True))
           - jnp.exp(jnp.sum(lp[2:3] * lp[3:4], axis=-1, keepdims=True)) + lam_init)

    states = [_OnlineSoftmax(hw, tq) for _ in range(2)]
    pen0 = pl.multiple_of(key0 - i * tq, tq)
    for c in range(seq // KV_CHUNK):
        cs = slice(c * KV_CHUNK, (c + 1) * KV_CHUNK)
        pen = pen_ref[pl.ds(pen0 + c * KV_CHUNK, KV_CHUNK), :]
        for mi, state in enumerate(states):
            sl = slice(mi * HD, (mi + 1) * HD)
            state.update(_qk_t(k_ref[cs, sl], q_ref[:, sl]) - pen, vt_ref[:, cs])
    o = (states[0].result() - lam * states[1].result()).T
    ms = jnp.mean(o * o, axis=-1, keepdims=True)
    o = o * lax.rsqrt(ms + RMS_EPS) * g_ref[...] * (1.0 - lam_init)
    o_ref[...] = (o * z_ref[...]).astype(o_ref.dtype)


def _diff_attention(qkv, zf, diff_lambda, subln_g, lam_init, batch, seq, off_q, off_z):
    hw = 2 * HD
    nq = seq // TQ
    slopes = jnp.asarray(2.0 ** (-8.0 * np.arange(1, C_HEADS + 1) / C_HEADS), dtype=_F32)
    q_spec = lambda off: pl.BlockSpec((TQ, hw), lambda b, h, i, sl: (b * nq + i, off // hw + h))
    kv_spec = lambda off: pl.BlockSpec((seq, hw), lambda b, h, i, sl: (b, off // hw + h))
    return pl.pallas_call(
        functools.partial(_diff_kernel, lam_init=lam_init),
        grid_spec=pltpu.PrefetchScalarGridSpec(
            num_scalar_prefetch=1,
            grid=(batch, C_HEADS, nq),
            in_specs=[q_spec(off_q["c_q"]), kv_spec(off_q["c_k"]), kv_spec(off_q["c_v"]), q_spec(off_z["c_z"]),
                      pl.BlockSpec((4, HD), lambda b, h, i, sl: (0, 0)),
                      pl.BlockSpec((1, hw), lambda b, h, i, sl: (0, 0))],
            out_specs=pl.BlockSpec((TQ, hw), lambda b, h, i, sl: (b * nq + i, h)),
            scratch_shapes=[pltpu.VMEM((hw + ONES_ROWS, seq), _BF16),
                            pltpu.VMEM((2 * seq - TQ, TQ), _F32)],
        ),
        out_shape=jax.ShapeDtypeStruct((batch * seq, C_VW), _BF16),
        compiler_params=_cparams("parallel", "parallel", "arbitrary"),
        name="diff_attention",
    )(slopes, qkv, qkv, qkv, zf, diff_lambda.astype(_F32), subln_g.astype(_F32).reshape(1, hw))


def _dilated_group(gi, window, dil, slope, q_ref, k_ref, v_ref, og_ref, lse_ref, bias_ref):
    seq = q_ref.shape[0]
    length = seq // dil
    tq = min(D_TQ, length)
    span = min(D_SPAN, length)
    n_tiles = length // tq
    n_side = window // (2 * dil)
    half = (span - tq) // 2
    unit_pen = slope * (float(dil) * LOG2E)
    rel0 = (lax.broadcasted_iota(jnp.int32, (tq, span), 1)
            - lax.broadcasted_iota(jnp.int32, (tq, span), 0))
    for vi in range(3 if n_tiles > 1 else 1):
        rel = jnp.abs(rel0 - vi * half)
        bias_ref[vi, :tq, :span] = jnp.where(rel <= n_side, -unit_pen * rel.astype(_F32), NEG_INF)

    def rows(start, size):
        return pl.ds(start, size) if dil == 1 else pl.ds(start, size, stride=dil)

    def tile_group(gidx, carry):
        tiles = []
        for k in range(D_GROUP):
            idx = gidx * D_GROUP + k
            r = lax.div(idx, n_tiles)
            u0 = lax.rem(idx, n_tiles) * tq
            ks = jnp.clip(u0 - half, 0, length - span)
            vi = lax.div(u0 - ks, half) if n_tiles > 1 else 0
            tiles.append((rows(r + dil * u0, tq), rows(r + dil * ks, span), vi))
        scores = [_qk_t(q_ref[q_rows, :].astype(_BF16), k_ref[k_rows, :].astype(_BF16)) + bias_ref[vi, :tq, :span]
                  for q_rows, k_rows, vi in tiles]
        stats = []
        for s in scores:
            m = jnp.max(s, axis=-1, keepdims=True)
            p = jnp.exp2(s - m)
            stats.append((m, jnp.sum(p, axis=-1, keepdims=True), p.astype(_BF16)))
        for (q_rows, k_rows, _), (m, l, p) in zip(tiles, stats):
            o = jnp.dot(p, v_ref[k_rows, :].astype(_BF16), preferred_element_type=_F32) / l
            og_ref[gi, q_rows, :] = o
            lse_ref[gi, q_rows, :] = jnp.broadcast_to(m + jnp.log2(l), (tq, LANES))
        return carry

    assert (dil * n_tiles) % D_GROUP == 0
    lax.fori_loop(0, dil * n_tiles // D_GROUP, tile_group, 0, unroll=D_UNROLL)


def _dilated_kernel(slope_ref, q_ref, k_ref, v_ref, z_ref, o_ref, og_ref, lse_ref, bias_ref):
    h, g = pl.program_id(1), pl.program_id(2)
    for gi, (window, dil) in enumerate(D_PATTERNS):
        @pl.when(g == gi)
        def _(gi=gi, window=window, dil=dil):
            _dilated_group(gi, window, dil, slope_ref[gi * D_HEADS_PER_GROUP + h],
                           q_ref, k_ref, v_ref, og_ref, lse_ref, bias_ref)

    @pl.when(g == D_GROUPS - 1)
    def _():
        seq = o_ref.shape[0]
        chunk = min(512, seq)

        def merge(ci, carry):
            rs = pl.ds(pl.multiple_of(ci * chunk, chunk), chunk)
            lses = [lse_ref[gi, rs, :] for gi in range(D_GROUPS)]
            top = functools.reduce(jnp.maximum, lses)
            ws = [jnp.exp2(v - top) for v in lses]
            num = functools.reduce(jnp.add, [w * og_ref[gi, rs, :] for gi, w in enumerate(ws)])
            out = num / functools.reduce(jnp.add, ws)
            o_ref[rs, :] = (out * z_ref[rs, :]).astype(o_ref.dtype)
            return carry

        lax.fori_loop(0, seq // chunk, merge, 0)


def _dilated(zf, batch, seq, off_z):
    n = D_GROUPS * D_HEADS_PER_GROUP
    slopes = jnp.asarray(2.0 ** (-8.0 * np.arange(1, n + 1) / n), dtype=_F32)
    for window, dil in D_PATTERNS:
        assert seq % dil == 0 and (seq // dil) % min(D_TQ, seq // dil) == 0
        assert D_SPAN - D_TQ >= 2 * (window // (2 * dil)) or seq // dil <= D_SPAN
    qkv_spec = lambda off: pl.BlockSpec(
        (seq, HD), lambda b, h, g, sl: (b, off // HD + g * D_HEADS_PER_GROUP + h))
    return pl.pallas_call(
        _dilated_kernel,
        grid_spec=pltpu.PrefetchScalarGridSpec(
            num_scalar_prefetch=1,
            grid=(batch, D_HEADS_PER_GROUP, D_GROUPS),
            in_specs=[qkv_spec(off_z["d_q"]), qkv_spec(off_z["d_k"]), qkv_spec(off_z["d_v"]),
                      pl.BlockSpec((seq, HD), lambda b, h, g, sl: (b, off_z["d_z"] // HD + h))],
            out_specs=pl.BlockSpec((seq, HD), lambda b, h, g, sl: (b, h)),
            scratch_shapes=[pltpu.VMEM((D_GROUPS, seq, HD), _F32),
                            pltpu.VMEM((D_GROUPS, seq, LANES), _F32),
                            pltpu.VMEM((3, D_TQ, D_SPAN), _F32)],
        ),
        out_shape=jax.ShapeDtypeStruct((batch * seq, D_OW), _BF16),
        compiler_params=_cparams("parallel", "parallel", "arbitrary"),
        name="dilated_attention",
    )(slopes, zf, zf, zf, zf)


def _merge_kernel(x_ref, wg_ref, *refs, n_br):
    y_refs, wb_refs = refs[:n_br], refs[n_br:2 * n_br]
    o_ref, acc_ref = refs[2 * n_br:]
    br = pl.program_id(2)
    tm = x_ref.shape[0]

    @pl.when(br == 0)
    def _():
        acc_ref[...] = jnp.zeros_like(acc_ref)

    def add_branch(y_ref, wb_ref):
        for c in range(tm // ROW_CHUNK):
            rs = slice(c * ROW_CHUNK, (c + 1) * ROW_CHUNK)
            gate = _sigmoid(jnp.dot(x_ref[rs, :], wg_ref[...], preferred_element_type=_F32))
            acc_ref[rs, :] += gate * jnp.dot(y_ref[rs, :], wb_ref[...], preferred_element_type=_F32)

    for b in range(n_br):
        pl.when(br == b)(functools.partial(add_branch, y_refs[b], wb_refs[b]))

    @pl.when(br == n_br - 1)
    def _():
        o_ref[...] = acc_ref[...].astype(o_ref.dtype)


def _gated_merge(xn, w_b, layer, ys, wbs, gate_start):
    m, d = xn.shape
    n_br = len(ys)
    assert gate_start % TN == 0 and d % TN == 0
    gate_tile0 = gate_start // TN
    tiles_per_branch = d // TN
    y_specs = [pl.BlockSpec((TM, y.shape[1]), lambda i, j, br: (i, 0)) for y in ys]
    wb_specs = [pl.BlockSpec((None, w.shape[1], TN), lambda i, j, br: (layer, 0, j)) for w in wbs]
    return pl.pallas_call(
        functools.partial(_merge_kernel, n_br=n_br),
        grid=(m // TM, d // TN, n_br),
        in_specs=[
            pl.BlockSpec((TM, d), lambda i, j, br: (i, 0)),
            pl.BlockSpec((None, d, TN), lambda i, j, br: (layer, 0, gate_tile0 + br * tiles_per_branch + j)),
            *y_specs, *wb_specs,
        ],
        out_specs=pl.BlockSpec((TM, TN), lambda i, j, br: (i, j)),
        out_shape=jax.ShapeDtypeStruct((m, d), _BF16),
        scratch_shapes=[pltpu.VMEM((TM, TN), _F32)],
        compiler_params=_cparams("parallel", "parallel", "arbitrary"),
        name="gated_merge",
    )(xn, w_b, *ys, *wbs)


def _out_kernel(m_ref, w32_ref, x_ref, o_ref, w_ref):
    @pl.when(pl.program_id(1) == 0)
    def _():
        w_ref[...] = w32_ref[...].astype(w_ref.dtype)

    for c in range(o_ref.shape[0] // ROW_CHUNK):
        rs = slice(c * ROW_CHUNK, (c + 1) * ROW_CHUNK)
        o_ref[rs, :] = x_ref[rs, :] + jnp.dot(m_ref[rs, :], w_ref[...], preferred_element_type=_F32)


def _out_projection(merged, w_out, layer, x):
    m, d = x.shape
    return pl.pallas_call(
        _out_kernel,
        grid=(d // TN, m // TM),
        in_specs=[pl.BlockSpec((TM, d), lambda j, i: (i, 0)),
                  pl.BlockSpec((None, d, TN), lambda j, i: (layer, 0, j)),
                  pl.BlockSpec((TM, TN), lambda j, i: (i, j))],
        out_specs=pl.BlockSpec((TM, TN), lambda j, i: (i, j)),
        out_shape=jax.ShapeDtypeStruct((m, d), _F32),
        scratch_shapes=[pltpu.VMEM((d, TN), _BF16)],
        compiler_params=_cparams("parallel", "arbitrary"),
        name="out_projection",
    )(merged, w_out, x)


def kernel(x, norm_g, w_in, qk_gain, na_rel_bias, diff_lambda, diff_subln_g,
           w_branch_a, w_branch_b, w_branch_c, w_branch_d, w_out):
    batch, seq, d_model = x.shape
    depth = w_in.shape[0]
    m = batch * seq
    assert w_in.shape[2] == GATE_START + N_BRANCHES * d_model
    assert m % TM == 0 and seq % TM == 0 and seq % TQ == 0 and seq % GRID_W == 0 and d_model % TN == 0

    w_gate_b = w_in[:, :, GATE_START:].astype(_BF16)
    wbs = tuple(w.astype(_BF16) for w in (w_branch_a, w_branch_b, w_branch_c, w_branch_d))
    cos_t, sin_t = _rope_tables(seq)
    src_b, modes_b, gkeys_b, off_q = _PLAN_BF16
    src_f, modes_f, gkeys_f, off_z = _PLAN_F32

    xf = x.reshape(m, d_model)
    for l in range(depth):
        xn = _rmsnorm(xf, norm_g[l])
        qkv = _projection(xn, w_in, l, jnp.asarray(src_b), jnp.asarray(modes_b),
                          _gain_table(qk_gain[l], gkeys_b), cos_t, sin_t, _BF16, seq,
                          tuple(sorted(set(modes_b.tolist()))))
        zf = _projection(xn, w_in, l, jnp.asarray(src_f), jnp.asarray(modes_f),
                         _gain_table(qk_gain[l], gkeys_f), cos_t, sin_t, _F32, seq,
                         tuple(sorted(set(modes_f.tolist()))))
        lam_init = 0.8 - 0.6 * math.exp(-0.3 * l)
        y_a = _neighbourhood(qkv, zf, _na_bias_table(na_rel_bias[l], seq // GRID_W), batch, seq, off_q, off_z)
        y_b = _gqa(qkv, zf, batch, seq, off_q, off_z)
        y_c = _diff_attention(qkv, zf, diff_lambda[l], diff_subln_g[l], lam_init, batch, seq, off_q, off_z)
        y_d = _dilated(zf, batch, seq, off_z)
        merged = _gated_merge(xn, w_gate_b, l, (y_a, y_b, y_c, y_d), wbs, 0)
        xf = _out_projection(merged, w_out, l, xf)
    return xf.reshape(batch, seq, d_model)
```

```python
import functools
import math

import jax
import jax.numpy as jnp
import numpy as np
from jax import lax
from jax.experimental import pallas as pl
from jax.experimental.pallas import tpu as pltpu

HD = 128
GRID_W = 64
NA_ROWS, NA_COLS = 8, 16
A_HEADS = 8
B_Q_HEADS, B_KV_HEADS = 8, 2
ROPE_THETA = 10000.0
C_HEADS = 4
D_PATTERNS = ((128, 1), (512, 4), (2048, 16))
D_GROUPS, D_HEADS_PER_GROUP = 3, 4
N_BRANCHES = 4
RMS_EPS = 1e-6
NEG_INF = -1e30
LOG2E = math.log2(math.e)

A_W = A_HEADS * HD
B_QW = B_Q_HEADS * HD
B_KVW = B_KV_HEADS * HD
C_QKW = C_HEADS * 2 * HD
C_VW = C_HEADS * 2 * HD
D_QKVW = D_GROUPS * D_HEADS_PER_GROUP * HD
D_OW = D_HEADS_PER_GROUP * HD
_SEG_NAMES = ("a_q", "a_k", "a_v", "a_z", "b_q", "b_k", "b_v", "b_z",
              "c_q", "c_k", "c_v", "c_z", "d_q", "d_k", "d_v", "d_z")
_SEG_WIDTHS = (A_W, A_W, A_W, A_W, B_QW, B_KVW, B_KVW, B_QW,
               C_QKW, C_QKW, C_VW, C_VW, D_QKVW, D_QKVW, D_QKVW, D_OW)
_SEG_START = dict(zip(_SEG_NAMES, np.cumsum((0,) + _SEG_WIDTHS[:-1]).tolist()))
GATE_START = int(sum(_SEG_WIDTHS))

LANES = 128
V7X_VMEM_BYTES = 64 * 1024 * 1024
VMEM_LIMIT = 56 * 1024 * 1024
TM = 1024
TM_MERGE = 2048
TN = 512
ROW_CHUNK = 256
TM_NORM = 256
TQ = 256
KV_CHUNK = 512
ONES_ROWS = 16
D_TQ, D_SPAN = 256, 512
NA_GROUP, NA_SUPER = 4, 12
NA_BATCH, NA_UNROLL = 4, 1
D_GROUP, D_UNROLL = 4, 2

MODE_PLAIN, MODE_NORM, MODE_NORM_ROPE, MODE_KV_B, MODE_SILU = range(5)

_F32 = jnp.float32
_BF16 = jnp.bfloat16


def _cparams(*sem):
    return pltpu.CompilerParams(dimension_semantics=sem, vmem_limit_bytes=VMEM_LIMIT)


def _sigmoid(v):
    return 0.5 * jnp.tanh(0.5 * v) + 0.5


def _qk_t(q, k):
    return lax.dot_general(q, k, (((1,), (1,)), ((), ())), preferred_element_type=_F32)


def _rmsnorm_kernel(x_ref, g_ref, o_ref):
    x = x_ref[...]
    ms = jnp.mean(x * x, axis=-1, keepdims=True)
    o_ref[...] = (x * lax.rsqrt(ms + RMS_EPS) * g_ref[...]).astype(o_ref.dtype)


def _rmsnorm(x, g):
    m, d = x.shape
    return pl.pallas_call(
        _rmsnorm_kernel,
        grid=(m // TM_NORM,),
        in_specs=[pl.BlockSpec((TM_NORM, d), lambda i: (i, 0)),
                  pl.BlockSpec((1, d), lambda i: (0, 0))],
        out_specs=pl.BlockSpec((TM_NORM, d), lambda i: (i, 0)),
        out_shape=jax.ShapeDtypeStruct((m, d), _BF16),
        compiler_params=_cparams("parallel"),
        name="rmsnorm",
    )(x, g.reshape(1, d))


def _proj_kernel(src_ref, mode_ref, x_ref, w32_ref, g_ref, cos_ref, sin_ref, o_ref, w_ref, *, modes_used):
    del src_ref
    mode = mode_ref[pl.program_id(0)]
    tm, tn = o_ref.shape
    n_heads = tn // HD

    @pl.when(pl.program_id(1) == 0)
    def _():
        w_ref[...] = w32_ref[...].astype(w_ref.dtype)

    def norm_head(acc, h):
        blk = acc[:, h * HD:(h + 1) * HD]
        ms = jnp.mean(blk * blk, axis=-1, keepdims=True)
        return blk * lax.rsqrt(ms + RMS_EPS) * g_ref[:, h * HD:(h + 1) * HD]

    def rope(y, rs):
        even = (lax.broadcasted_iota(jnp.int32, y.shape, 1) % 2) == 0
        partner = jnp.where(even, pltpu.roll(y, HD - 1, 1), pltpu.roll(y, 1, 1))
        return y * cos_ref[rs, :] + partner * sin_ref[rs, :]

    def head_epilogue(mode_id, acc, h, rs):
        is_key_half = h < n_heads // 2
        if mode_id == MODE_NORM:
            return norm_head(acc, h)
        if mode_id == MODE_NORM_ROPE or (mode_id == MODE_KV_B and is_key_half):
            return rope(norm_head(acc, h), rs)
        return acc[:, h * HD:(h + 1) * HD]

    def run(mode_id):
        for c in range(tm // ROW_CHUNK):
            rs = slice(c * ROW_CHUNK, (c + 1) * ROW_CHUNK)
            acc = jnp.dot(x_ref[rs, :], w_ref[...], preferred_element_type=_F32)
            if mode_id == MODE_PLAIN:
                o_ref[rs, :] = acc.astype(o_ref.dtype)
            elif mode_id == MODE_SILU:
                o_ref[rs, :] = (acc * _sigmoid(acc)).astype(o_ref.dtype)
            else:
                for h in range(n_heads):
                    o_ref[rs, h * HD:(h + 1) * HD] = head_epilogue(mode_id, acc, h, rs).astype(o_ref.dtype)

    for mode_id in modes_used:
        pl.when(mode == mode_id)(functools.partial(run, mode_id))


def _projection(xn, w_in, layer, src_tiles, modes, gains, cos_t, sin_t, out_dtype, seq, modes_used):
    m, d = xn.shape
    n_t = src_tiles.shape[0]
    rope_blocks = seq // TM
    return pl.pallas_call(
        functools.partial(_proj_kernel, modes_used=modes_used),
        grid_spec=pltpu.PrefetchScalarGridSpec(
            num_scalar_prefetch=2,
            grid=(n_t, m // TM),
            in_specs=[
                pl.BlockSpec((TM, d), lambda j, i, src, md: (i, 0)),
                pl.BlockSpec((None, d, TN), lambda j, i, src, md: (layer, 0, src[j])),
                pl.BlockSpec((None, 1, TN), lambda j, i, src, md: (j, 0, 0)),
                pl.BlockSpec((TM, HD), lambda j, i, src, md: (i % rope_blocks, 0)),
                pl.BlockSpec((TM, HD), lambda j, i, src, md: (i % rope_blocks, 0)),
            ],
            out_specs=pl.BlockSpec((TM, TN), lambda j, i, src, md: (i, j)),
            scratch_shapes=[pltpu.VMEM((d, TN), _BF16)],
        ),
        out_shape=jax.ShapeDtypeStruct((m, n_t * TN), out_dtype),
        compiler_params=_cparams("parallel", "arbitrary"),
        name="projection",
    )(src_tiles, modes, xn, w_in, gains, cos_t, sin_t)


def _tile_plan(segments):
    src, modes, gain_keys, out_start = [], [], [], {}
    col = 0
    for name, mode, gain_key in segments:
        if name == "b_kv":
            start, width = _SEG_START["b_k"], 2 * B_KVW
            assert width == TN and _SEG_START["b_v"] == start + B_KVW
            out_start["b_k"], out_start["b_v"] = col, col + B_KVW
        else:
            start, width = _SEG_START[name], _SEG_WIDTHS[_SEG_NAMES.index(name)]
            out_start[name] = col
        assert start % TN == 0 and width % TN == 0, (name, start, width)
        for t in range(width // TN):
            src.append(start // TN + t)
            modes.append(mode)
            gain_keys.append(gain_key)
        col += width
    return np.asarray(src, np.int32), np.asarray(modes, np.int32), gain_keys, out_start


_Q_GAIN = HD ** -0.5 * LOG2E
_PLAN_BF16 = _tile_plan((
    ("a_q", MODE_NORM, (0, 0)), ("a_k", MODE_NORM, (0, 1)), ("a_v", MODE_PLAIN, None),
    ("b_q", MODE_NORM_ROPE, (1, 0)), ("b_kv", MODE_KV_B, (1, 1)),
    ("c_q", MODE_NORM, (2, 0)), ("c_k", MODE_NORM, (2, 1)), ("c_v", MODE_PLAIN, None)))
_PLAN_F32 = _tile_plan((
    ("a_z", MODE_SILU, None), ("b_z", MODE_SILU, None), ("c_z", MODE_SILU, None),
    ("d_q", MODE_NORM, (3, 0)), ("d_k", MODE_NORM, (3, 1)), ("d_v", MODE_PLAIN, None),
    ("d_z", MODE_SILU, None)))


def _gain_table(qk_gain, gain_keys):
    rows = []
    for key in gain_keys:
        if key is None:
            rows.append(jnp.ones((TN,), _F32))
        else:
            g = qk_gain[key[0], key[1]].astype(_F32)
            if key[1] == 0:
                g = g * _Q_GAIN
            rows.append(jnp.tile(g, TN // HD))
    return jnp.stack(rows)[:, None, :]


def _rope_tables(seq):
    t = jnp.arange(seq)
    row = (t // GRID_W).astype(_F32)
    col = (t % GRID_W).astype(_F32)
    n_pairs = HD // 4
    inv_freq = ROPE_THETA ** (-jnp.arange(n_pairs, dtype=_F32) / n_pairs)
    ang = jnp.concatenate([row[:, None] * inv_freq, col[:, None] * inv_freq], axis=-1)
    cos, sin = jnp.cos(ang), jnp.sin(ang)
    cos_t = jnp.repeat(cos, 2, axis=-1)
    sin_t = jnp.stack([-sin, sin], axis=-1).reshape(seq, HD)
    return cos_t, sin_t


def _na_window_starts(rows):
    starts = {}
    for r in range(rows):
        w0 = int(np.clip(NA_GROUP * (r // NA_GROUP) - NA_ROWS // 2, 0, rows - NA_SUPER))
        r0 = int(np.clip(r - NA_ROWS // 2, 0, rows - NA_ROWS))
        assert 0 <= r0 - w0 and r0 - w0 + NA_ROWS <= NA_SUPER and 0 <= r - w0 < NA_SUPER
        assert starts.setdefault(r - w0, r0 - w0) == r0 - w0
    return starts


def _na_bias_kernel(rb_ref, o_ref, base_ref, *, win_starts):
    h = pl.program_id(0)
    n_row_off, n_col_off = 2 * NA_ROWS - 1, 2 * NA_COLS - 1
    c = lax.broadcasted_iota(jnp.int32, (GRID_W, GRID_W), 0)
    kc = lax.broadcasted_iota(jnp.int32, (GRID_W, GRID_W), 1)
    c0 = jnp.clip(c - NA_COLS // 2, 0, GRID_W - NA_COLS)
    valid = (kc >= c0) & (kc < c0 + NA_COLS)
    col_off = kc - c + (NA_COLS - 1)
    masked = jnp.full((GRID_W, GRID_W), NEG_INF, _F32)
    for row_off in range(n_row_off):
        blk = masked
        for d in range(n_col_off):
            blk = jnp.where(col_off == d, rb_ref[(h * n_row_off + row_off) * n_col_off + d] * LOG2E, blk)
        base_ref[row_off] = jnp.where(valid, blk, NEG_INF)
    for dw in range(NA_SUPER):
        ws = win_starts.get(dw, 0)
        for i in range(NA_SUPER):
            in_window = ws <= i < ws + NA_ROWS
            o_ref[dw, :, i * GRID_W:(i + 1) * GRID_W] = base_ref[i - dw + NA_ROWS - 1] if in_window else masked


def _na_bias_table(rel_bias, rows):
    heads = rel_bias.shape[0]
    return pl.pallas_call(
        functools.partial(_na_bias_kernel, win_starts=_na_window_starts(rows)),
        grid=(heads,),
        in_specs=[pl.BlockSpec(memory_space=pltpu.SMEM)],
        out_specs=pl.BlockSpec((None, NA_SUPER, GRID_W, NA_SUPER * GRID_W), lambda h: (h, 0, 0, 0)),
        out_shape=jax.ShapeDtypeStruct((heads, NA_SUPER, GRID_W, NA_SUPER * GRID_W), _F32),
        scratch_shapes=[pltpu.VMEM((2 * NA_ROWS - 1, GRID_W, GRID_W), _F32)],
        compiler_params=_cparams("parallel"),
        name="na_bias_table",
    )(rel_bias.astype(_F32).reshape(-1))


def _na_kernel(q_ref, k_ref, v_ref, z_ref, bias_ref, o_ref, *, rows):
    gq = NA_GROUP * GRID_W
    sw = NA_SUPER * GRID_W

    def row_groups(t, carry):
        addr = []
        for k in range(NA_BATCH):
            g = t * NA_BATCH + k
            w0 = jnp.clip(NA_GROUP * g - NA_ROWS // 2, 0, rows - NA_SUPER)
            addr.append((pl.ds(pl.multiple_of(g * gq, gq), gq), pl.ds(pl.multiple_of(w0 * GRID_W, GRID_W), sw),
                         NA_GROUP * g - w0))
        scores = [_qk_t(q_ref[qs, :], k_ref[ks, :]) + bias_ref[pl.ds(dw, NA_GROUP)].reshape(gq, sw)
                  for qs, ks, dw in addr]
        stats = []
        for s in scores:
            p = jnp.exp2(s - jnp.max(s, axis=-1, keepdims=True))
            stats.append((jnp.sum(p, axis=-1, keepdims=True), p.astype(_BF16)))
        for (qs, ks, _), (l, p) in zip(addr, stats):
            o = jnp.dot(p, v_ref[ks, :], preferred_element_type=_F32) / l
            o_ref[qs, :] = (o * z_ref[qs, :]).astype(o_ref.dtype)
        return carry

    assert (rows // NA_GROUP) % NA_BATCH == 0
    lax.fori_loop(0, rows // NA_GROUP // NA_BATCH, row_groups, 0, unroll=NA_UNROLL)


def _neighbourhood(qkv, zf, bias_tab, batch, seq, off_q, off_z):
    rows = seq // GRID_W
    assert rows >= NA_SUPER and rows % NA_GROUP == 0
    blk = lambda off: pl.BlockSpec((seq, HD), lambda b, h: (b, off // HD + h))
    return pl.pallas_call(
        functools.partial(_na_kernel, rows=rows),
        grid=(batch, A_HEADS),
        in_specs=[blk(off_q["a_q"]), blk(off_q["a_k"]), blk(off_q["a_v"]), blk(off_z["a_z"]),
                  pl.BlockSpec((None, NA_SUPER, GRID_W, NA_SUPER * GRID_W), lambda b, h: (h, 0, 0, 0))],
        out_specs=pl.BlockSpec((seq, HD), lambda b, h: (b, h)),
        out_shape=jax.ShapeDtypeStruct((batch * seq, A_W), _BF16),
        compiler_params=_cparams("parallel", "parallel"),
        name="neighbourhood_attention",
    )(qkv, qkv, qkv, zf, bias_tab)


def _gqa_kernel(q_ref, k_ref, v_ref, z_ref, o_ref):
    n_heads = q_ref.shape[1] // HD
    group = n_heads // (k_ref.shape[1] // HD)
    head_slice = lambda h: slice(h * HD, (h + 1) * HD)
    scores = lambda h: _qk_t(q_ref[:, head_slice(h)], k_ref[:, head_slice(h // group)])
    s_next = scores(0)
    for h in range(n_heads):
        s = s_next
        if h + 1 < n_heads:
            s_next = scores(h + 1)
        m = jnp.max(s, axis=-1, keepdims=True)
        p = jnp.exp2(s - m)
        l = jnp.sum(p, axis=-1, keepdims=True)
        o = jnp.dot(p.astype(_BF16), v_ref[:, head_slice(h // group)], preferred_element_type=_F32) / l
        o_ref[:, head_slice(h)] = (o * z_ref[:, head_slice(h)]).astype(o_ref.dtype)


def _gqa(qkv, zf, batch, seq, off_q, off_z):
    nq = seq // TQ
    assert off_q["b_q"] % B_QW == 0 and off_z["b_z"] % B_QW == 0 and off_q["b_k"] % B_KVW == 0
    q_spec = lambda off: pl.BlockSpec((TQ, B_QW), lambda b, i: (b * nq + i, off // B_QW))
    kv_spec = lambda off: pl.BlockSpec((seq, B_KVW), lambda b, i: (b, off // B_KVW))
    return pl.pallas_call(
        _gqa_kernel,
        grid=(batch, nq),
        in_specs=[q_spec(off_q["b_q"]), kv_spec(off_q["b_k"]), kv_spec(off_q["b_v"]), q_spec(off_z["b_z"])],
        out_specs=pl.BlockSpec((TQ, B_QW), lambda b, i: (b * nq + i, 0)),
        out_shape=jax.ShapeDtypeStruct((batch * seq, B_QW), _BF16),
        compiler_params=_cparams("parallel", "parallel"),
        name="gqa_attention",
    )(qkv, qkv, qkv, zf)


def _store_transposed(vt_ref, v_ref):
    seq, d = v_ref.shape
    for c in range(seq // KV_CHUNK):
        cs = slice(c * KV_CHUNK, (c + 1) * KV_CHUNK)
        vt_ref[:d, cs] = v_ref[cs, :].astype(_F32).T.astype(vt_ref.dtype)
    vt_ref[d:, :] = jnp.ones((vt_ref.shape[0] - d, seq), vt_ref.dtype)


class _OnlineSoftmax:
    def __init__(self, d_v, n_q):
        self.d_v = d_v
        self.m = jnp.full((1, n_q), NEG_INF, _F32)
        self.acc = jnp.zeros((d_v + ONES_ROWS, n_q), _F32)

    def update(self, s, vt_chunk):
        m_new = jnp.maximum(self.m, jnp.max(s, axis=0, keepdims=True))
        p = jnp.exp2(s - m_new).astype(_BF16)
        self.acc = (jnp.exp2(self.m - m_new) * self.acc
                    + jnp.dot(vt_chunk, p, preferred_element_type=_F32))
        self.m = m_new

    def result(self):
        return self.acc[:self.d_v] / self.acc[self.d_v:self.d_v + 1]


def _diff_kernel(slope_ref, q_ref, k_ref, v_ref, z_ref, lam_ref, g_ref, o_ref, vt_ref, pen_ref, *, lam_init):
    h, i = pl.program_id(1), pl.program_id(2)
    seq = k_ref.shape[0]
    tq, hw = q_ref.shape
    key0 = seq - tq

    @pl.when(i == 0)
    def _():
        _store_transposed(vt_ref, v_ref)
        slope = slope_ref[h] * LOG2E
        n_rows = pen_ref.shape[0]
        for y0 in range(0, n_rows, KV_CHUNK):
            size = min(KV_CHUNK, n_rows - y0)
            row_minus_col = (lax.broadcasted_iota(jnp.int32, (size, tq), 0)
                             - lax.broadcasted_iota(jnp.int32, (size, tq), 1))
            pen_ref[y0:y0 + size, :] = slope * jnp.abs(row_minus_col + (y0 - key0)).astype(_F32)

    lp = lam_ref[...]
    lam = (jnp.exp(jnp.sum(lp[0:1] * lp[1:2], axis=-1, keepdims=True))
           - jnp.exp(jnp.sum(lp[2:3] * lp[3:4], axis=-1, keepdims=True)) + lam_init)

    states = [_OnlineSoftmax(hw, tq) for _ in range(2)]
    pen0 = pl.multiple_of(key0 - i * tq, tq)
    for c in range(seq // KV_CHUNK):
        cs = slice(c * KV_CHUNK, (c + 1) * KV_CHUNK)
        pen = pen_ref[pl.ds(pen0 + c * KV_CHUNK, KV_CHUNK), :]
        for mi, state in enumerate(states):
            sl = slice(mi * HD, (mi + 1) * HD)
            state.update(_qk_t(k_ref[cs, sl], q_ref[:, sl]) - pen, vt_ref[:, cs])
    o = (states[0].result() - lam * states[1].result()).T
    ms = jnp.mean(o * o, axis=-1, keepdims=True)
    o = o * lax.rsqrt(ms + RMS_EPS) * g_ref[...] * (1.0 - lam_init)
    o_ref[...] = (o * z_ref[...]).astype(o_ref.dtype)


def _diff_attention(qkv, zf, diff_lambda, subln_g, lam_init, batch, seq, off_q, off_z):
    hw = 2 * HD
    nq = seq // TQ
    slopes = jnp.asarray(2.0 ** (-8.0 * np.arange(1, C_HEADS + 1) / C_HEADS), dtype=_F32)
    q_spec = lambda off: pl.BlockSpec((TQ, hw), lambda b, h, i, sl: (b * nq + i, off // hw + h))
    kv_spec = lambda off: pl.BlockSpec((seq, hw), lambda b, h, i, sl: (b, off // hw + h))
    return pl.pallas_call(
        functools.partial(_diff_kernel, lam_init=lam_init),
        grid_spec=pltpu.PrefetchScalarGridSpec(
            num_scalar_prefetch=1,
            grid=(batch, C_HEADS, nq),
            in_specs=[q_spec(off_q["c_q"]), kv_spec(off_q["c_k"]), kv_spec(off_q["c_v"]), q_spec(off_z["c_z"]),
                      pl.BlockSpec((4, HD), lambda b, h, i, sl: (0, 0)),
                      pl.BlockSpec((1, hw), lambda b, h, i, sl: (0, 0))],
            out_specs=pl.BlockSpec((TQ, hw), lambda b, h, i, sl: (b * nq + i, h)),
            scratch_shapes=[pltpu.VMEM((hw + ONES_ROWS, seq), _BF16),
                            pltpu.VMEM((2 * seq - TQ, TQ), _F32)],
        ),
        out_shape=jax.ShapeDtypeStruct((batch * seq, C_VW), _BF16),
        compiler_params=_cparams("parallel", "parallel", "arbitrary"),
        name="diff_attention",
    )(slopes, qkv, qkv, qkv, zf, diff_lambda.astype(_F32), subln_g.astype(_F32).reshape(1, hw))


def _dilated_group(gi, window, dil, slope, q_ref, k_ref, v_ref, og_ref, lse_ref, bias_ref):
    seq = q_ref.shape[0]
    length = seq // dil
    tq = min(D_TQ, length)
    span = min(D_SPAN, length)
    n_tiles = length // tq
    n_side = window // (2 * dil)
    half = (span - tq) // 2
    unit_pen = slope * (float(dil) * LOG2E)
    rel0 = (lax.broadcasted_iota(jnp.int32, (tq, span), 1)
            - lax.broadcasted_iota(jnp.int32, (tq, span), 0))
    for vi in range(3 if n_tiles > 1 else 1):
        rel = jnp.abs(rel0 - vi * half)
        bias_ref[vi, :tq, :span] = jnp.where(rel <= n_side, -unit_pen * rel.astype(_F32), NEG_INF)

    def rows(start, size):
        return pl.ds(start, size) if dil == 1 else pl.ds(start, size, stride=dil)

    def tile_group(gidx, carry):
        tiles = []
        for k in range(D_GROUP):
            idx = gidx * D_GROUP + k
            r = lax.div(idx, n_tiles)
            u0 = lax.rem(idx, n_tiles) * tq
            ks = jnp.clip(u0 - half, 0, length - span)
            vi = lax.div(u0 - ks, half) if n_tiles > 1 else 0
            tiles.append((rows(r + dil * u0, tq), rows(r + dil * ks, span), vi))
        scores = [_qk_t(q_ref[q_rows, :].astype(_BF16), k_ref[k_rows, :].astype(_BF16)) + bias_ref[vi, :tq, :span]
                  for q_rows, k_rows, vi in tiles]
        stats = []
        for s in scores:
            m = jnp.max(s, axis=-1, keepdims=True)
            p = jnp.exp2(s - m)
            stats.append((m, jnp.sum(p, axis=-1, keepdims=True), p.astype(_BF16)))
        for (q_rows, k_rows, _), (m, l, p) in zip(tiles, stats):
            o = jnp.dot(p, v_ref[k_rows, :].astype(_BF16), preferred_element_type=_F32) / l
            og_ref[gi, q_rows, :] = o
            lse_ref[gi, q_rows, :] = jnp.broadcast_to(m + jnp.log2(l), (tq, LANES))
        return carry

    assert (dil * n_tiles) % D_GROUP == 0
    lax.fori_loop(0, dil * n_tiles // D_GROUP, tile_group, 0, unroll=D_UNROLL)


def _dilated_kernel(slope_ref, q_ref, k_ref, v_ref, z_ref, o_ref, og_ref, lse_ref, bias_ref):
    h, g = pl.program_id(1), pl.program_id(2)
    for gi, (window, dil) in enumerate(D_PATTERNS):
        @pl.when(g == gi)
        def _(gi=gi, window=window, dil=dil):
            _dilated_group(gi, window, dil, slope_ref[gi * D_HEADS_PER_GROUP + h],
                           q_ref, k_ref, v_ref, og_ref, lse_ref, bias_ref)

    @pl.when(g == D_GROUPS - 1)
    def _():
        seq = o_ref.shape[0]
        chunk = min(512, seq)

        def merge(ci, carry):
            rs = pl.ds(pl.multiple_of(ci * chunk, chunk), chunk)
            lses = [lse_ref[gi, rs, :] for gi in range(D_GROUPS)]
            top = functools.reduce(jnp.maximum, lses)
            ws = [jnp.exp2(v - top) for v in lses]
            num = functools.reduce(jnp.add, [w * og_ref[gi, rs, :] for gi, w in enumerate(ws)])
            out = num / functools.reduce(jnp.add, ws)
            o_ref[rs, :] = (out * z_ref[rs, :]).astype(o_ref.dtype)
            return carry

        lax.fori_loop(0, seq // chunk, merge, 0)


def _dilated(zf, batch, seq, off_z):
    n = D_GROUPS * D_HEADS_PER_GROUP
    slopes = jnp.asarray(2.0 ** (-8.0 * np.arange(1, n + 1) / n), dtype=_F32)
    for window, dil in D_PATTERNS:
        assert seq % dil == 0 and (seq // dil) % min(D_TQ, seq // dil) == 0
        assert D_SPAN - D_TQ >= 2 * (window // (2 * dil)) or seq // dil <= D_SPAN
    qkv_spec = lambda off: pl.BlockSpec(
        (seq, HD), lambda b, h, g, sl: (b, off // HD + g * D_HEADS_PER_GROUP + h))
    return pl.pallas_call(
        _dilated_kernel,
        grid_spec=pltpu.PrefetchScalarGridSpec(
            num_scalar_prefetch=1,
            grid=(batch, D_HEADS_PER_GROUP, D_GROUPS),
            in_specs=[qkv_spec(off_z["d_q"]), qkv_spec(off_z["d_k"]), qkv_spec(off_z["d_v"]),
                      pl.BlockSpec((seq, HD), lambda b, h, g, sl: (b, off_z["d_z"] // HD + h))],
            out_specs=pl.BlockSpec((seq, HD), lambda b, h, g, sl: (b, h)),
            scratch_shapes=[pltpu.VMEM((D_GROUPS, seq, HD), _F32),
                            pltpu.VMEM((D_GROUPS, seq, LANES), _F32),
                            pltpu.VMEM((3, D_TQ, D_SPAN), _F32)],
        ),
        out_shape=jax.ShapeDtypeStruct((batch * seq, D_OW), _BF16),
        compiler_params=_cparams("parallel", "parallel", "arbitrary"),
        name="dilated_attention",
    )(slopes, zf, zf, zf, zf)


def _merge_kernel(x_ref, wg_ref, *refs, n_br):
    y_refs, wb_refs = refs[:n_br], refs[n_br:2 * n_br]
    o_ref, acc_ref = refs[2 * n_br:]
    br = pl.program_id(2)
    tm = x_ref.shape[0]

    @pl.when(br == 0)
    def _():
        acc_ref[...] = jnp.zeros_like(acc_ref)

    def add_branch(y_ref, wb_ref):
        for c in range(tm // ROW_CHUNK):
            rs = slice(c * ROW_CHUNK, (c + 1) * ROW_CHUNK)
            gate = _sigmoid(jnp.dot(x_ref[rs, :], wg_ref[...], preferred_element_type=_F32))
            acc_ref[rs, :] += gate * jnp.dot(y_ref[rs, :], wb_ref[...], preferred_element_type=_F32)

    for b in range(n_br):
        pl.when(br == b)(functools.partial(add_branch, y_refs[b], wb_refs[b]))

    @pl.when(br == n_br - 1)
    def _():
        o_ref[...] = acc_ref[...].astype(o_ref.dtype)


def _gated_merge(xn, w_b, layer, ys, wbs, gate_start):
    m, d = xn.shape
    n_br = len(ys)
    assert gate_start % TN == 0 and d % TN == 0
    gate_tile0 = gate_start // TN
    tiles_per_branch = d // TN
    once = dict(pipeline_mode=pl.Buffered(1))
    y_specs = [pl.BlockSpec((TM_MERGE, y.shape[1]), lambda i, j, br: (i, 0), **once) for y in ys]
    wb_specs = [pl.BlockSpec((None, w.shape[1], TN), lambda i, j, br: (layer, 0, j)) for w in wbs]
    return pl.pallas_call(
        functools.partial(_merge_kernel, n_br=n_br),
        grid=(m // TM_MERGE, d // TN, n_br),
        in_specs=[
            pl.BlockSpec((TM_MERGE, d), lambda i, j, br: (i, 0), **once),
            pl.BlockSpec((None, d, TN), lambda i, j, br: (layer, 0, gate_tile0 + br * tiles_per_branch + j)),
            *y_specs, *wb_specs,
        ],
        out_specs=pl.BlockSpec((TM_MERGE, TN), lambda i, j, br: (i, j)),
        out_shape=jax.ShapeDtypeStruct((m, d), _BF16),
        scratch_shapes=[pltpu.VMEM((TM_MERGE, TN), _F32)],
        compiler_params=_cparams("parallel", "parallel", "arbitrary"),
        name="gated_merge",
    )(xn, w_b, *ys, *wbs)


def _out_kernel(m_ref, w32_ref, x_ref, o_ref, w_ref):
    @pl.when(pl.program_id(1) == 0)
    def _():
        w_ref[...] = w32_ref[...].astype(w_ref.dtype)

    for c in range(o_ref.shape[0] // ROW_CHUNK):
        rs = slice(c * ROW_CHUNK, (c + 1) * ROW_CHUNK)
        o_ref[rs, :] = x_ref[rs, :] + jnp.dot(m_ref[rs, :], w_ref[...], preferred_element_type=_F32)


def _out_projection(merged, w_out, layer, x):
    m, d = x.shape
    return pl.pallas_call(
        _out_kernel,
        grid=(d // TN, m // TM),
        in_specs=[pl.BlockSpec((TM, d), lambda j, i: (i, 0)),
                  pl.BlockSpec((None, d, TN), lambda j, i: (layer, 0, j)),
                  pl.BlockSpec((TM, TN), lambda j, i: (i, j))],
        out_specs=pl.BlockSpec((TM, TN), lambda j, i: (i, j)),
        out_shape=jax.ShapeDtypeStruct((m, d), _F32),
        scratch_shapes=[pltpu.VMEM((d, TN), _BF16)],
        compiler_params=_cparams("parallel", "arbitrary"),
        name="out_projection",
    )(merged, w_out, x)


def kernel(x, norm_g, w_in, qk_gain, na_rel_bias, diff_lambda, diff_subln_g,
           w_branch_a, w_branch_b, w_branch_c, w_branch_d, w_out):
    batch, seq, d_model = x.shape
    depth = w_in.shape[0]
    m = batch * seq
    assert w_in.shape[2] == GATE_START + N_BRANCHES * d_model
    assert m % TM == 0 and seq % TM == 0 and seq % TQ == 0 and seq % GRID_W == 0 and d_model % TN == 0

    w_gate_b = w_in[:, :, GATE_START:].astype(_BF16)
    wbs = tuple(w.astype(_BF16) for w in (w_branch_a, w_branch_b, w_branch_c, w_branch_d))
    cos_t, sin_t = _rope_tables(seq)
    src_b, modes_b, gkeys_b, off_q = _PLAN_BF16
    src_f, modes_f, gkeys_f, off_z = _PLAN_F32

    xf = x.reshape(m, d_model)
    for l in range(depth):
        xn = _rmsnorm(xf, norm_g[l])
        qkv = _projection(xn, w_in, l, jnp.asarray(src_b), jnp.asarray(modes_b),
                          _gain_table(qk_gain[l], gkeys_b), cos_t, sin_t, _BF16, seq,
                          tuple(sorted(set(modes_b.tolist()))))
        zf = _projection(xn, w_in, l, jnp.asarray(src_f), jnp.asarray(modes_f),
                         _gain_table(qk_gain[l], gkeys_f), cos_t, sin_t, _F32, seq,
                         tuple(sorted(set(modes_f.tolist()))))
        lam_init = 0.8 - 0.6 * math.exp(-0.3 * l)
        y_a = _neighbourhood(qkv, zf, _na_bias_table(na_rel_bias[l], seq // GRID_W), batch, seq, off_q, off_z)
        y_b = _gqa(qkv, zf, batch, seq, off_q, off_z)
        y_c = _diff_attention(qkv, zf, diff_lambda[l], diff_subln_g[l], lam_init, batch, seq, off_q, off_z)
        y_d = _dilated(zf, batch, seq, off_z)
        merged = _gated_merge(xn, w_gate_b, l, (y_a, y_b, y_c, y_d), wbs, 0)
        xf = _out_projection(merged, w_out, l, xf)
    return xf.reshape(batch, seq, d_model)
```

```python
import functools
import math

import jax
import jax.numpy as jnp
import numpy as np
from jax import lax
from jax.experimental import pallas as pl
from jax.experimental.pallas import tpu as pltpu

HD = 128
GRID_W = 64
NA_ROWS, NA_COLS = 8, 16
A_HEADS = 8
B_Q_HEADS, B_KV_HEADS = 8, 2
ROPE_THETA = 10000.0
C_HEADS = 4
D_PATTERNS = ((128, 1), (512, 4), (2048, 16))
D_GROUPS, D_HEADS_PER_GROUP = 3, 4
N_BRANCHES = 4
RMS_EPS = 1e-6
NEG_INF = -1e30
LOG2E = math.log2(math.e)

A_W = A_HEADS * HD
B_QW = B_Q_HEADS * HD
B_KVW = B_KV_HEADS * HD
C_QKW = C_HEADS * 2 * HD
C_VW = C_HEADS * 2 * HD
D_QKVW = D_GROUPS * D_HEADS_PER_GROUP * HD
D_OW = D_HEADS_PER_GROUP * HD
_SEG_NAMES = ("a_q", "a_k", "a_v", "a_z", "b_q", "b_k", "b_v", "b_z",
              "c_q", "c_k", "c_v", "c_z", "d_q", "d_k", "d_v", "d_z")
_SEG_WIDTHS = (A_W, A_W, A_W, A_W, B_QW, B_KVW, B_KVW, B_QW,
               C_QKW, C_QKW, C_VW, C_VW, D_QKVW, D_QKVW, D_QKVW, D_OW)
_SEG_START = dict(zip(_SEG_NAMES, np.cumsum((0,) + _SEG_WIDTHS[:-1]).tolist()))
GATE_START = int(sum(_SEG_WIDTHS))

LANES = 128
V7X_VMEM_BYTES = 64 * 1024 * 1024
VMEM_LIMIT = V7X_VMEM_BYTES - 8 * 1024 * 1024
TM = 1024
TM_MERGE = 2048
TN = 512
ROW_CHUNK = 256
TM_NORM = 512
TQ = 256
KV_CHUNK = 512
ONES_ROWS = 16
D_TQ, D_SPAN = 256, 512
NA_GROUP, NA_SUPER = 4, 12
NA_BATCH, NA_UNROLL = 4, 1
D_GROUP, D_UNROLL = 4, 2

MODE_PLAIN, MODE_NORM, MODE_NORM_ROPE, MODE_KV_B, MODE_SILU = range(5)

_F32 = jnp.float32
_BF16 = jnp.bfloat16


def _cparams(*sem):
    return pltpu.CompilerParams(dimension_semantics=sem, vmem_limit_bytes=VMEM_LIMIT)


def _sigmoid(v):
    return 0.5 * jnp.tanh(0.5 * v) + 0.5


def _qk_t(q, k):
    return lax.dot_general(q, k, (((1,), (1,)), ((), ())), preferred_element_type=_F32)


def _rmsnorm_kernel(x_ref, g_ref, o_ref):
    x = x_ref[...]
    ms = jnp.mean(x * x, axis=-1, keepdims=True)
    o_ref[...] = (x * lax.rsqrt(ms + RMS_EPS) * g_ref[...]).astype(o_ref.dtype)


def _rmsnorm(x, g):
    m, d = x.shape
    return pl.pallas_call(
        _rmsnorm_kernel,
        grid=(m // TM_NORM,),
        in_specs=[pl.BlockSpec((TM_NORM, d), lambda i: (i, 0)),
                  pl.BlockSpec((1, d), lambda i: (0, 0))],
        out_specs=pl.BlockSpec((TM_NORM, d), lambda i: (i, 0)),
        out_shape=jax.ShapeDtypeStruct((m, d), _BF16),
        compiler_params=_cparams("parallel"),
        name="rmsnorm",
    )(x, g.reshape(1, d))


def _proj_kernel(src_ref, mode_ref, x_ref, w32_ref, g_ref, cos_ref, sin_ref, o_ref, w_ref, *, modes_used):
    del src_ref
    mode = mode_ref[pl.program_id(0)]
    tm, tn = o_ref.shape
    n_heads = tn // HD

    @pl.when(pl.program_id(1) == 0)
    def _():
        w_ref[...] = w32_ref[...].astype(w_ref.dtype)

    def norm_head(acc, h):
        blk = acc[:, h * HD:(h + 1) * HD]
        ms = jnp.mean(blk * blk, axis=-1, keepdims=True)
        return blk * lax.rsqrt(ms + RMS_EPS) * g_ref[:, h * HD:(h + 1) * HD]

    def rope(y, rs):
        even = (lax.broadcasted_iota(jnp.int32, y.shape, 1) % 2) == 0
        partner = jnp.where(even, pltpu.roll(y, HD - 1, 1), pltpu.roll(y, 1, 1))
        return y * cos_ref[rs, :] + partner * sin_ref[rs, :]

    def head_epilogue(mode_id, acc, h, rs):
        is_key_half = h < n_heads // 2
        if mode_id == MODE_NORM:
            return norm_head(acc, h)
        if mode_id == MODE_NORM_ROPE or (mode_id == MODE_KV_B and is_key_half):
            return rope(norm_head(acc, h), rs)
        return acc[:, h * HD:(h + 1) * HD]

    def run(mode_id):
        for c in range(tm // ROW_CHUNK):
            rs = slice(c * ROW_CHUNK, (c + 1) * ROW_CHUNK)
            acc = jnp.dot(x_ref[rs, :], w_ref[...], preferred_element_type=_F32)
            if mode_id == MODE_PLAIN:
                o_ref[rs, :] = acc.astype(o_ref.dtype)
            elif mode_id == MODE_SILU:
                o_ref[rs, :] = (acc * _sigmoid(acc)).astype(o_ref.dtype)
            else:
                for h in range(n_heads):
                    o_ref[rs, h * HD:(h + 1) * HD] = head_epilogue(mode_id, acc, h, rs).astype(o_ref.dtype)

    for mode_id in modes_used:
        pl.when(mode == mode_id)(functools.partial(run, mode_id))


def _projection(xn, w_in, layer, src_tiles, modes, gains, cos_t, sin_t, out_dtype, seq, modes_used):
    m, d = xn.shape
    n_t = src_tiles.shape[0]
    rope_blocks = seq // TM
    return pl.pallas_call(
        functools.partial(_proj_kernel, modes_used=modes_used),
        grid_spec=pltpu.PrefetchScalarGridSpec(
            num_scalar_prefetch=2,
            grid=(n_t, m // TM),
            in_specs=[
                pl.BlockSpec((TM, d), lambda j, i, src, md: (i, 0)),
                pl.BlockSpec((None, d, TN), lambda j, i, src, md: (layer, 0, src[j])),
                pl.BlockSpec((None, 1, TN), lambda j, i, src, md: (j, 0, 0)),
                pl.BlockSpec((TM, HD), lambda j, i, src, md: (i % rope_blocks, 0)),
                pl.BlockSpec((TM, HD), lambda j, i, src, md: (i % rope_blocks, 0)),
            ],
            out_specs=pl.BlockSpec((TM, TN), lambda j, i, src, md: (i, j)),
            scratch_shapes=[pltpu.VMEM((d, TN), _BF16)],
        ),
        out_shape=jax.ShapeDtypeStruct((m, n_t * TN), out_dtype),
        compiler_params=_cparams("parallel", "arbitrary"),
        name="projection",
    )(src_tiles, modes, xn, w_in, gains, cos_t, sin_t)


def _tile_plan(segments):
    src, modes, gain_keys, out_start = [], [], [], {}
    col = 0
    for name, mode, gain_key in segments:
        if name == "b_kv":
            start, width = _SEG_START["b_k"], 2 * B_KVW
            assert width == TN and _SEG_START["b_v"] == start + B_KVW
            out_start["b_k"], out_start["b_v"] = col, col + B_KVW
        else:
            start, width = _SEG_START[name], _SEG_WIDTHS[_SEG_NAMES.index(name)]
            out_start[name] = col
        assert start % TN == 0 and width % TN == 0, (name, start, width)
        for t in range(width // TN):
            src.append(start // TN + t)
            modes.append(mode)
            gain_keys.append(gain_key)
        col += width
    return np.asarray(src, np.int32), np.asarray(modes, np.int32), gain_keys, out_start


_Q_GAIN = HD ** -0.5 * LOG2E
_PLAN_BF16 = _tile_plan((
    ("a_q", MODE_NORM, (0, 0)), ("a_k", MODE_NORM, (0, 1)), ("a_v", MODE_PLAIN, None),
    ("b_q", MODE_NORM_ROPE, (1, 0)), ("b_kv", MODE_KV_B, (1, 1)),
    ("c_q", MODE_NORM, (2, 0)), ("c_k", MODE_NORM, (2, 1)), ("c_v", MODE_PLAIN, None)))
_PLAN_F32 = _tile_plan((
    ("a_z", MODE_SILU, None), ("b_z", MODE_SILU, None), ("c_z", MODE_SILU, None),
    ("d_q", MODE_NORM, (3, 0)), ("d_k", MODE_NORM, (3, 1)), ("d_v", MODE_PLAIN, None),
    ("d_z", MODE_SILU, None)))


def _gain_table(qk_gain, gain_keys):
    rows = []
    for key in gain_keys:
        if key is None:
            rows.append(jnp.ones((TN,), _F32))
        else:
            g = qk_gain[key[0], key[1]].astype(_F32)
            if key[1] == 0:
                g = g * _Q_GAIN
            rows.append(jnp.tile(g, TN // HD))
    return jnp.stack(rows)[:, None, :]


def _rope_tables(seq):
    t = jnp.arange(seq)
    row = (t // GRID_W).astype(_F32)
    col = (t % GRID_W).astype(_F32)
    n_pairs = HD // 4
    inv_freq = ROPE_THETA ** (-jnp.arange(n_pairs, dtype=_F32) / n_pairs)
    ang = jnp.concatenate([row[:, None] * inv_freq, col[:, None] * inv_freq], axis=-1)
    cos, sin = jnp.cos(ang), jnp.sin(ang)
    cos_t = jnp.repeat(cos, 2, axis=-1)
    sin_t = jnp.stack([-sin, sin], axis=-1).reshape(seq, HD)
    return cos_t, sin_t


def _na_window_starts(rows):
    starts = {}
    for r in range(rows):
        w0 = int(np.clip(NA_GROUP * (r // NA_GROUP) - NA_ROWS // 2, 0, rows - NA_SUPER))
        r0 = int(np.clip(r - NA_ROWS // 2, 0, rows - NA_ROWS))
        assert 0 <= r0 - w0 and r0 - w0 + NA_ROWS <= NA_SUPER and 0 <= r - w0 < NA_SUPER
        assert starts.setdefault(r - w0, r0 - w0) == r0 - w0
    return starts


def _na_bias_kernel(rb_ref, o_ref, base_ref, *, win_starts):
    h = pl.program_id(0)
    n_row_off, n_col_off = 2 * NA_ROWS - 1, 2 * NA_COLS - 1
    c = lax.broadcasted_iota(jnp.int32, (GRID_W, GRID_W), 0)
    kc = lax.broadcasted_iota(jnp.int32, (GRID_W, GRID_W), 1)
    c0 = jnp.clip(c - NA_COLS // 2, 0, GRID_W - NA_COLS)
    valid = (kc >= c0) & (kc < c0 + NA_COLS)
    col_off = kc - c + (NA_COLS - 1)
    masked = jnp.full((GRID_W, GRID_W), NEG_INF, _F32)
    for row_off in range(n_row_off):
        blk = masked
        for d in range(n_col_off):
            blk = jnp.where(col_off == d, rb_ref[(h * n_row_off + row_off) * n_col_off + d] * LOG2E, blk)
        base_ref[row_off] = jnp.where(valid, blk, NEG_INF)
    for dw in range(NA_SUPER):
        ws = win_starts.get(dw, 0)
        for i in range(NA_SUPER):
            in_window = ws <= i < ws + NA_ROWS
            o_ref[dw, :, i * GRID_W:(i + 1) * GRID_W] = base_ref[i - dw + NA_ROWS - 1] if in_window else masked


def _na_bias_table(rel_bias, rows):
    heads = rel_bias.shape[0]
    return pl.pallas_call(
        functools.partial(_na_bias_kernel, win_starts=_na_window_starts(rows)),
        grid=(heads,),
        in_specs=[pl.BlockSpec(memory_space=pltpu.SMEM)],
        out_specs=pl.BlockSpec((None, NA_SUPER, GRID_W, NA_SUPER * GRID_W), lambda h: (h, 0, 0, 0)),
        out_shape=jax.ShapeDtypeStruct((heads, NA_SUPER, GRID_W, NA_SUPER * GRID_W), _F32),
        scratch_shapes=[pltpu.VMEM((2 * NA_ROWS - 1, GRID_W, GRID_W), _F32)],
        compiler_params=_cparams("parallel"),
        name="na_bias_table",
    )(rel_bias.astype(_F32).reshape(-1))


def _na_kernel(q_ref, k_ref, v_ref, z_ref, bias_ref, o_ref, *, rows):
    gq = NA_GROUP * GRID_W
    sw = NA_SUPER * GRID_W

    def row_groups(t, carry):
        addr = []
        for k in range(NA_BATCH):
            g = t * NA_BATCH + k
            w0 = jnp.clip(NA_GROUP * g - NA_ROWS // 2, 0, rows - NA_SUPER)
            addr.append((pl.ds(pl.multiple_of(g * gq, gq), gq), pl.ds(pl.multiple_of(w0 * GRID_W, GRID_W), sw),
                         NA_GROUP * g - w0))
        scores = [_qk_t(q_ref[qs, :], k_ref[ks, :]) + bias_ref[pl.ds(dw, NA_GROUP)].reshape(gq, sw)
                  for qs, ks, dw in addr]
        stats = []
        for s in scores:
            p = jnp.exp2(s - jnp.max(s, axis=-1, keepdims=True))
            stats.append((jnp.sum(p, axis=-1, keepdims=True), p.astype(_BF16)))
        for (qs, ks, _), (l, p) in zip(addr, stats):
            o = jnp.dot(p, v_ref[ks, :], preferred_element_type=_F32) / l
            o_ref[qs, :] = (o * z_ref[qs, :]).astype(o_ref.dtype)
        return carry

    assert (rows // NA_GROUP) % NA_BATCH == 0
    lax.fori_loop(0, rows // NA_GROUP // NA_BATCH, row_groups, 0, unroll=NA_UNROLL)


def _neighbourhood(qkv, zf, bias_tab, batch, seq, off_q, off_z):
    rows = seq // GRID_W
    assert rows >= NA_SUPER and rows % NA_GROUP == 0
    blk = lambda off: pl.BlockSpec((seq, HD), lambda b, h: (b, off // HD + h))
    return pl.pallas_call(
        functools.partial(_na_kernel, rows=rows),
        grid=(batch, A_HEADS),
        in_specs=[blk(off_q["a_q"]), blk(off_q["a_k"]), blk(off_q["a_v"]), blk(off_z["a_z"]),
                  pl.BlockSpec((None, NA_SUPER, GRID_W, NA_SUPER * GRID_W), lambda b, h: (h, 0, 0, 0))],
        out_specs=pl.BlockSpec((seq, HD), lambda b, h: (b, h)),
        out_shape=jax.ShapeDtypeStruct((batch * seq, A_W), _BF16),
        compiler_params=_cparams("parallel", "parallel"),
        name="neighbourhood_attention",
    )(qkv, qkv, qkv, zf, bias_tab)


def _gqa_kernel(q_ref, k_ref, v_ref, z_ref, o_ref):
    n_heads = q_ref.shape[1] // HD
    group = n_heads // (k_ref.shape[1] // HD)
    head_slice = lambda h: slice(h * HD, (h + 1) * HD)
    scores = lambda h: _qk_t(q_ref[:, head_slice(h)], k_ref[:, head_slice(h // group)])
    s_next = scores(0)
    for h in range(n_heads):
        s = s_next
        if h + 1 < n_heads:
            s_next = scores(h + 1)
        m = jnp.max(s, axis=-1, keepdims=True)
        p = jnp.exp2(s - m)
        l = jnp.sum(p, axis=-1, keepdims=True)
        o = jnp.dot(p.astype(_BF16), v_ref[:, head_slice(h // group)], preferred_element_type=_F32) / l
        o_ref[:, head_slice(h)] = (o * z_ref[:, head_slice(h)]).astype(o_ref.dtype)


def _gqa(qkv, zf, batch, seq, off_q, off_z):
    nq = seq // TQ
    assert off_q["b_q"] % B_QW == 0 and off_z["b_z"] % B_QW == 0 and off_q["b_k"] % B_KVW == 0
    q_spec = lambda off: pl.BlockSpec((TQ, B_QW), lambda b, i: (b * nq + i, off // B_QW))
    kv_spec = lambda off: pl.BlockSpec((seq, B_KVW), lambda b, i: (b, off // B_KVW))
    return pl.pallas_call(
        _gqa_kernel,
        grid=(batch, nq),
        in_specs=[q_spec(off_q["b_q"]), kv_spec(off_q["b_k"]), kv_spec(off_q["b_v"]), q_spec(off_z["b_z"])],
        out_specs=pl.BlockSpec((TQ, B_QW), lambda b, i: (b * nq + i, 0)),
        out_shape=jax.ShapeDtypeStruct((batch * seq, B_QW), _BF16),
        compiler_params=_cparams("parallel", "parallel"),
        name="gqa_attention",
    )(qkv, qkv, qkv, zf)


def _store_transposed(vt_ref, v_ref):
    seq, d = v_ref.shape
    for c in range(seq // KV_CHUNK):
        cs = slice(c * KV_CHUNK, (c + 1) * KV_CHUNK)
        vt_ref[:d, cs] = v_ref[cs, :].astype(_F32).T.astype(vt_ref.dtype)
    vt_ref[d:, :] = jnp.ones((vt_ref.shape[0] - d, seq), vt_ref.dtype)


class _OnlineSoftmax:
    def __init__(self, d_v, n_q):
        self.d_v = d_v
        self.m = jnp.full((1, n_q), NEG_INF, _F32)
        self.acc = jnp.zeros((d_v + ONES_ROWS, n_q), _F32)

    def update(self, s, vt_chunk):
        m_new = jnp.maximum(self.m, jnp.max(s, axis=0, keepdims=True))
        p = jnp.exp2(s - m_new).astype(_BF16)
        self.acc = (jnp.exp2(self.m - m_new) * self.acc
                    + jnp.dot(vt_chunk, p, preferred_element_type=_F32))
        self.m = m_new

    def result(self):
        return self.acc[:self.d_v] / self.acc[self.d_v:self.d_v + 1]


def _diff_kernel(slope_ref, q_ref, k_ref, v_ref, z_ref, lam_ref, g_ref, o_ref, vt_ref, pen_ref, *, lam_init):
    h, i = pl.program_id(1), pl.program_id(2)
    seq = k_ref.shape[0]
    tq, hw = q_ref.shape
    key0 = seq - tq

    @pl.when(i == 0)
    def _():
        _store_transposed(vt_ref, v_ref)
        slope = slope_ref[h] * LOG2E
        n_rows = pen_ref.shape[0]
        for y0 in range(0, n_rows, KV_CHUNK):
            size = min(KV_CHUNK, n_rows - y0)
            row_minus_col = (lax.broadcasted_iota(jnp.int32, (size, tq), 0)
                             - lax.broadcasted_iota(jnp.int32, (size, tq), 1))
            pen_ref[y0:y0 + size, :] = slope * jnp.abs(row_minus_col + (y0 - key0)).astype(_F32)

    lp = lam_ref[...]
    lam = (jnp.exp(jnp.sum(lp[0:1] * lp[1:2], axis=-1, keepdims=True))
           - jnp.exp(jnp.sum(lp[2:3] * lp[3:4], axis=-1, keepdims=True)) + lam_init)

    states = [_OnlineSoftmax(hw, tq) for _ in range(2)]
    pen0 = pl.multiple_of(key0 - i * tq, tq)
    for c in range(seq // KV_CHUNK):
        cs = slice(c * KV_CHUNK, (c + 1) * KV_CHUNK)
        pen = pen_ref[pl.ds(pen0 + c * KV_CHUNK, KV_CHUNK), :]
        for mi, state in enumerate(states):
            sl = slice(mi * HD, (mi + 1) * HD)
            state.update(_qk_t(k_ref[cs, sl], q_ref[:, sl]) - pen, vt_ref[:, cs])
    o = (states[0].result() - lam * states[1].result()).T
    ms = jnp.mean(o * o, axis=-1, keepdims=True)
    o = o * lax.rsqrt(ms + RMS_EPS) * g_ref[...] * (1.0 - lam_init)
    o_ref[...] = (o * z_ref[...]).astype(o_ref.dtype)


def _diff_attention(qkv, zf, diff_lambda, subln_g, lam_init, batch, seq, off_q, off_z):
    hw = 2 * HD
    nq = seq // TQ
    slopes = jnp.asarray(2.0 ** (-8.0 * np.arange(1, C_HEADS + 1) / C_HEADS), dtype=_F32)
    q_spec = lambda off: pl.BlockSpec((TQ, hw), lambda b, h, i, sl: (b * nq + i, off // hw + h))
    kv_spec = lambda off: pl.BlockSpec((seq, hw), lambda b, h, i, sl: (b, off // hw + h))
    return pl.pallas_call(
        functools.partial(_diff_kernel, lam_init=lam_init),
        grid_spec=pltpu.PrefetchScalarGridSpec(
            num_scalar_prefetch=1,
            grid=(batch, C_HEADS, nq),
            in_specs=[q_spec(off_q["c_q"]), kv_spec(off_q["c_k"]), kv_spec(off_q["c_v"]), q_spec(off_z["c_z"]),
                      pl.BlockSpec((4, HD), lambda b, h, i, sl: (0, 0)),
                      pl.BlockSpec((1, hw), lambda b, h, i, sl: (0, 0))],
            out_specs=pl.BlockSpec((TQ, hw), lambda b, h, i, sl: (b * nq + i, h)),
            scratch_shapes=[pltpu.VMEM((hw + ONES_ROWS, seq), _BF16),
                            pltpu.VMEM((2 * seq - TQ, TQ), _F32)],
        ),
        out_shape=jax.ShapeDtypeStruct((batch * seq, C_VW), _BF16),
        compiler_params=_cparams("parallel", "parallel", "arbitrary"),
        name="diff_attention",
    )(slopes, qkv, qkv, qkv, zf, diff_lambda.astype(_F32), subln_g.astype(_F32).reshape(1, hw))


def _dilated_group(gi, window, dil, slope, q_ref, k_ref, v_ref, og_ref, lse_ref, bias_ref):
    seq = q_ref.shape[0]
    length = seq // dil
    tq = min(D_TQ, length)
    span = min(D_SPAN, length)
    n_tiles = length // tq
    n_side = window // (2 * dil)
    half = (span - tq) // 2
    unit_pen = slope * (float(dil) * LOG2E)
    rel0 = (lax.broadcasted_iota(jnp.int32, (tq, span), 1)
            - lax.broadcasted_iota(jnp.int32, (tq, span), 0))
    for vi in range(3 if n_tiles > 1 else 1):
        rel = jnp.abs(rel0 - vi * half)
        bias_ref[vi, :tq, :span] = jnp.where(rel <= n_side, -unit_pen * rel.astype(_F32), NEG_INF)

    def rows(start, size):
        return pl.ds(start, size) if dil == 1 else pl.ds(start, size, stride=dil)

    def tile_group(gidx, carry):
        tiles = []
        for k in range(D_GROUP):
            idx = gidx * D_GROUP + k
            r = lax.div(idx, n_tiles)
            u0 = lax.rem(idx, n_tiles) * tq
            ks = jnp.clip(u0 - half, 0, length - span)
            vi = lax.div(u0 - ks, half) if n_tiles > 1 else 0
            tiles.append((rows(r + dil * u0, tq), rows(r + dil * ks, span), vi))
        scores = [_qk_t(q_ref[q_rows, :].astype(_BF16), k_ref[k_rows, :].astype(_BF16)) + bias_ref[vi, :tq, :span]
                  for q_rows, k_rows, vi in tiles]
        stats = []
        for s in scores:
            m = jnp.max(s, axis=-1, keepdims=True)
            p = jnp.exp2(s - m)
            stats.append((m, jnp.sum(p, axis=-1, keepdims=True), p.astype(_BF16)))
        for (q_rows, k_rows, _), (m, l, p) in zip(tiles, stats):
            o = jnp.dot(p, v_ref[k_rows, :].astype(_BF16), preferred_element_type=_F32) / l
            og_ref[gi, q_rows, :] = o
            lse_ref[gi, q_rows, :] = jnp.broadcast_to(m + jnp.log2(l), (tq, LANES))
        return carry

    assert (dil * n_tiles) % D_GROUP == 0
    lax.fori_loop(0, dil * n_tiles // D_GROUP, tile_group, 0, unroll=D_UNROLL)


def _dilated_kernel(slope_ref, q_ref, k_ref, v_ref, z_ref, o_ref, og_ref, lse_ref, bias_ref):
    h, g = pl.program_id(1), pl.program_id(2)
    for gi, (window, dil) in enumerate(D_PATTERNS):
        @pl.when(g == gi)
        def _(gi=gi, window=window, dil=dil):
            _dilated_group(gi, window, dil, slope_ref[gi * D_HEADS_PER_GROUP + h],
                           q_ref, k_ref, v_ref, og_ref, lse_ref, bias_ref)

    @pl.when(g == D_GROUPS - 1)
    def _():
        seq = o_ref.shape[0]
        chunk = min(512, seq)

        def merge(ci, carry):
            rs = pl.ds(pl.multiple_of(ci * chunk, chunk), chunk)
            lses = [lse_ref[gi, rs, :] for gi in range(D_GROUPS)]
            top = functools.reduce(jnp.maximum, lses)
            ws = [jnp.exp2(v - top) for v in lses]
            num = functools.reduce(jnp.add, [w * og_ref[gi, rs, :] for gi, w in enumerate(ws)])
            out = num / functools.reduce(jnp.add, ws)
            o_ref[rs, :] = (out * z_ref[rs, :]).astype(o_ref.dtype)
            return carry

        lax.fori_loop(0, seq // chunk, merge, 0)


def _dilated(zf, batch, seq, off_z):
    n = D_GROUPS * D_HEADS_PER_GROUP
    slopes = jnp.asarray(2.0 ** (-8.0 * np.arange(1, n + 1) / n), dtype=_F32)
    for window, dil in D_PATTERNS:
        assert seq % dil == 0 and (seq // dil) % min(D_TQ, seq // dil) == 0
        assert D_SPAN - D_TQ >= 2 * (window // (2 * dil)) or seq // dil <= D_SPAN
    qkv_spec = lambda off: pl.BlockSpec(
        (seq, HD), lambda b, h, g, sl: (b, off // HD + g * D_HEADS_PER_GROUP + h))
    return pl.pallas_call(
        _dilated_kernel,
        grid_spec=pltpu.PrefetchScalarGridSpec(
            num_scalar_prefetch=1,
            grid=(batch, D_HEADS_PER_GROUP, D_GROUPS),
            in_specs=[qkv_spec(off_z["d_q"]), qkv_spec(off_z["d_k"]), qkv_spec(off_z["d_v"]),
                      pl.BlockSpec((seq, HD), lambda b, h, g, sl: (b, off_z["d_z"] // HD + h))],
            out_specs=pl.BlockSpec((seq, HD), lambda b, h, g, sl: (b, h)),
            scratch_shapes=[pltpu.VMEM((D_GROUPS, seq, HD), _F32),
                            pltpu.VMEM((D_GROUPS, seq, LANES), _F32),
                            pltpu.VMEM((3, D_TQ, D_SPAN), _F32)],
        ),
        out_shape=jax.ShapeDtypeStruct((batch * seq, D_OW), _BF16),
        compiler_params=_cparams("parallel", "parallel", "arbitrary"),
        name="dilated_attention",
    )(slopes, zf, zf, zf, zf)


def _merge_kernel(x_ref, wg_ref, *refs, n_br):
    y_refs, wb_refs = refs[:n_br], refs[n_br:2 * n_br]
    o_ref, acc_ref = refs[2 * n_br:]
    br = pl.program_id(2)
    tm = x_ref.shape[0]

    @pl.when(br == 0)
    def _():
        acc_ref[...] = jnp.zeros_like(acc_ref)

    def add_branch(y_ref, wb_ref):
        for c in range(tm // ROW_CHUNK):
            rs = slice(c * ROW_CHUNK, (c + 1) * ROW_CHUNK)
            gate = _sigmoid(jnp.dot(x_ref[rs, :], wg_ref[...], preferred_element_type=_F32))
            acc_ref[rs, :] += gate * jnp.dot(y_ref[rs, :], wb_ref[...], preferred_element_type=_F32)

    for b in range(n_br):
        pl.when(br == b)(functools.partial(add_branch, y_refs[b], wb_refs[b]))

    @pl.when(br == n_br - 1)
    def _():
        o_ref[...] = acc_ref[...].astype(o_ref.dtype)


def _gated_merge(xn, w_b, layer, ys, wbs, gate_start):
    m, d = xn.shape
    n_br = len(ys)
    assert gate_start % TN == 0 and d % TN == 0
    gate_tile0 = gate_start // TN
    tiles_per_branch = d // TN
    once = dict(pipeline_mode=pl.Buffered(1))
    y_specs = [pl.BlockSpec((TM_MERGE, y.shape[1]), lambda i, j, br: (i, 0), **once) for y in ys]
    wb_specs = [pl.BlockSpec((None, w.shape[1], TN), lambda i, j, br: (layer, 0, j)) for w in wbs]
    return pl.pallas_call(
        functools.partial(_merge_kernel, n_br=n_br),
        grid=(m // TM_MERGE, d // TN, n_br),
        in_specs=[
            pl.BlockSpec((TM_MERGE, d), lambda i, j, br: (i, 0), **once),
            pl.BlockSpec((None, d, TN), lambda i, j, br: (layer, 0, gate_tile0 + br * tiles_per_branch + j)),
            *y_specs, *wb_specs,
        ],
        out_specs=pl.BlockSpec((TM_MERGE, TN), lambda i, j, br: (i, j)),
        out_shape=jax.ShapeDtypeStruct((m, d), _BF16),
        scratch_shapes=[pltpu.VMEM((TM_MERGE, TN), _F32)],
        compiler_params=_cparams("parallel", "parallel", "arbitrary"),
        name="gated_merge",
    )(xn, w_b, *ys, *wbs)


def _out_kernel(m_ref, w32_ref, x_ref, o_ref, w_ref):
    @pl.when(pl.program_id(1) == 0)
    def _():
        w_ref[...] = w32_ref[...].astype(w_ref.dtype)

    for c in range(o_ref.shape[0] // ROW_CHUNK):
        rs = slice(c * ROW_CHUNK, (c + 1) * ROW_CHUNK)
        o_ref[rs, :] = x_ref[rs, :] + jnp.dot(m_ref[rs, :], w_ref[...], preferred_element_type=_F32)


def _out_projection(merged, w_out, layer, x):
    m, d = x.shape
    return pl.pallas_call(
        _out_kernel,
        grid=(d // TN, m // TM),
        in_specs=[pl.BlockSpec((TM, d), lambda j, i: (i, 0)),
                  pl.BlockSpec((None, d, TN), lambda j, i: (layer, 0, j)),
                  pl.BlockSpec((TM, TN), lambda j, i: (i, j))],
        out_specs=pl.BlockSpec((TM, TN), lambda j, i: (i, j)),
        out_shape=jax.ShapeDtypeStruct((m, d), _F32),
        scratch_shapes=[pltpu.VMEM((d, TN), _BF16)],
        compiler_params=_cparams("parallel", "arbitrary"),
        name="out_projection",
    )(merged, w_out, x)


def kernel(x, norm_g, w_in, qk_gain, na_rel_bias, diff_lambda, diff_subln_g,
           w_branch_a, w_branch_b, w_branch_c, w_branch_d, w_out):
    batch, seq, d_model = x.shape
    depth = w_in.shape[0]
    m = batch * seq
    assert w_in.shape[2] == GATE_START + N_BRANCHES * d_model
    assert m % TM == 0 and seq % TM == 0 and seq % TQ == 0 and seq % GRID_W == 0 and d_model % TN == 0

    w_gate_b = w_in[:, :, GATE_START:].astype(_BF16)
    wbs = tuple(w.astype(_BF16) for w in (w_branch_a, w_branch_b, w_branch_c, w_branch_d))
    cos_t, sin_t = _rope_tables(seq)
    src_b, modes_b, gkeys_b, off_q = _PLAN_BF16
    src_f, modes_f, gkeys_f, off_z = _PLAN_F32

    xf = x.reshape(m, d_model)
    for l in range(depth):
        xn = _rmsnorm(xf, norm_g[l])
        qkv = _projection(xn, w_in, l, jnp.asarray(src_b), jnp.asarray(modes_b),
                          _gain_table(qk_gain[l], gkeys_b), cos_t, sin_t, _BF16, seq,
                          tuple(sorted(set(modes_b.tolist()))))
        zf = _projection(xn, w_in, l, jnp.asarray(src_f), jnp.asarray(modes_f),
                         _gain_table(qk_gain[l], gkeys_f), cos_t, sin_t, _F32, seq,
                         tuple(sorted(set(modes_f.tolist()))))
        lam_init = 0.8 - 0.6 * math.exp(-0.3 * l)
        y_a = _neighbourhood(qkv, zf, _na_bias_table(na_rel_bias[l], seq // GRID_W), batch, seq, off_q, off_z)
        y_b = _gqa(qkv, zf, batch, seq, off_q, off_z)
        y_c = _diff_attention(qkv, zf, diff_lambda[l], diff_subln_g[l], lam_init, batch, seq, off_q, off_z)
        y_d = _dilated(zf, batch, seq, off_z)
        merged = _gated_merge(xn, w_gate_b, l, (y_a, y_b, y_c, y_d), wbs, 0)
        xf = _out_projection(merged, w_out, l, xf)
    return xf.reshape(batch, seq, d_model)
```

```python
import functools
import math

import jax
import jax.numpy as jnp
import numpy as np
from jax import lax
from jax.experimental import pallas as pl
from jax.experimental.pallas import tpu as pltpu

HD = 128
GRID_W = 64
NA_ROWS, NA_COLS = 8, 16
A_HEADS = 8
B_Q_HEADS, B_KV_HEADS = 8, 2
ROPE_THETA = 10000.0
C_HEADS = 4
D_PATTERNS = ((128, 1), (512, 4), (2048, 16))
D_GROUPS, D_HEADS_PER_GROUP = 3, 4
N_BRANCHES = 4
RMS_EPS = 1e-6
NEG_INF = -1e30
LOG2E = math.log2(math.e)

A_W = A_HEADS * HD
B_QW = B_Q_HEADS * HD
B_KVW = B_KV_HEADS * HD
C_QKW = C_HEADS * 2 * HD
C_VW = C_HEADS * 2 * HD
D_QKVW = D_GROUPS * D_HEADS_PER_GROUP * HD
D_OW = D_HEADS_PER_GROUP * HD
_SEG_NAMES = ("a_q", "a_k", "a_v", "a_z", "b_q", "b_k", "b_v", "b_z",
              "c_q", "c_k", "c_v", "c_z", "d_q", "d_k", "d_v", "d_z")
_SEG_WIDTHS = (A_W, A_W, A_W, A_W, B_QW, B_KVW, B_KVW, B_QW,
               C_QKW, C_QKW, C_VW, C_VW, D_QKVW, D_QKVW, D_QKVW, D_OW)
_SEG_START = dict(zip(_SEG_NAMES, np.cumsum((0,) + _SEG_WIDTHS[:-1]).tolist()))
GATE_START = int(sum(_SEG_WIDTHS))

LANES = 128
V7X_VMEM_BYTES = 64 * 1024 * 1024
VMEM_LIMIT = V7X_VMEM_BYTES - 8 * 1024 * 1024
TM = 1024
TM_MERGE = 2048
TN = 512
ROW_CHUNK = 256
TM_NORM = 512
TQ = 256
KV_CHUNK = 512
ONES_ROWS = 16
D_TQ, D_SPAN = 256, 512
NA_GROUP, NA_SUPER = 4, 12
NA_BATCH, NA_UNROLL = 4, 1
D_GROUP, D_UNROLL = 4, 2

MODE_PLAIN, MODE_NORM, MODE_NORM_ROPE, MODE_KV_B, MODE_SILU = range(5)

_F32 = jnp.float32
_BF16 = jnp.bfloat16


def _cparams(*sem):
    return pltpu.CompilerParams(dimension_semantics=sem, vmem_limit_bytes=VMEM_LIMIT)


def _sigmoid(v):
    return 0.5 * jnp.tanh(0.5 * v) + 0.5


def _qk_t(q, k):
    return lax.dot_general(q, k, (((1,), (1,)), ((), ())), preferred_element_type=_F32)


def _rmsnorm_kernel(x_ref, g_ref, o_ref):
    x = x_ref[...]
    ms = jnp.mean(x * x, axis=-1, keepdims=True)
    o_ref[...] = (x * lax.rsqrt(ms + RMS_EPS) * g_ref[...]).astype(o_ref.dtype)


def _rmsnorm(x, g):
    m, d = x.shape
    return pl.pallas_call(
        _rmsnorm_kernel,
        grid=(m // TM_NORM,),
        in_specs=[pl.BlockSpec((TM_NORM, d), lambda i: (i, 0)),
                  pl.BlockSpec((1, d), lambda i: (0, 0))],
        out_specs=pl.BlockSpec((TM_NORM, d), lambda i: (i, 0)),
        out_shape=jax.ShapeDtypeStruct((m, d), _BF16),
        compiler_params=_cparams("parallel"),
        name="rmsnorm",
    )(x, g.reshape(1, d))


def _proj_kernel(src_ref, mode_ref, x_ref, w32_ref, g_ref, cos_ref, sin_ref, o_ref, w_ref, *, modes_used):
    del src_ref
    mode = mode_ref[pl.program_id(0)]
    tm, tn = o_ref.shape
    n_heads = tn // HD

    @pl.when(pl.program_id(1) == 0)
    def _():
        w_ref[...] = w32_ref[...].astype(w_ref.dtype)

    def norm_head(acc, h):
        blk = acc[:, h * HD:(h + 1) * HD]
        ms = jnp.mean(blk * blk, axis=-1, keepdims=True)
        return blk * lax.rsqrt(ms + RMS_EPS) * g_ref[:, h * HD:(h + 1) * HD]

    def rope(y, rs):
        even = (lax.broadcasted_iota(jnp.int32, y.shape, 1) % 2) == 0
        partner = jnp.where(even, pltpu.roll(y, HD - 1, 1), pltpu.roll(y, 1, 1))
        return y * cos_ref[rs, :] + partner * sin_ref[rs, :]

    def head_epilogue(mode_id, acc, h, rs):
        is_key_half = h < n_heads // 2
        if mode_id == MODE_NORM:
            return norm_head(acc, h)
        if mode_id == MODE_NORM_ROPE or (mode_id == MODE_KV_B and is_key_half):
            return rope(norm_head(acc, h), rs)
        return acc[:, h * HD:(h + 1) * HD]

    def run(mode_id):
        for c in range(tm // ROW_CHUNK):
            rs = slice(c * ROW_CHUNK, (c + 1) * ROW_CHUNK)
            acc = jnp.dot(x_ref[rs, :], w_ref[...], preferred_element_type=_F32)
            if mode_id == MODE_PLAIN:
                o_ref[rs, :] = acc.astype(o_ref.dtype)
            elif mode_id == MODE_SILU:
                o_ref[rs, :] = (acc * _sigmoid(acc)).astype(o_ref.dtype)
            else:
                for h in range(n_heads):
                    o_ref[rs, h * HD:(h + 1) * HD] = head_epilogue(mode_id, acc, h, rs).astype(o_ref.dtype)

    for mode_id in modes_used:
        pl.when(mode == mode_id)(functools.partial(run, mode_id))


def _projection(xn, w_in, layer, src_tiles, modes, gains, cos_t, sin_t, out_dtype, seq, modes_used):
    m, d = xn.shape
    n_t = src_tiles.shape[0]
    rope_blocks = seq // TM
    return pl.pallas_call(
        functools.partial(_proj_kernel, modes_used=modes_used),
        grid_spec=pltpu.PrefetchScalarGridSpec(
            num_scalar_prefetch=2,
            grid=(n_t, m // TM),
            in_specs=[
                pl.BlockSpec((TM, d), lambda j, i, src, md: (i, 0)),
                pl.BlockSpec((None, d, TN), lambda j, i, src, md: (layer, 0, src[j])),
                pl.BlockSpec((None, 1, TN), lambda j, i, src, md: (j, 0, 0)),
                pl.BlockSpec((TM, HD), lambda j, i, src, md: (i % rope_blocks, 0)),
                pl.BlockSpec((TM, HD), lambda j, i, src, md: (i % rope_blocks, 0)),
            ],
            out_specs=pl.BlockSpec((TM, TN), lambda j, i, src, md: (i, j)),
            scratch_shapes=[pltpu.VMEM((d, TN), _BF16)],
        ),
        out_shape=jax.ShapeDtypeStruct((m, n_t * TN), out_dtype),
        compiler_params=_cparams("parallel", "arbitrary"),
        name="projection",
    )(src_tiles, modes, xn, w_in, gains, cos_t, sin_t)


def _tile_plan(segments):
    src, modes, gain_keys, out_start = [], [], [], {}
    col = 0
    for name, mode, gain_key in segments:
        if name == "b_kv":
            start, width = _SEG_START["b_k"], 2 * B_KVW
            assert width == TN and _SEG_START["b_v"] == start + B_KVW
            out_start["b_k"], out_start["b_v"] = col, col + B_KVW
        else:
            start, width = _SEG_START[name], _SEG_WIDTHS[_SEG_NAMES.index(name)]
            out_start[name] = col
        assert start % TN == 0 and width % TN == 0, (name, start, width)
        for t in range(width // TN):
            src.append(start // TN + t)
            modes.append(mode)
            gain_keys.append(gain_key)
        col += width
    return np.asarray(src, np.int32), np.asarray(modes, np.int32), gain_keys, out_start


_Q_GAIN = HD ** -0.5 * LOG2E
_PLAN_BF16 = _tile_plan((
    ("a_q", MODE_NORM, (0, 0)), ("a_k", MODE_NORM, (0, 1)), ("a_v", MODE_PLAIN, None),
    ("b_q", MODE_NORM_ROPE, (1, 0)), ("b_kv", MODE_KV_B, (1, 1)),
    ("c_q", MODE_NORM, (2, 0)), ("c_k", MODE_NORM, (2, 1)), ("c_v", MODE_PLAIN, None)))
_PLAN_F32 = _tile_plan((
    ("a_z", MODE_SILU, None), ("b_z", MODE_SILU, None), ("c_z", MODE_SILU, None),
    ("d_q", MODE_NORM, (3, 0)), ("d_k", MODE_NORM, (3, 1)), ("d_v", MODE_PLAIN, None),
    ("d_z", MODE_SILU, None)))


def _gain_table(qk_gain, gain_keys):
    rows = []
    for key in gain_keys:
        if key is None:
            rows.append(jnp.ones((TN,), _F32))
        else:
            g = qk_gain[key[0], key[1]].astype(_F32)
            if key[1] == 0:
                g = g * _Q_GAIN
            rows.append(jnp.tile(g, TN // HD))
    return jnp.stack(rows)[:, None, :]


def _rope_tables(seq):
    t = jnp.arange(seq)
    row = (t // GRID_W).astype(_F32)
    col = (t % GRID_W).astype(_F32)
    n_pairs = HD // 4
    inv_freq = ROPE_THETA ** (-jnp.arange(n_pairs, dtype=_F32) / n_pairs)
    ang = jnp.concatenate([row[:, None] * inv_freq, col[:, None] * inv_freq], axis=-1)
    cos, sin = jnp.cos(ang), jnp.sin(ang)
    cos_t = jnp.repeat(cos, 2, axis=-1)
    sin_t = jnp.stack([-sin, sin], axis=-1).reshape(seq, HD)
    return cos_t, sin_t


def _na_window_starts(rows):
    starts = {}
    for r in range(rows):
        w0 = int(np.clip(NA_GROUP * (r // NA_GROUP) - NA_ROWS // 2, 0, rows - NA_SUPER))
        r0 = int(np.clip(r - NA_ROWS // 2, 0, rows - NA_ROWS))
        assert 0 <= r0 - w0 and r0 - w0 + NA_ROWS <= NA_SUPER and 0 <= r - w0 < NA_SUPER
        assert starts.setdefault(r - w0, r0 - w0) == r0 - w0
    return starts


def _na_bias_kernel(rb_ref, o_ref, base_ref, *, win_starts):
    h = pl.program_id(0)
    n_row_off, n_col_off = 2 * NA_ROWS - 1, 2 * NA_COLS - 1
    c = lax.broadcasted_iota(jnp.int32, (GRID_W, GRID_W), 0)
    kc = lax.broadcasted_iota(jnp.int32, (GRID_W, GRID_W), 1)
    c0 = jnp.clip(c - NA_COLS // 2, 0, GRID_W - NA_COLS)
    valid = (kc >= c0) & (kc < c0 + NA_COLS)
    col_off = kc - c + (NA_COLS - 1)
    masked = jnp.full((GRID_W, GRID_W), NEG_INF, _F32)
    for row_off in range(n_row_off):
        blk = masked
        for d in range(n_col_off):
            blk = jnp.where(col_off == d, rb_ref[(h * n_row_off + row_off) * n_col_off + d] * LOG2E, blk)
        base_ref[row_off] = jnp.where(valid, blk, NEG_INF)
    for dw in range(NA_SUPER):
        ws = win_starts.get(dw, 0)
        for i in range(NA_SUPER):
            in_window = ws <= i < ws + NA_ROWS
            o_ref[dw, :, i * GRID_W:(i + 1) * GRID_W] = base_ref[i - dw + NA_ROWS - 1] if in_window else masked


def _na_bias_table(rel_bias, rows):
    heads = rel_bias.shape[0]
    return pl.pallas_call(
        functools.partial(_na_bias_kernel, win_starts=_na_window_starts(rows)),
        grid=(heads,),
        in_specs=[pl.BlockSpec(memory_space=pltpu.SMEM)],
        out_specs=pl.BlockSpec((None, NA_SUPER, GRID_W, NA_SUPER * GRID_W), lambda h: (h, 0, 0, 0)),
        out_shape=jax.ShapeDtypeStruct((heads, NA_SUPER, GRID_W, NA_SUPER * GRID_W), _F32),
        scratch_shapes=[pltpu.VMEM((2 * NA_ROWS - 1, GRID_W, GRID_W), _F32)],
        compiler_params=_cparams("parallel"),
        name="na_bias_table",
    )(rel_bias.astype(_F32).reshape(-1))


def _na_kernel(q_ref, k_ref, v_ref, z_ref, bias_ref, o_ref, *, rows):
    gq = NA_GROUP * GRID_W
    sw = NA_SUPER * GRID_W

    def row_groups(t, carry):
        addr = []
        for k in range(NA_BATCH):
            g = t * NA_BATCH + k
            w0 = jnp.clip(NA_GROUP * g - NA_ROWS // 2, 0, rows - NA_SUPER)
            addr.append((pl.ds(pl.multiple_of(g * gq, gq), gq), pl.ds(pl.multiple_of(w0 * GRID_W, GRID_W), sw),
                         NA_GROUP * g - w0))
        scores = [_qk_t(q_ref[qs, :], k_ref[ks, :]) + bias_ref[pl.ds(dw, NA_GROUP)].reshape(gq, sw)
                  for qs, ks, dw in addr]
        stats = []
        for s in scores:
            p = jnp.exp2(s - jnp.max(s, axis=-1, keepdims=True))
            stats.append((jnp.sum(p, axis=-1, keepdims=True), p.astype(_BF16)))
        for (qs, ks, _), (l, p) in zip(addr, stats):
            o = jnp.dot(p, v_ref[ks, :], preferred_element_type=_F32) / l
            o_ref[qs, :] = (o * z_ref[qs, :]).astype(o_ref.dtype)
        return carry

    assert (rows // NA_GROUP) % NA_BATCH == 0
    lax.fori_loop(0, rows // NA_GROUP // NA_BATCH, row_groups, 0, unroll=NA_UNROLL)


def _neighbourhood(qkv, zf, bias_tab, batch, seq, off_q, off_z):
    rows = seq // GRID_W
    assert rows >= NA_SUPER and rows % NA_GROUP == 0
    blk = lambda off: pl.BlockSpec((seq, HD), lambda b, h: (b, off // HD + h))
    return pl.pallas_call(
        functools.partial(_na_kernel, rows=rows),
        grid=(batch, A_HEADS),
        in_specs=[blk(off_q["a_q"]), blk(off_q["a_k"]), blk(off_q["a_v"]), blk(off_z["a_z"]),
                  pl.BlockSpec((None, NA_SUPER, GRID_W, NA_SUPER * GRID_W), lambda b, h: (h, 0, 0, 0))],
        out_specs=pl.BlockSpec((seq, HD), lambda b, h: (b, h)),
        out_shape=jax.ShapeDtypeStruct((batch * seq, A_W), _BF16),
        compiler_params=_cparams("parallel", "parallel"),
        name="neighbourhood_attention",
    )(qkv, qkv, qkv, zf, bias_tab)


def _gqa_kernel(q_ref, k_ref, v_ref, z_ref, o_ref, vext_ref):
    n_heads = q_ref.shape[1] // HD
    n_kv = k_ref.shape[1] // HD
    group = n_heads // n_kv
    head_slice = lambda h: slice(h * HD, (h + 1) * HD)

    @pl.when(pl.program_id(1) == 0)
    def _():
        for hk in range(n_kv):
            vext_ref[:, 2 * hk * HD:(2 * hk + 1) * HD] = v_ref[:, head_slice(hk)]
            vext_ref[:, (2 * hk + 1) * HD:(2 * hk + 2) * HD] = jnp.ones((v_ref.shape[0], HD), vext_ref.dtype)

    scores = lambda h: _qk_t(q_ref[:, head_slice(h)], k_ref[:, head_slice(h // group)])
    s_next = scores(0)
    for h in range(n_heads):
        s = s_next
        if h + 1 < n_heads:
            s_next = scores(h + 1)
        p = jnp.exp2(s - jnp.max(s, axis=-1, keepdims=True)).astype(_BF16)
        hk = h // group
        acc = jnp.dot(p, vext_ref[:, 2 * hk * HD:(2 * hk + 2) * HD], preferred_element_type=_F32)
        o = acc[:, :HD] / acc[:, HD:HD + 1]
        o_ref[:, head_slice(h)] = (o * z_ref[:, head_slice(h)]).astype(o_ref.dtype)


def _gqa(qkv, zf, batch, seq, off_q, off_z):
    nq = seq // TQ
    assert off_q["b_q"] % B_QW == 0 and off_z["b_z"] % B_QW == 0 and off_q["b_k"] % B_KVW == 0
    q_spec = lambda off: pl.BlockSpec((TQ, B_QW), lambda b, i: (b * nq + i, off // B_QW))
    kv_spec = lambda off: pl.BlockSpec((seq, B_KVW), lambda b, i: (b, off // B_KVW))
    return pl.pallas_call(
        _gqa_kernel,
        grid=(batch, nq),
        in_specs=[q_spec(off_q["b_q"]), kv_spec(off_q["b_k"]), kv_spec(off_q["b_v"]), q_spec(off_z["b_z"])],
        out_specs=pl.BlockSpec((TQ, B_QW), lambda b, i: (b * nq + i, 0)),
        out_shape=jax.ShapeDtypeStruct((batch * seq, B_QW), _BF16),
        scratch_shapes=[pltpu.VMEM((seq, 2 * B_KVW), _BF16)],
        compiler_params=_cparams("parallel", "arbitrary"),
        name="gqa_attention",
    )(qkv, qkv, qkv, zf)


def _store_transposed(vt_ref, v_ref):
    seq, d = v_ref.shape
    for c in range(seq // KV_CHUNK):
        cs = slice(c * KV_CHUNK, (c + 1) * KV_CHUNK)
        vt_ref[:d, cs] = v_ref[cs, :].astype(_F32).T.astype(vt_ref.dtype)
    vt_ref[d:, :] = jnp.ones((vt_ref.shape[0] - d, seq), vt_ref.dtype)


class _OnlineSoftmax:
    def __init__(self, d_v, n_q):
        self.d_v = d_v
        self.m = jnp.full((1, n_q), NEG_INF, _F32)
        self.acc = jnp.zeros((d_v + ONES_ROWS, n_q), _F32)

    def update(self, s, vt_chunk):
        m_new = jnp.maximum(self.m, jnp.max(s, axis=0, keepdims=True))
        p = jnp.exp2(s - m_new).astype(_BF16)
        self.acc = (jnp.exp2(self.m - m_new) * self.acc
                    + jnp.dot(vt_chunk, p, preferred_element_type=_F32))
        self.m = m_new

    def result(self):
        return self.acc[:self.d_v] / self.acc[self.d_v:self.d_v + 1]


def _diff_kernel(slope_ref, q_ref, k_ref, v_ref, z_ref, lam_ref, g_ref, o_ref, vt_ref, pen_ref, *, lam_init):
    h, i = pl.program_id(1), pl.program_id(2)
    seq = k_ref.shape[0]
    tq, hw = q_ref.shape
    key0 = seq - tq

    @pl.when(i == 0)
    def _():
        _store_transposed(vt_ref, v_ref)
        slope = slope_ref[h] * LOG2E
        n_rows = pen_ref.shape[0]
        for y0 in range(0, n_rows, KV_CHUNK):
            size = min(KV_CHUNK, n_rows - y0)
            row_minus_col = (lax.broadcasted_iota(jnp.int32, (size, tq), 0)
                             - lax.broadcasted_iota(jnp.int32, (size, tq), 1))
            pen_ref[y0:y0 + size, :] = slope * jnp.abs(row_minus_col + (y0 - key0)).astype(_F32)

    lp = lam_ref[...]
    lam = (jnp.exp(jnp.sum(lp[0:1] * lp[1:2], axis=-1, keepdims=True))
           - jnp.exp(jnp.sum(lp[2:3] * lp[3:4], axis=-1, keepdims=True)) + lam_init)

    states = [_OnlineSoftmax(hw, tq) for _ in range(2)]
    pen0 = pl.multiple_of(key0 - i * tq, tq)
    for c in range(seq // KV_CHUNK):
        cs = slice(c * KV_CHUNK, (c + 1) * KV_CHUNK)
        pen = pen_ref[pl.ds(pen0 + c * KV_CHUNK, KV_CHUNK), :]
        for mi, state in enumerate(states):
            sl = slice(mi * HD, (mi + 1) * HD)
            state.update(_qk_t(k_ref[cs, sl], q_ref[:, sl]) - pen, vt_ref[:, cs])
    o = (states[0].result() - lam * states[1].result()).T
    ms = jnp.mean(o * o, axis=-1, keepdims=True)
    o = o * lax.rsqrt(ms + RMS_EPS) * g_ref[...] * (1.0 - lam_init)
    o_ref[...] = (o * z_ref[...]).astype(o_ref.dtype)


def _diff_attention(qkv, zf, diff_lambda, subln_g, lam_init, batch, seq, off_q, off_z):
    hw = 2 * HD
    nq = seq // TQ
    slopes = jnp.asarray(2.0 ** (-8.0 * np.arange(1, C_HEADS + 1) / C_HEADS), dtype=_F32)
    q_spec = lambda off: pl.BlockSpec((TQ, hw), lambda b, h, i, sl: (b * nq + i, off // hw + h))
    kv_spec = lambda off: pl.BlockSpec((seq, hw), lambda b, h, i, sl: (b, off // hw + h))
    return pl.pallas_call(
        functools.partial(_diff_kernel, lam_init=lam_init),
        grid_spec=pltpu.PrefetchScalarGridSpec(
            num_scalar_prefetch=1,
            grid=(batch, C_HEADS, nq),
            in_specs=[q_spec(off_q["c_q"]), kv_spec(off_q["c_k"]), kv_spec(off_q["c_v"]), q_spec(off_z["c_z"]),
                      pl.BlockSpec((4, HD), lambda b, h, i, sl: (0, 0)),
                      pl.BlockSpec((1, hw), lambda b, h, i, sl: (0, 0))],
            out_specs=pl.BlockSpec((TQ, hw), lambda b, h, i, sl: (b * nq + i, h)),
            scratch_shapes=[pltpu.VMEM((hw + ONES_ROWS, seq), _BF16),
                            pltpu.VMEM((2 * seq - TQ, TQ), _F32)],
        ),
        out_shape=jax.ShapeDtypeStruct((batch * seq, C_VW), _BF16),
        compiler_params=_cparams("parallel", "parallel", "arbitrary"),
        name="diff_attention",
    )(slopes, qkv, qkv, qkv, zf, diff_lambda.astype(_F32), subln_g.astype(_F32).reshape(1, hw))


def _dilated_group(gi, window, dil, slope, q_ref, k_ref, v_ref, og_ref, lse_ref, bias_ref):
    seq = q_ref.shape[0]
    length = seq // dil
    tq = min(D_TQ, length)
    span = min(D_SPAN, length)
    n_tiles = length // tq
    n_side = window // (2 * dil)
    half = (span - tq) // 2
    unit_pen = slope * (float(dil) * LOG2E)
    rel0 = (lax.broadcasted_iota(jnp.int32, (tq, span), 1)
            - lax.broadcasted_iota(jnp.int32, (tq, span), 0))
    for vi in range(3 if n_tiles > 1 else 1):
        rel = jnp.abs(rel0 - vi * half)
        bias_ref[vi, :tq, :span] = jnp.where(rel <= n_side, -unit_pen * rel.astype(_F32), NEG_INF)

    def rows(start, size):
        return pl.ds(start, size) if dil == 1 else pl.ds(start, size, stride=dil)

    def tile_group(gidx, carry):
        tiles = []
        for k in range(D_GROUP):
            idx = gidx * D_GROUP + k
            r = lax.div(idx, n_tiles)
            u0 = lax.rem(idx, n_tiles) * tq
            ks = jnp.clip(u0 - half, 0, length - span)
            vi = lax.div(u0 - ks, half) if n_tiles > 1 else 0
            tiles.append((rows(r + dil * u0, tq), rows(r + dil * ks, span), vi))
        scores = [_qk_t(q_ref[q_rows, :].astype(_BF16), k_ref[k_rows, :].astype(_BF16)) + bias_ref[vi, :tq, :span]
                  for q_rows, k_rows, vi in tiles]
        stats = []
        for s in scores:
            m = jnp.max(s, axis=-1, keepdims=True)
            p = jnp.exp2(s - m)
            stats.append((m, jnp.sum(p, axis=-1, keepdims=True), p.astype(_BF16)))
        for (q_rows, k_rows, _), (m, l, p) in zip(tiles, stats):
            o = jnp.dot(p, v_ref[k_rows, :].astype(_BF16), preferred_element_type=_F32) / l
            og_ref[gi, q_rows, :] = o
            lse_ref[gi, q_rows, :] = jnp.broadcast_to(m + jnp.log2(l), (tq, LANES))
        return carry

    assert (dil * n_tiles) % D_GROUP == 0
    lax.fori_loop(0, dil * n_tiles // D_GROUP, tile_group, 0, unroll=D_UNROLL)


def _dilated_kernel(slope_ref, q_ref, k_ref, v_ref, z_ref, o_ref, og_ref, lse_ref, bias_ref):
    h, g = pl.program_id(1), pl.program_id(2)
    for gi, (window, dil) in enumerate(D_PATTERNS):
        @pl.when(g == gi)
        def _(gi=gi, window=window, dil=dil):
            _dilated_group(gi, window, dil, slope_ref[gi * D_HEADS_PER_GROUP + h],
                           q_ref, k_ref, v_ref, og_ref, lse_ref, bias_ref)

    @pl.when(g == D_GROUPS - 1)
    def _():
        seq = o_ref.shape[0]
        chunk = min(512, seq)

        def merge(ci, carry):
            rs = pl.ds(pl.multiple_of(ci * chunk, chunk), chunk)
            lses = [lse_ref[gi, rs, :] for gi in range(D_GROUPS)]
            top = functools.reduce(jnp.maximum, lses)
            ws = [jnp.exp2(v - top) for v in lses]
            num = functools.reduce(jnp.add, [w * og_ref[gi, rs, :] for gi, w in enumerate(ws)])
            out = num / functools.reduce(jnp.add, ws)
            o_ref[rs, :] = (out * z_ref[rs, :]).astype(o_ref.dtype)
            return carry

        lax.fori_loop(0, seq // chunk, merge, 0)


def _dilated(zf, batch, seq, off_z):
    n = D_GROUPS * D_HEADS_PER_GROUP
    slopes = jnp.asarray(2.0 ** (-8.0 * np.arange(1, n + 1) / n), dtype=_F32)
    for window, dil in D_PATTERNS:
        assert seq % dil == 0 and (seq // dil) % min(D_TQ, seq // dil) == 0
        assert D_SPAN - D_TQ >= 2 * (window // (2 * dil)) or seq // dil <= D_SPAN
    qkv_spec = lambda off: pl.BlockSpec(
        (seq, HD), lambda b, h, g, sl: (b, off // HD + g * D_HEADS_PER_GROUP + h))
    return pl.pallas_call(
        _dilated_kernel,
        grid_spec=pltpu.PrefetchScalarGridSpec(
            num_scalar_prefetch=1,
            grid=(batch, D_HEADS_PER_GROUP, D_GROUPS),
            in_specs=[qkv_spec(off_z["d_q"]), qkv_spec(off_z["d_k"]), qkv_spec(off_z["d_v"]),
                      pl.BlockSpec((seq, HD), lambda b, h, g, sl: (b, off_z["d_z"] // HD + h))],
            out_specs=pl.BlockSpec((seq, HD), lambda b, h, g, sl: (b, h)),
            scratch_shapes=[pltpu.VMEM((D_GROUPS, seq, HD), _F32),
                            pltpu.VMEM((D_GROUPS, seq, LANES), _F32),
                            pltpu.VMEM((3, D_TQ, D_SPAN), _F32)],
        ),
        out_shape=jax.ShapeDtypeStruct((batch * seq, D_OW), _BF16),
        compiler_params=_cparams("parallel", "parallel", "arbitrary"),
        name="dilated_attention",
    )(slopes, zf, zf, zf, zf)


def _merge_kernel(x_ref, wg_ref, *refs, n_br):
    y_refs, wb_refs = refs[:n_br], refs[n_br:2 * n_br]
    o_ref, acc_ref = refs[2 * n_br:]
    br = pl.program_id(2)
    tm = x_ref.shape[0]

    @pl.when(br == 0)
    def _():
        acc_ref[...] = jnp.zeros_like(acc_ref)

    def add_branch(y_ref, wb_ref):
        for c in range(tm // ROW_CHUNK):
            rs = slice(c * ROW_CHUNK, (c + 1) * ROW_CHUNK)
            gate = _sigmoid(jnp.dot(x_ref[rs, :], wg_ref[...], preferred_element_type=_F32))
            acc_ref[rs, :] += gate * jnp.dot(y_ref[rs, :], wb_ref[...], preferred_element_type=_F32)

    for b in range(n_br):
        pl.when(br == b)(functools.partial(add_branch, y_refs[b], wb_refs[b]))

    @pl.when(br == n_br - 1)
    def _():
        o_ref[...] = acc_ref[...].astype(o_ref.dtype)


def _gated_merge(xn, w_b, layer, ys, wbs, gate_start):
    m, d = xn.shape
    n_br = len(ys)
    assert gate_start % TN == 0 and d % TN == 0
    gate_tile0 = gate_start // TN
    tiles_per_branch = d // TN
    once = dict(pipeline_mode=pl.Buffered(1))
    y_specs = [pl.BlockSpec((TM_MERGE, y.shape[1]), lambda i, j, br: (i, 0), **once) for y in ys]
    wb_specs = [pl.BlockSpec((None, w.shape[1], TN), lambda i, j, br: (layer, 0, j)) for w in wbs]
    return pl.pallas_call(
        functools.partial(_merge_kernel, n_br=n_br),
        grid=(m // TM_MERGE, d // TN, n_br),
        in_specs=[
            pl.BlockSpec((TM_MERGE, d), lambda i, j, br: (i, 0), **once),
            pl.BlockSpec((None, d, TN), lambda i, j, br: (layer, 0, gate_tile0 + br * tiles_per_branch + j)),
            *y_specs, *wb_specs,
        ],
        out_specs=pl.BlockSpec((TM_MERGE, TN), lambda i, j, br: (i, j)),
        out_shape=jax.ShapeDtypeStruct((m, d), _BF16),
        scratch_shapes=[pltpu.VMEM((TM_MERGE, TN), _F32)],
        compiler_params=_cparams("parallel", "parallel", "arbitrary"),
        name="gated_merge",
    )(xn, w_b, *ys, *wbs)


def _out_kernel(m_ref, w32_ref, x_ref, o_ref, w_ref):
    @pl.when(pl.program_id(1) == 0)
    def _():
        w_ref[...] = w32_ref[...].astype(w_ref.dtype)

    for c in range(o_ref.shape[0] // ROW_CHUNK):
        rs = slice(c * ROW_CHUNK, (c + 1) * ROW_CHUNK)
        o_ref[rs, :] = x_ref[rs, :] + jnp.dot(m_ref[rs, :], w_ref[...], preferred_element_type=_F32)


def _out_projection(merged, w_out, layer, x):
    m, d = x.shape
    return pl.pallas_call(
        _out_kernel,
        grid=(d // TN, m // TM),
        in_specs=[pl.BlockSpec((TM, d), lambda j, i: (i, 0)),
                  pl.BlockSpec((None, d, TN), lambda j, i: (layer, 0, j)),
                  pl.BlockSpec((TM, TN), lambda j, i: (i, j))],
        out_specs=pl.BlockSpec((TM, TN), lambda j, i: (i, j)),
        out_shape=jax.ShapeDtypeStruct((m, d), _F32),
        scratch_shapes=[pltpu.VMEM((d, TN), _BF16)],
        compiler_params=_cparams("parallel", "arbitrary"),
        name="out_projection",
    )(merged, w_out, x)


def kernel(x, norm_g, w_in, qk_gain, na_rel_bias, diff_lambda, diff_subln_g,
           w_branch_a, w_branch_b, w_branch_c, w_branch_d, w_out):
    batch, seq, d_model = x.shape
    depth = w_in.shape[0]
    m = batch * seq
    assert w_in.shape[2] == GATE_START + N_BRANCHES * d_model
    assert m % TM == 0 and seq % TM == 0 and seq % TQ == 0 and seq % GRID_W == 0 and d_model % TN == 0

    w_gate_b = w_in[:, :, GATE_START:].astype(_BF16)
    wbs = tuple(w.astype(_BF16) for w in (w_branch_a, w_branch_b, w_branch_c, w_branch_d))
    cos_t, sin_t = _rope_tables(seq)
    src_b, modes_b, gkeys_b, off_q = _PLAN_BF16
    src_f, modes_f, gkeys_f, off_z = _PLAN_F32

    xf = x.reshape(m, d_model)
    for l in range(depth):
        xn = _rmsnorm(xf, norm_g[l])
        qkv = _projection(xn, w_in, l, jnp.asarray(src_b), jnp.asarray(modes_b),
                          _gain_table(qk_gain[l], gkeys_b), cos_t, sin_t, _BF16, seq,
                          tuple(sorted(set(modes_b.tolist()))))
        zf = _projection(xn, w_in, l, jnp.asarray(src_f), jnp.asarray(modes_f),
                         _gain_table(qk_gain[l], gkeys_f), cos_t, sin_t, _F32, seq,
                         tuple(sorted(set(modes_f.tolist()))))
        lam_init = 0.8 - 0.6 * math.exp(-0.3 * l)
        y_a = _neighbourhood(qkv, zf, _na_bias_table(na_rel_bias[l], seq // GRID_W), batch, seq, off_q, off_z)
        y_b = _gqa(qkv, zf, batch, seq, off_q, off_z)
        y_c = _diff_attention(qkv, zf, diff_lambda[l], diff_subln_g[l], lam_init, batch, seq, off_q, off_z)
        y_d = _dilated(zf, batch, seq, off_z)
        merged = _gated_merge(xn, w_gate_b, l, (y_a, y_b, y_c, y_d), wbs, 0)
        xf = _out_projection(merged, w_out, l, xf)
    return xf.reshape(batch, seq, d_model)
```

```python
import functools
import math

import jax
import jax.numpy as jnp
import numpy as np
from jax import lax
from jax.experimental import pallas as pl
from jax.experimental.pallas import tpu as pltpu

HD = 128
GRID_W = 64
NA_ROWS, NA_COLS = 8, 16
A_HEADS = 8
B_Q_HEADS, B_KV_HEADS = 8, 2
ROPE_THETA = 10000.0
C_HEADS = 4
D_PATTERNS = ((128, 1), (512, 4), (2048, 16))
D_GROUPS, D_HEADS_PER_GROUP = 3, 4
N_BRANCHES = 4
RMS_EPS = 1e-6
NEG_INF = -1e30
LOG2E = math.log2(math.e)

A_W = A_HEADS * HD
B_QW = B_Q_HEADS * HD
B_KVW = B_KV_HEADS * HD
C_QKW = C_HEADS * 2 * HD
C_VW = C_HEADS * 2 * HD
D_QKVW = D_GROUPS * D_HEADS_PER_GROUP * HD
D_OW = D_HEADS_PER_GROUP * HD
_SEG_NAMES = ("a_q", "a_k", "a_v", "a_z", "b_q", "b_k", "b_v", "b_z",
              "c_q", "c_k", "c_v", "c_z", "d_q", "d_k", "d_v", "d_z")
_SEG_WIDTHS = (A_W, A_W, A_W, A_W, B_QW, B_KVW, B_KVW, B_QW,
               C_QKW, C_QKW, C_VW, C_VW, D_QKVW, D_QKVW, D_QKVW, D_OW)
_SEG_START = dict(zip(_SEG_NAMES, np.cumsum((0,) + _SEG_WIDTHS[:-1]).tolist()))
GATE_START = int(sum(_SEG_WIDTHS))

LANES = 128
V7X_VMEM_BYTES = 64 * 1024 * 1024
VMEM_LIMIT = V7X_VMEM_BYTES - 8 * 1024 * 1024
TM = 1024
TM_MERGE = 2048
TN = 512
ROW_CHUNK = 256
TM_NORM = 512
TQ = 256
KV_CHUNK = 512
ONES_ROWS = 16
D_TQ, D_SPAN = 256, 512
NA_GROUP, NA_SUPER = 4, 12
NA_BATCH, NA_UNROLL = 4, 1
D_GROUP, D_UNROLL = 4, 2

MODE_PLAIN, MODE_NORM, MODE_NORM_ROPE, MODE_KV_B, MODE_SILU = range(5)

_F32 = jnp.float32
_BF16 = jnp.bfloat16


def _cparams(*sem):
    return pltpu.CompilerParams(dimension_semantics=sem, vmem_limit_bytes=VMEM_LIMIT)


def _sigmoid(v):
    return 0.5 * jnp.tanh(0.5 * v) + 0.5


def _qk_t(q, k):
    return lax.dot_general(q, k, (((1,), (1,)), ((), ())), preferred_element_type=_F32)


def _rmsnorm_kernel(x_ref, g_ref, o_ref):
    x = x_ref[...]
    ms = jnp.mean(x * x, axis=-1, keepdims=True)
    o_ref[...] = (x * lax.rsqrt(ms + RMS_EPS) * g_ref[...]).astype(o_ref.dtype)


def _rmsnorm(x, g):
    m, d = x.shape
    return pl.pallas_call(
        _rmsnorm_kernel,
        grid=(m // TM_NORM,),
        in_specs=[pl.BlockSpec((TM_NORM, d), lambda i: (i, 0)),
                  pl.BlockSpec((1, d), lambda i: (0, 0))],
        out_specs=pl.BlockSpec((TM_NORM, d), lambda i: (i, 0)),
        out_shape=jax.ShapeDtypeStruct((m, d), _BF16),
        compiler_params=_cparams("parallel"),
        name="rmsnorm",
    )(x, g.reshape(1, d))


def _proj_kernel(src_ref, mode_ref, x_ref, w32_ref, g_ref, cos_ref, sin_ref, *refs, modes_used):
    del src_ref
    if len(refs) == 4:
        gate32_ref, o_ref, gate_ref, w_ref = refs
        gate_ref[...] = gate32_ref[...].astype(gate_ref.dtype)
    else:
        o_ref, w_ref = refs
    mode = mode_ref[pl.program_id(0)]
    tm, tn = o_ref.shape
    n_heads = tn // HD

    @pl.when(pl.program_id(1) == 0)
    def _():
        w_ref[...] = w32_ref[...].astype(w_ref.dtype)

    def norm_head(acc, h):
        blk = acc[:, h * HD:(h + 1) * HD]
        ms = jnp.mean(blk * blk, axis=-1, keepdims=True)
        return blk * lax.rsqrt(ms + RMS_EPS) * g_ref[:, h * HD:(h + 1) * HD]

    def rope(y, rs):
        even = (lax.broadcasted_iota(jnp.int32, y.shape, 1) % 2) == 0
        partner = jnp.where(even, pltpu.roll(y, HD - 1, 1), pltpu.roll(y, 1, 1))
        return y * cos_ref[rs, :] + partner * sin_ref[rs, :]

    def head_epilogue(mode_id, acc, h, rs):
        is_key_half = h < n_heads // 2
        if mode_id == MODE_NORM:
            return norm_head(acc, h)
        if mode_id == MODE_NORM_ROPE or (mode_id == MODE_KV_B and is_key_half):
            return rope(norm_head(acc, h), rs)
        return acc[:, h * HD:(h + 1) * HD]

    def run(mode_id):
        for c in range(tm // ROW_CHUNK):
            rs = slice(c * ROW_CHUNK, (c + 1) * ROW_CHUNK)
            acc = jnp.dot(x_ref[rs, :], w_ref[...], preferred_element_type=_F32)
            if mode_id == MODE_PLAIN:
                o_ref[rs, :] = acc.astype(o_ref.dtype)
            elif mode_id == MODE_SILU:
                o_ref[rs, :] = (acc * _sigmoid(acc)).astype(o_ref.dtype)
            else:
                for h in range(n_heads):
                    o_ref[rs, h * HD:(h + 1) * HD] = head_epilogue(mode_id, acc, h, rs).astype(o_ref.dtype)

    for mode_id in modes_used:
        pl.when(mode == mode_id)(functools.partial(run, mode_id))


def _projection(xn, w_in, layer, src_tiles, modes, gains, cos_t, sin_t, out_dtype, seq, modes_used,
                round_gates=False):
    m, d = xn.shape
    n_t = src_tiles.shape[0]
    n_i = m // TM
    rope_blocks = seq // TM
    in_specs = [
        pl.BlockSpec((TM, d), lambda j, i, src, md: (i, 0)),
        pl.BlockSpec((None, d, TN), lambda j, i, src, md: (layer, 0, src[j])),
        pl.BlockSpec((None, 1, TN), lambda j, i, src, md: (j, 0, 0)),
        pl.BlockSpec((TM, HD), lambda j, i, src, md: (i % rope_blocks, 0)),
        pl.BlockSpec((TM, HD), lambda j, i, src, md: (i % rope_blocks, 0)),
    ]
    out_specs = pl.BlockSpec((TM, TN), lambda j, i, src, md: (i, j))
    out_shape = jax.ShapeDtypeStruct((m, n_t * TN), out_dtype)
    operands = [src_tiles, modes, xn, w_in, gains, cos_t, sin_t]
    if round_gates:
        gate_tiles = (w_in.shape[2] - GATE_START) // TN
        row_blocks = (n_t * n_i) // gate_tiles
        assert row_blocks >= 1 and d % row_blocks == 0 and (d // row_blocks) % 16 == 0 and GATE_START % TN == 0
        rb = d // row_blocks
        step = lambda j, i: jnp.minimum(j * n_i + i, gate_tiles * row_blocks - 1)
        in_specs.append(pl.BlockSpec(
            (None, rb, TN), lambda j, i, src, md: (layer, step(j, i) % row_blocks, GATE_START // TN + step(j, i) // row_blocks)))
        out_specs = [out_specs, pl.BlockSpec((rb, TN), lambda j, i, src, md: (step(j, i) % row_blocks, step(j, i) // row_blocks))]
        out_shape = [out_shape, jax.ShapeDtypeStruct((d, gate_tiles * TN), _BF16)]
        operands.append(w_in)
    return pl.pallas_call(
        functools.partial(_proj_kernel, modes_used=modes_used),
        grid_spec=pltpu.PrefetchScalarGridSpec(
            num_scalar_prefetch=2,
            grid=(n_t, n_i),
            in_specs=in_specs,
            out_specs=out_specs,
            scratch_shapes=[pltpu.VMEM((d, TN), _BF16)],
        ),
        out_shape=out_shape,
        compiler_params=_cparams("arbitrary", "arbitrary"),
        name="projection",
    )(*operands)


def _tile_plan(segments):
    src, modes, gain_keys, out_start = [], [], [], {}
    col = 0
    for name, mode, gain_key in segments:
        if name == "b_kv":
            start, width = _SEG_START["b_k"], 2 * B_KVW
            assert width == TN and _SEG_START["b_v"] == start + B_KVW
            out_start["b_k"], out_start["b_v"] = col, col + B_KVW
        else:
            start, width = _SEG_START[name], _SEG_WIDTHS[_SEG_NAMES.index(name)]
            out_start[name] = col
        assert start % TN == 0 and width % TN == 0, (name, start, width)
        for t in range(width // TN):
            src.append(start // TN + t)
            modes.append(mode)
            gain_keys.append(gain_key)
        col += width
    return np.asarray(src, np.int32), np.asarray(modes, np.int32), gain_keys, out_start


_Q_GAIN = HD ** -0.5 * LOG2E
_PLAN_BF16 = _tile_plan((
    ("a_q", MODE_NORM, (0, 0)), ("a_k", MODE_NORM, (0, 1)), ("a_v", MODE_PLAIN, None),
    ("b_q", MODE_NORM_ROPE, (1, 0)), ("b_kv", MODE_KV_B, (1, 1)),
    ("c_q", MODE_NORM, (2, 0)), ("c_k", MODE_NORM, (2, 1)), ("c_v", MODE_PLAIN, None)))
_PLAN_F32 = _tile_plan((
    ("a_z", MODE_SILU, None), ("b_z", MODE_SILU, None), ("c_z", MODE_SILU, None),
    ("d_q", MODE_NORM, (3, 0)), ("d_k", MODE_NORM, (3, 1)), ("d_v", MODE_PLAIN, None),
    ("d_z", MODE_SILU, None)))


def _gain_table(qk_gain, gain_keys):
    rows = []
    for key in gain_keys:
        if key is None:
            rows.append(jnp.ones((TN,), _F32))
        else:
            g = qk_gain[key[0], key[1]].astype(_F32)
            if key[1] == 0:
                g = g * _Q_GAIN
            rows.append(jnp.tile(g, TN // HD))
    return jnp.stack(rows)[:, None, :]


def _rope_tables(seq):
    t = jnp.arange(seq)
    row = (t // GRID_W).astype(_F32)
    col = (t % GRID_W).astype(_F32)
    n_pairs = HD // 4
    inv_freq = ROPE_THETA ** (-jnp.arange(n_pairs, dtype=_F32) / n_pairs)
    ang = jnp.concatenate([row[:, None] * inv_freq, col[:, None] * inv_freq], axis=-1)
    cos, sin = jnp.cos(ang), jnp.sin(ang)
    cos_t = jnp.repeat(cos, 2, axis=-1)
    sin_t = jnp.stack([-sin, sin], axis=-1).reshape(seq, HD)
    return cos_t, sin_t


def _na_window_starts(rows):
    starts = {}
    for r in range(rows):
        w0 = int(np.clip(NA_GROUP * (r // NA_GROUP) - NA_ROWS // 2, 0, rows - NA_SUPER))
        r0 = int(np.clip(r - NA_ROWS // 2, 0, rows - NA_ROWS))
        assert 0 <= r0 - w0 and r0 - w0 + NA_ROWS <= NA_SUPER and 0 <= r - w0 < NA_SUPER
        assert starts.setdefault(r - w0, r0 - w0) == r0 - w0
    return starts


def _na_bias_kernel(rb_ref, o_ref, base_ref, *, win_starts):
    h = pl.program_id(0)
    n_row_off, n_col_off = 2 * NA_ROWS - 1, 2 * NA_COLS - 1
    c = lax.broadcasted_iota(jnp.int32, (GRID_W, GRID_W), 0)
    kc = lax.broadcasted_iota(jnp.int32, (GRID_W, GRID_W), 1)
    c0 = jnp.clip(c - NA_COLS // 2, 0, GRID_W - NA_COLS)
    valid = (kc >= c0) & (kc < c0 + NA_COLS)
    col_off = kc - c + (NA_COLS - 1)
    masked = jnp.full((GRID_W, GRID_W), NEG_INF, _F32)
    for row_off in range(n_row_off):
        blk = masked
        for d in range(n_col_off):
            blk = jnp.where(col_off == d, rb_ref[(h * n_row_off + row_off) * n_col_off + d] * LOG2E, blk)
        base_ref[row_off] = jnp.where(valid, blk, NEG_INF)
    for dw in range(NA_SUPER):
        ws = win_starts.get(dw, 0)
        for i in range(NA_SUPER):
            in_window = ws <= i < ws + NA_ROWS
            o_ref[dw, :, i * GRID_W:(i + 1) * GRID_W] = base_ref[i - dw + NA_ROWS - 1] if in_window else masked


def _na_bias_table(rel_bias, rows):
    heads = rel_bias.shape[0]
    return pl.pallas_call(
        functools.partial(_na_bias_kernel, win_starts=_na_window_starts(rows)),
        grid=(heads,),
        in_specs=[pl.BlockSpec(memory_space=pltpu.SMEM)],
        out_specs=pl.BlockSpec((None, NA_SUPER, GRID_W, NA_SUPER * GRID_W), lambda h: (h, 0, 0, 0)),
        out_shape=jax.ShapeDtypeStruct((heads, NA_SUPER, GRID_W, NA_SUPER * GRID_W), _F32),
        scratch_shapes=[pltpu.VMEM((2 * NA_ROWS - 1, GRID_W, GRID_W), _F32)],
        compiler_params=_cparams("parallel"),
        name="na_bias_table",
    )(rel_bias.astype(_F32).reshape(-1))


def _na_kernel(q_ref, k_ref, v_ref, z_ref, bias_ref, o_ref, *, rows):
    gq = NA_GROUP * GRID_W
    sw = NA_SUPER * GRID_W

    def row_groups(t, carry):
        addr = []
        for k in range(NA_BATCH):
            g = t * NA_BATCH + k
            w0 = jnp.clip(NA_GROUP * g - NA_ROWS // 2, 0, rows - NA_SUPER)
            addr.append((pl.ds(pl.multiple_of(g * gq, gq), gq), pl.ds(pl.multiple_of(w0 * GRID_W, GRID_W), sw),
                         NA_GROUP * g - w0))
        scores = [_qk_t(q_ref[qs, :], k_ref[ks, :]) + bias_ref[pl.ds(dw, NA_GROUP)].reshape(gq, sw)
                  for qs, ks, dw in addr]
        stats = []
        for s in scores:
            p = jnp.exp2(s - jnp.max(s, axis=-1, keepdims=True))
            stats.append((jnp.sum(p, axis=-1, keepdims=True), p.astype(_BF16)))
        for (qs, ks, _), (l, p) in zip(addr, stats):
            o = jnp.dot(p, v_ref[ks, :], preferred_element_type=_F32) / l
            o_ref[qs, :] = (o * z_ref[qs, :]).astype(o_ref.dtype)
        return carry

    assert (rows // NA_GROUP) % NA_BATCH == 0
    lax.fori_loop(0, rows // NA_GROUP // NA_BATCH, row_groups, 0, unroll=NA_UNROLL)


def _neighbourhood(qkv, zf, bias_tab, batch, seq, off_q, off_z):
    rows = seq // GRID_W
    assert rows >= NA_SUPER and rows % NA_GROUP == 0
    blk = lambda off: pl.BlockSpec((seq, HD), lambda b, h: (b, off // HD + h))
    return pl.pallas_call(
        functools.partial(_na_kernel, rows=rows),
        grid=(batch, A_HEADS),
        in_specs=[blk(off_q["a_q"]), blk(off_q["a_k"]), blk(off_q["a_v"]), blk(off_z["a_z"]),
                  pl.BlockSpec((None, NA_SUPER, GRID_W, NA_SUPER * GRID_W), lambda b, h: (h, 0, 0, 0))],
        out_specs=pl.BlockSpec((seq, HD), lambda b, h: (b, h)),
        out_shape=jax.ShapeDtypeStruct((batch * seq, A_W), _BF16),
        compiler_params=_cparams("parallel", "parallel"),
        name="neighbourhood_attention",
    )(qkv, qkv, qkv, zf, bias_tab)


def _gqa_kernel(q_ref, k_ref, v_ref, z_ref, o_ref, vext_ref):
    n_heads = q_ref.shape[1] // HD
    n_kv = k_ref.shape[1] // HD
    group = n_heads // n_kv
    head_slice = lambda h: slice(h * HD, (h + 1) * HD)

    @pl.when(pl.program_id(1) == 0)
    def _():
        for hk in range(n_kv):
            vext_ref[:, 2 * hk * HD:(2 * hk + 1) * HD] = v_ref[:, head_slice(hk)]
            vext_ref[:, (2 * hk + 1) * HD:(2 * hk + 2) * HD] = jnp.ones((v_ref.shape[0], HD), vext_ref.dtype)

    scores = lambda h: _qk_t(q_ref[:, head_slice(h)], k_ref[:, head_slice(h // group)])
    s_next = scores(0)
    for h in range(n_heads):
        s = s_next
        if h + 1 < n_heads:
            s_next = scores(h + 1)
        p = jnp.exp2(s - jnp.max(s, axis=-1, keepdims=True)).astype(_BF16)
        hk = h // group
        acc = jnp.dot(p, vext_ref[:, 2 * hk * HD:(2 * hk + 2) * HD], preferred_element_type=_F32)
        o = acc[:, :HD] / acc[:, HD:HD + 1]
        o_ref[:, head_slice(h)] = (o * z_ref[:, head_slice(h)]).astype(o_ref.dtype)


def _gqa(qkv, zf, batch, seq, off_q, off_z):
    nq = seq // TQ
    assert off_q["b_q"] % B_QW == 0 and off_z["b_z"] % B_QW == 0 and off_q["b_k"] % B_KVW == 0
    q_spec = lambda off: pl.BlockSpec((TQ, B_QW), lambda b, i: (b * nq + i, off // B_QW))
    kv_spec = lambda off: pl.BlockSpec((seq, B_KVW), lambda b, i: (b, off // B_KVW))
    return pl.pallas_call(
        _gqa_kernel,
        grid=(batch, nq),
        in_specs=[q_spec(off_q["b_q"]), kv_spec(off_q["b_k"]), kv_spec(off_q["b_v"]), q_spec(off_z["b_z"])],
        out_specs=pl.BlockSpec((TQ, B_QW), lambda b, i: (b * nq + i, 0)),
        out_shape=jax.ShapeDtypeStruct((batch * seq, B_QW), _BF16),
        scratch_shapes=[pltpu.VMEM((seq, 2 * B_KVW), _BF16)],
        compiler_params=_cparams("parallel", "arbitrary"),
        name="gqa_attention",
    )(qkv, qkv, qkv, zf)


def _store_transposed(vt_ref, v_ref):
    seq, d = v_ref.shape
    for c in range(seq // KV_CHUNK):
        cs = slice(c * KV_CHUNK, (c + 1) * KV_CHUNK)
        vt_ref[:d, cs] = v_ref[cs, :].astype(_F32).T.astype(vt_ref.dtype)
    vt_ref[d:, :] = jnp.ones((vt_ref.shape[0] - d, seq), vt_ref.dtype)


class _OnlineSoftmax:
    def __init__(self, d_v, n_q):
        self.d_v = d_v
        self.m = jnp.full((1, n_q), NEG_INF, _F32)
        self.acc = jnp.zeros((d_v + ONES_ROWS, n_q), _F32)

    def update(self, s, vt_chunk):
        m_new = jnp.maximum(self.m, jnp.max(s, axis=0, keepdims=True))
        p = jnp.exp2(s - m_new).astype(_BF16)
        self.acc = (jnp.exp2(self.m - m_new) * self.acc
                    + jnp.dot(vt_chunk, p, preferred_element_type=_F32))
        self.m = m_new

    def result(self):
        return self.acc[:self.d_v] / self.acc[self.d_v:self.d_v + 1]


def _diff_kernel(slope_ref, q_ref, k_ref, v_ref, z_ref, lam_ref, g_ref, o_ref, vt_ref, pen_ref, *, lam_init):
    h, i = pl.program_id(1), pl.program_id(2)
    seq = k_ref.shape[0]
    tq, hw = q_ref.shape
    key0 = seq - tq

    @pl.when(i == 0)
    def _():
        _store_transposed(vt_ref, v_ref)
        slope = slope_ref[h] * LOG2E
        n_rows = pen_ref.shape[0]
        for y0 in range(0, n_rows, KV_CHUNK):
            size = min(KV_CHUNK, n_rows - y0)
            row_minus_col = (lax.broadcasted_iota(jnp.int32, (size, tq), 0)
                             - lax.broadcasted_iota(jnp.int32, (size, tq), 1))
            pen_ref[y0:y0 + size, :] = slope * jnp.abs(row_minus_col + (y0 - key0)).astype(_F32)

    lp = lam_ref[...]
    lam = (jnp.exp(jnp.sum(lp[0:1] * lp[1:2], axis=-1, keepdims=True))
           - jnp.exp(jnp.sum(lp[2:3] * lp[3:4], axis=-1, keepdims=True)) + lam_init)

    states = [_OnlineSoftmax(hw, tq) for _ in range(2)]
    pen0 = pl.multiple_of(key0 - i * tq, tq)
    for c in range(seq // KV_CHUNK):
        cs = slice(c * KV_CHUNK, (c + 1) * KV_CHUNK)
        pen = pen_ref[pl.ds(pen0 + c * KV_CHUNK, KV_CHUNK), :]
        for mi, state in enumerate(states):
            sl = slice(mi * HD, (mi + 1) * HD)
            state.update(_qk_t(k_ref[cs, sl], q_ref[:, sl]) - pen, vt_ref[:, cs])
    o = (states[0].result() - lam * states[1].result()).T
    ms = jnp.mean(o * o, axis=-1, keepdims=True)
    o = o * lax.rsqrt(ms + RMS_EPS) * g_ref[...] * (1.0 - lam_init)
    o_ref[...] = (o * z_ref[...]).astype(o_ref.dtype)


def _diff_attention(qkv, zf, diff_lambda, subln_g, lam_init, batch, seq, off_q, off_z):
    hw = 2 * HD
    nq = seq // TQ
    slopes = jnp.asarray(2.0 ** (-8.0 * np.arange(1, C_HEADS + 1) / C_HEADS), dtype=_F32)
    q_spec = lambda off: pl.BlockSpec((TQ, hw), lambda b, h, i, sl: (b * nq + i, off // hw + h))
    kv_spec = lambda off: pl.BlockSpec((seq, hw), lambda b, h, i, sl: (b, off // hw + h))
    return pl.pallas_call(
        functools.partial(_diff_kernel, lam_init=lam_init),
        grid_spec=pltpu.PrefetchScalarGridSpec(
            num_scalar_prefetch=1,
            grid=(batch, C_HEADS, nq),
            in_specs=[q_spec(off_q["c_q"]), kv_spec(off_q["c_k"]), kv_spec(off_q["c_v"]), q_spec(off_z["c_z"]),
                      pl.BlockSpec((4, HD), lambda b, h, i, sl: (0, 0)),
                      pl.BlockSpec((1, hw), lambda b, h, i, sl: (0, 0))],
            out_specs=pl.BlockSpec((TQ, hw), lambda b, h, i, sl: (b * nq + i, h)),
            scratch_shapes=[pltpu.VMEM((hw + ONES_ROWS, seq), _BF16),
                            pltpu.VMEM((2 * seq - TQ, TQ), _F32)],
        ),
        out_shape=jax.ShapeDtypeStruct((batch * seq, C_VW), _BF16),
        compiler_params=_cparams("parallel", "parallel", "arbitrary"),
        name="diff_attention",
    )(slopes, qkv, qkv, qkv, zf, diff_lambda.astype(_F32), subln_g.astype(_F32).reshape(1, hw))


def _dilated_group(gi, window, dil, slope, q_ref, k_ref, v_ref, og_ref, lse_ref, bias_ref):
    seq = q_ref.shape[0]
    length = seq // dil
    tq = min(D_TQ, length)
    span = min(D_SPAN, length)
    n_tiles = length // tq
    n_side = window // (2 * dil)
    half = (span - tq) // 2
    unit_pen = slope * (float(dil) * LOG2E)
    rel0 = (lax.broadcasted_iota(jnp.int32, (tq, span), 1)
            - lax.broadcasted_iota(jnp.int32, (tq, span), 0))
    for vi in range(3 if n_tiles > 1 else 1):
        rel = jnp.abs(rel0 - vi * half)
        bias_ref[vi, :tq, :span] = jnp.where(rel <= n_side, -unit_pen * rel.astype(_F32), NEG_INF)

    def rows(start, size):
        return pl.ds(start, size) if dil == 1 else pl.ds(start, size, stride=dil)

    def tile_group(gidx, carry):
        tiles = []
        for k in range(D_GROUP):
            idx = gidx * D_GROUP + k
            r = lax.div(idx, n_tiles)
            u0 = lax.rem(idx, n_tiles) * tq
            ks = jnp.clip(u0 - half, 0, length - span)
            vi = lax.div(u0 - ks, half) if n_tiles > 1 else 0
            tiles.append((rows(r + dil * u0, tq), rows(r + dil * ks, span), vi))
        scores = [_qk_t(q_ref[q_rows, :].astype(_BF16), k_ref[k_rows, :].astype(_BF16)) + bias_ref[vi, :tq, :span]
                  for q_rows, k_rows, vi in tiles]
        stats = []
        for s in scores:
            m = jnp.max(s, axis=-1, keepdims=True)
            p = jnp.exp2(s - m)
            stats.append((m, jnp.sum(p, axis=-1, keepdims=True), p.astype(_BF16)))
        for (q_rows, k_rows, _), (m, l, p) in zip(tiles, stats):
            o = jnp.dot(p, v_ref[k_rows, :].astype(_BF16), preferred_element_type=_F32) / l
            og_ref[gi, q_rows, :] = o
            lse_ref[gi, q_rows, :] = jnp.broadcast_to(m + jnp.log2(l), (tq, LANES))
        return carry

    assert (dil * n_tiles) % D_GROUP == 0
    lax.fori_loop(0, dil * n_tiles // D_GROUP, tile_group, 0, unroll=D_UNROLL)


def _dilated_kernel(slope_ref, q_ref, k_ref, v_ref, z_ref, o_ref, og_ref, lse_ref, bias_ref):
    h, g = pl.program_id(1), pl.program_id(2)
    for gi, (window, dil) in enumerate(D_PATTERNS):
        @pl.when(g == gi)
        def _(gi=gi, window=window, dil=dil):
            _dilated_group(gi, window, dil, slope_ref[gi * D_HEADS_PER_GROUP + h],
                           q_ref, k_ref, v_ref, og_ref, lse_ref, bias_ref)

    @pl.when(g == D_GROUPS - 1)
    def _():
        seq = o_ref.shape[0]
        chunk = min(512, seq)

        def merge(ci, carry):
            rs = pl.ds(pl.multiple_of(ci * chunk, chunk), chunk)
            lses = [lse_ref[gi, rs, :] for gi in range(D_GROUPS)]
            top = functools.reduce(jnp.maximum, lses)
            ws = [jnp.exp2(v - top) for v in lses]
            num = functools.reduce(jnp.add, [w * og_ref[gi, rs, :] for gi, w in enumerate(ws)])
            out = num / functools.reduce(jnp.add, ws)
            o_ref[rs, :] = (out * z_ref[rs, :]).astype(o_ref.dtype)
            return carry

        lax.fori_loop(0, seq // chunk, merge, 0)


def _dilated(zf, batch, seq, off_z):
    n = D_GROUPS * D_HEADS_PER_GROUP
    slopes = jnp.asarray(2.0 ** (-8.0 * np.arange(1, n + 1) / n), dtype=_F32)
    for window, dil in D_PATTERNS:
        assert seq % dil == 0 and (seq // dil) % min(D_TQ, seq // dil) == 0
        assert D_SPAN - D_TQ >= 2 * (window // (2 * dil)) or seq // dil <= D_SPAN
    qkv_spec = lambda off: pl.BlockSpec(
        (seq, HD), lambda b, h, g, sl: (b, off // HD + g * D_HEADS_PER_GROUP + h))
    return pl.pallas_call(
        _dilated_kernel,
        grid_spec=pltpu.PrefetchScalarGridSpec(
            num_scalar_prefetch=1,
            grid=(batch, D_HEADS_PER_GROUP, D_GROUPS),
            in_specs=[qkv_spec(off_z["d_q"]), qkv_spec(off_z["d_k"]), qkv_spec(off_z["d_v"]),
                      pl.BlockSpec((seq, HD), lambda b, h, g, sl: (b, off_z["d_z"] // HD + h))],
            out_specs=pl.BlockSpec((seq, HD), lambda b, h, g, sl: (b, h)),
            scratch_shapes=[pltpu.VMEM((D_GROUPS, seq, HD), _F32),
                            pltpu.VMEM((D_GROUPS, seq, LANES), _F32),
                            pltpu.VMEM((3, D_TQ, D_SPAN), _F32)],
        ),
        out_shape=jax.ShapeDtypeStruct((batch * seq, D_OW), _BF16),
        compiler_params=_cparams("parallel", "parallel", "arbitrary"),
        name="dilated_attention",
    )(slopes, zf, zf, zf, zf)


def _merge_kernel(x_ref, wg_ref, *refs, n_br):
    y_refs, wb_refs = refs[:n_br], refs[n_br:2 * n_br]
    o_ref, acc_ref = refs[2 * n_br:]
    br = pl.program_id(2)
    tm = x_ref.shape[0]

    @pl.when(br == 0)
    def _():
        acc_ref[...] = jnp.zeros_like(acc_ref)

    def add_branch(y_ref, wb_ref):
        for c in range(tm // ROW_CHUNK):
            rs = slice(c * ROW_CHUNK, (c + 1) * ROW_CHUNK)
            gate = _sigmoid(jnp.dot(x_ref[rs, :], wg_ref[...], preferred_element_type=_F32))
            acc_ref[rs, :] += gate * jnp.dot(y_ref[rs, :], wb_ref[...], preferred_element_type=_F32)

    for b in range(n_br):
        pl.when(br == b)(functools.partial(add_branch, y_refs[b], wb_refs[b]))

    @pl.when(br == n_br - 1)
    def _():
        o_ref[...] = acc_ref[...].astype(o_ref.dtype)


def _gated_merge(xn, w_gate, layer, ys, wbs):
    m, d = xn.shape
    n_br = len(ys)
    assert d % TN == 0 and w_gate.shape == (d, n_br * d)
    tiles_per_branch = d // TN
    once = dict(pipeline_mode=pl.Buffered(1))
    y_specs = [pl.BlockSpec((TM_MERGE, y.shape[1]), lambda i, j, br: (i, 0), **once) for y in ys]
    wb_specs = [pl.BlockSpec((None, w.shape[1], TN), lambda i, j, br: (layer, 0, j)) for w in wbs]
    return pl.pallas_call(
        functools.partial(_merge_kernel, n_br=n_br),
        grid=(m // TM_MERGE, d // TN, n_br),
        in_specs=[
            pl.BlockSpec((TM_MERGE, d), lambda i, j, br: (i, 0), **once),
            pl.BlockSpec((d, TN), lambda i, j, br: (0, br * tiles_per_branch + j)),
            *y_specs, *wb_specs,
        ],
        out_specs=pl.BlockSpec((TM_MERGE, TN), lambda i, j, br: (i, j)),
        out_shape=jax.ShapeDtypeStruct((m, d), _BF16),
        scratch_shapes=[pltpu.VMEM((TM_MERGE, TN), _F32)],
        compiler_params=_cparams("parallel", "parallel", "arbitrary"),
        name="gated_merge",
    )(xn, w_gate, *ys, *wbs)


def _out_kernel(m_ref, w32_ref, x_ref, o_ref, w_ref):
    @pl.when(pl.program_id(1) == 0)
    def _():
        w_ref[...] = w32_ref[...].astype(w_ref.dtype)

    for c in range(o_ref.shape[0] // ROW_CHUNK):
        rs = slice(c * ROW_CHUNK, (c + 1) * ROW_CHUNK)
        o_ref[rs, :] = x_ref[rs, :] + jnp.dot(m_ref[rs, :], w_ref[...], preferred_element_type=_F32)


def _out_projection(merged, w_out, layer, x):
    m, d = x.shape
    return pl.pallas_call(
        _out_kernel,
        grid=(d // TN, m // TM),
        in_specs=[pl.BlockSpec((TM, d), lambda j, i: (i, 0)),
                  pl.BlockSpec((None, d, TN), lambda j, i: (layer, 0, j)),
                  pl.BlockSpec((TM, TN), lambda j, i: (i, j))],
        out_specs=pl.BlockSpec((TM, TN), lambda j, i: (i, j)),
        out_shape=jax.ShapeDtypeStruct((m, d), _F32),
        scratch_shapes=[pltpu.VMEM((d, TN), _BF16)],
        compiler_params=_cparams("parallel", "arbitrary"),
        name="out_projection",
    )(merged, w_out, x)


def kernel(x, norm_g, w_in, qk_gain, na_rel_bias, diff_lambda, diff_subln_g,
           w_branch_a, w_branch_b, w_branch_c, w_branch_d, w_out):
    batch, seq, d_model = x.shape
    depth = w_in.shape[0]
    m = batch * seq
    assert w_in.shape[2] == GATE_START + N_BRANCHES * d_model
    assert m % TM == 0 and seq % TM == 0 and seq % TQ == 0 and seq % GRID_W == 0 and d_model % TN == 0

    wbs = tuple(w.astype(_BF16) for w in (w_branch_a, w_branch_b, w_branch_c, w_branch_d))
    cos_t, sin_t = _rope_tables(seq)
    src_b, modes_b, gkeys_b, off_q = _PLAN_BF16
    src_f, modes_f, gkeys_f, off_z = _PLAN_F32

    xf = x.reshape(m, d_model)
    for l in range(depth):
        xn = _rmsnorm(xf, norm_g[l])
        qkv = _projection(xn, w_in, l, jnp.asarray(src_b), jnp.asarray(modes_b),
                          _gain_table(qk_gain[l], gkeys_b), cos_t, sin_t, _BF16, seq,
                          tuple(sorted(set(modes_b.tolist()))))
        zf, w_gate_b = _projection(xn, w_in, l, jnp.asarray(src_f), jnp.asarray(modes_f),
                                   _gain_table(qk_gain[l], gkeys_f), cos_t, sin_t, _F32, seq,
                                   tuple(sorted(set(modes_f.tolist()))), round_gates=True)
        lam_init = 0.8 - 0.6 * math.exp(-0.3 * l)
        y_a = _neighbourhood(qkv, zf, _na_bias_table(na_rel_bias[l], seq // GRID_W), batch, seq, off_q, off_z)
        y_b = _gqa(qkv, zf, batch, seq, off_q, off_z)
        y_c = _diff_attention(qkv, zf, diff_lambda[l], diff_subln_g[l], lam_init, batch, seq, off_q, off_z)
        y_d = _dilated(zf, batch, seq, off_z)
        merged = _gated_merge(xn, w_gate_b, l, (y_a, y_b, y_c, y_d), wbs)
        xf = _out_projection(merged, w_out, l, xf)
    return xf.reshape(batch, seq, d_model)
```

```python
import functools
import math

import jax
import jax.numpy as jnp
import numpy as np
from jax import lax
from jax.experimental import pallas as pl
from jax.experimental.pallas import tpu as pltpu

HD = 128
GRID_W = 64
NA_ROWS, NA_COLS = 8, 16
A_HEADS = 8
B_Q_HEADS, B_KV_HEADS = 8, 2
ROPE_THETA = 10000.0
C_HEADS = 4
D_PATTERNS = ((128, 1), (512, 4), (2048, 16))
D_GROUPS, D_HEADS_PER_GROUP = 3, 4
N_BRANCHES = 4
RMS_EPS = 1e-6
NEG_INF = -1e30
LOG2E = math.log2(math.e)

A_W = A_HEADS * HD
B_QW = B_Q_HEADS * HD
B_KVW = B_KV_HEADS * HD
C_QKW = C_HEADS * 2 * HD
C_VW = C_HEADS * 2 * HD
D_QKVW = D_GROUPS * D_HEADS_PER_GROUP * HD
D_OW = D_HEADS_PER_GROUP * HD
_SEG_NAMES = ("a_q", "a_k", "a_v", "a_z", "b_q", "b_k", "b_v", "b_z",
              "c_q", "c_k", "c_v", "c_z", "d_q", "d_k", "d_v", "d_z")
_SEG_WIDTHS = (A_W, A_W, A_W, A_W, B_QW, B_KVW, B_KVW, B_QW,
               C_QKW, C_QKW, C_VW, C_VW, D_QKVW, D_QKVW, D_QKVW, D_OW)
_SEG_START = dict(zip(_SEG_NAMES, np.cumsum((0,) + _SEG_WIDTHS[:-1]).tolist()))
GATE_START = int(sum(_SEG_WIDTHS))

LANES = 128
V7X_VMEM_BYTES = 64 * 1024 * 1024
VMEM_LIMIT = V7X_VMEM_BYTES - 8 * 1024 * 1024
TM = 1024
TM_MERGE = 2048
TN = 512
ROW_CHUNK = 256
TM_NORM = 512
TQ = 256
TQ_GQA = 512
KV_CHUNK = 512
ONES_ROWS = 16
D_TQ, D_SPAN = 256, 512
NA_GROUP, NA_SUPER = 4, 12
NA_BATCH, NA_UNROLL = 4, 1
D_GROUP, D_UNROLL = 4, 2

MODE_PLAIN, MODE_NORM, MODE_NORM_ROPE, MODE_KV_B, MODE_SILU = range(5)

_F32 = jnp.float32
_BF16 = jnp.bfloat16


def _cparams(*sem):
    return pltpu.CompilerParams(dimension_semantics=sem, vmem_limit_bytes=VMEM_LIMIT)


def _sigmoid(v):
    return 0.5 * jnp.tanh(0.5 * v) + 0.5


def _qk_t(q, k):
    return lax.dot_general(q, k, (((1,), (1,)), ((), ())), preferred_element_type=_F32)


def _rmsnorm_kernel(x_ref, g_ref, o_ref):
    x = x_ref[...]
    ms = jnp.mean(x * x, axis=-1, keepdims=True)
    o_ref[...] = (x * lax.rsqrt(ms + RMS_EPS) * g_ref[...]).astype(o_ref.dtype)


def _rmsnorm(x, g):
    m, d = x.shape
    return pl.pallas_call(
        _rmsnorm_kernel,
        grid=(m // TM_NORM,),
        in_specs=[pl.BlockSpec((TM_NORM, d), lambda i: (i, 0)),
                  pl.BlockSpec((1, d), lambda i: (0, 0))],
        out_specs=pl.BlockSpec((TM_NORM, d), lambda i: (i, 0)),
        out_shape=jax.ShapeDtypeStruct((m, d), _BF16),
        compiler_params=_cparams("parallel"),
        name="rmsnorm",
    )(x, g.reshape(1, d))


def _proj_kernel(src_ref, mode_ref, x_ref, w32_ref, g_ref, cos_ref, sin_ref, *refs, modes_used):
    del src_ref
    if len(refs) == 4:
        gate32_ref, o_ref, gate_ref, w_ref = refs
        gate_ref[...] = gate32_ref[...].astype(gate_ref.dtype)
    else:
        o_ref, w_ref = refs
    mode = mode_ref[pl.program_id(0)]
    tm, tn = o_ref.shape
    n_heads = tn // HD

    @pl.when(pl.program_id(1) == 0)
    def _():
        w_ref[...] = w32_ref[...].astype(w_ref.dtype)

    def norm_head(acc, h):
        blk = acc[:, h * HD:(h + 1) * HD]
        ms = jnp.mean(blk * blk, axis=-1, keepdims=True)
        return blk * lax.rsqrt(ms + RMS_EPS) * g_ref[:, h * HD:(h + 1) * HD]

    def rope(y, rs):
        even = (lax.broadcasted_iota(jnp.int32, y.shape, 1) % 2) == 0
        partner = jnp.where(even, pltpu.roll(y, HD - 1, 1), pltpu.roll(y, 1, 1))
        return y * cos_ref[rs, :] + partner * sin_ref[rs, :]

    def head_epilogue(mode_id, acc, h, rs):
        is_key_half = h < n_heads // 2
        if mode_id == MODE_NORM:
            return norm_head(acc, h)
        if mode_id == MODE_NORM_ROPE or (mode_id == MODE_KV_B and is_key_half):
            return rope(norm_head(acc, h), rs)
        return acc[:, h * HD:(h + 1) * HD]

    def run(mode_id):
        for c in range(tm // ROW_CHUNK):
            rs = slice(c * ROW_CHUNK, (c + 1) * ROW_CHUNK)
            acc = jnp.dot(x_ref[rs, :], w_ref[...], preferred_element_type=_F32)
            if mode_id == MODE_PLAIN:
                o_ref[rs, :] = acc.astype(o_ref.dtype)
            elif mode_id == MODE_SILU:
                o_ref[rs, :] = (acc * _sigmoid(acc)).astype(o_ref.dtype)
            else:
                for h in range(n_heads):
                    o_ref[rs, h * HD:(h + 1) * HD] = head_epilogue(mode_id, acc, h, rs).astype(o_ref.dtype)

    for mode_id in modes_used:
        pl.when(mode == mode_id)(functools.partial(run, mode_id))


def _projection(xn, w_in, layer, src_tiles, modes, gains, cos_t, sin_t, out_dtype, seq, modes_used,
                round_gates=False):
    m, d = xn.shape
    n_t = src_tiles.shape[0]
    n_i = m // TM
    rope_blocks = seq // TM
    in_specs = [
        pl.BlockSpec((TM, d), lambda j, i, src, md: (i, 0)),
        pl.BlockSpec((None, d, TN), lambda j, i, src, md: (layer, 0, src[j])),
        pl.BlockSpec((None, 1, TN), lambda j, i, src, md: (j, 0, 0)),
        pl.BlockSpec((TM, HD), lambda j, i, src, md: (i % rope_blocks, 0)),
        pl.BlockSpec((TM, HD), lambda j, i, src, md: (i % rope_blocks, 0)),
    ]
    out_specs = pl.BlockSpec((TM, TN), lambda j, i, src, md: (i, j))
    out_shape = jax.ShapeDtypeStruct((m, n_t * TN), out_dtype)
    operands = [src_tiles, modes, xn, w_in, gains, cos_t, sin_t]
    if round_gates:
        gate_tiles = (w_in.shape[2] - GATE_START) // TN
        row_blocks = (n_t * n_i) // gate_tiles
        assert row_blocks >= 1 and d % row_blocks == 0 and (d // row_blocks) % 16 == 0 and GATE_START % TN == 0
        rb = d // row_blocks
        step = lambda j, i: jnp.minimum(j * n_i + i, gate_tiles * row_blocks - 1)
        in_specs.append(pl.BlockSpec(
            (None, rb, TN), lambda j, i, src, md: (layer, step(j, i) % row_blocks, GATE_START // TN + step(j, i) // row_blocks)))
        out_specs = [out_specs, pl.BlockSpec((rb, TN), lambda j, i, src, md: (step(j, i) % row_blocks, step(j, i) // row_blocks))]
        out_shape = [out_shape, jax.ShapeDtypeStruct((d, gate_tiles * TN), _BF16)]
        operands.append(w_in)
    return pl.pallas_call(
        functools.partial(_proj_kernel, modes_used=modes_used),
        grid_spec=pltpu.PrefetchScalarGridSpec(
            num_scalar_prefetch=2,
            grid=(n_t, n_i),
            in_specs=in_specs,
            out_specs=out_specs,
            scratch_shapes=[pltpu.VMEM((d, TN), _BF16)],
        ),
        out_shape=out_shape,
        compiler_params=_cparams("arbitrary", "arbitrary"),
        name="projection",
    )(*operands)


def _tile_plan(segments):
    src, modes, gain_keys, out_start = [], [], [], {}
    col = 0
    for name, mode, gain_key in segments:
        if name == "b_kv":
            start, width = _SEG_START["b_k"], 2 * B_KVW
            assert width == TN and _SEG_START["b_v"] == start + B_KVW
            out_start["b_k"], out_start["b_v"] = col, col + B_KVW
        else:
            start, width = _SEG_START[name], _SEG_WIDTHS[_SEG_NAMES.index(name)]
            out_start[name] = col
        assert start % TN == 0 and width % TN == 0, (name, start, width)
        for t in range(width // TN):
            src.append(start // TN + t)
            modes.append(mode)
            gain_keys.append(gain_key)
        col += width
    return np.asarray(src, np.int32), np.asarray(modes, np.int32), gain_keys, out_start


_Q_GAIN = HD ** -0.5 * LOG2E
_PLAN_BF16 = _tile_plan((
    ("a_q", MODE_NORM, (0, 0)), ("a_k", MODE_NORM, (0, 1)), ("a_v", MODE_PLAIN, None),
    ("b_q", MODE_NORM_ROPE, (1, 0)), ("b_kv", MODE_KV_B, (1, 1)),
    ("c_q", MODE_NORM, (2, 0)), ("c_k", MODE_NORM, (2, 1)), ("c_v", MODE_PLAIN, None)))
_PLAN_F32 = _tile_plan((
    ("a_z", MODE_SILU, None), ("b_z", MODE_SILU, None), ("c_z", MODE_SILU, None),
    ("d_q", MODE_NORM, (3, 0)), ("d_k", MODE_NORM, (3, 1)), ("d_v", MODE_PLAIN, None),
    ("d_z", MODE_SILU, None)))


def _gain_table(qk_gain, gain_keys):
    rows = []
    for key in gain_keys:
        if key is None:
            rows.append(jnp.ones((TN,), _F32))
        else:
            g = qk_gain[key[0], key[1]].astype(_F32)
            if key[1] == 0:
                g = g * _Q_GAIN
            rows.append(jnp.tile(g, TN // HD))
    return jnp.stack(rows)[:, None, :]


def _rope_tables(seq):
    t = jnp.arange(seq)
    row = (t // GRID_W).astype(_F32)
    col = (t % GRID_W).astype(_F32)
    n_pairs = HD // 4
    inv_freq = ROPE_THETA ** (-jnp.arange(n_pairs, dtype=_F32) / n_pairs)
    ang = jnp.concatenate([row[:, None] * inv_freq, col[:, None] * inv_freq], axis=-1)
    cos, sin = jnp.cos(ang), jnp.sin(ang)
    cos_t = jnp.repeat(cos, 2, axis=-1)
    sin_t = jnp.stack([-sin, sin], axis=-1).reshape(seq, HD)
    return cos_t, sin_t


def _na_window_starts(rows):
    starts = {}
    for r in range(rows):
        w0 = int(np.clip(NA_GROUP * (r // NA_GROUP) - NA_ROWS // 2, 0, rows - NA_SUPER))
        r0 = int(np.clip(r - NA_ROWS // 2, 0, rows - NA_ROWS))
        assert 0 <= r0 - w0 and r0 - w0 + NA_ROWS <= NA_SUPER and 0 <= r - w0 < NA_SUPER
        assert starts.setdefault(r - w0, r0 - w0) == r0 - w0
    return starts


def _na_bias_kernel(rb_ref, o_ref, base_ref, *, win_starts):
    h = pl.program_id(0)
    n_row_off, n_col_off = 2 * NA_ROWS - 1, 2 * NA_COLS - 1
    c = lax.broadcasted_iota(jnp.int32, (GRID_W, GRID_W), 0)
    kc = lax.broadcasted_iota(jnp.int32, (GRID_W, GRID_W), 1)
    c0 = jnp.clip(c - NA_COLS // 2, 0, GRID_W - NA_COLS)
    valid = (kc >= c0) & (kc < c0 + NA_COLS)
    col_off = kc - c + (NA_COLS - 1)
    masked = jnp.full((GRID_W, GRID_W), NEG_INF, _F32)
    for row_off in range(n_row_off):
        blk = masked
        for d in range(n_col_off):
            blk = jnp.where(col_off == d, rb_ref[(h * n_row_off + row_off) * n_col_off + d] * LOG2E, blk)
        base_ref[row_off] = jnp.where(valid, blk, NEG_INF)
    for dw in range(NA_SUPER):
        ws = win_starts.get(dw, 0)
        for i in range(NA_SUPER):
            in_window = ws <= i < ws + NA_ROWS
            o_ref[dw, :, i * GRID_W:(i + 1) * GRID_W] = base_ref[i - dw + NA_ROWS - 1] if in_window else masked


def _na_bias_table(rel_bias, rows):
    heads = rel_bias.shape[0]
    return pl.pallas_call(
        functools.partial(_na_bias_kernel, win_starts=_na_window_starts(rows)),
        grid=(heads,),
        in_specs=[pl.BlockSpec(memory_space=pltpu.SMEM)],
        out_specs=pl.BlockSpec((None, NA_SUPER, GRID_W, NA_SUPER * GRID_W), lambda h: (h, 0, 0, 0)),
        out_shape=jax.ShapeDtypeStruct((heads, NA_SUPER, GRID_W, NA_SUPER * GRID_W), _F32),
        scratch_shapes=[pltpu.VMEM((2 * NA_ROWS - 1, GRID_W, GRID_W), _F32)],
        compiler_params=_cparams("parallel"),
        name="na_bias_table",
    )(rel_bias.astype(_F32).reshape(-1))


def _na_kernel(q_ref, k_ref, v_ref, z_ref, bias_ref, o_ref, *, rows):
    gq = NA_GROUP * GRID_W
    sw = NA_SUPER * GRID_W

    def row_groups(t, carry):
        addr = []
        for k in range(NA_BATCH):
            g = t * NA_BATCH + k
            w0 = jnp.clip(NA_GROUP * g - NA_ROWS // 2, 0, rows - NA_SUPER)
            addr.append((pl.ds(pl.multiple_of(g * gq, gq), gq), pl.ds(pl.multiple_of(w0 * GRID_W, GRID_W), sw),
                         NA_GROUP * g - w0))
        scores = [_qk_t(q_ref[qs, :], k_ref[ks, :]) + bias_ref[pl.ds(dw, NA_GROUP)].reshape(gq, sw)
                  for qs, ks, dw in addr]
        stats = []
        for s in scores:
            p = jnp.exp2(s - jnp.max(s, axis=-1, keepdims=True))
            stats.append((jnp.sum(p, axis=-1, keepdims=True), p.astype(_BF16)))
        for (qs, ks, _), (l, p) in zip(addr, stats):
            o = jnp.dot(p, v_ref[ks, :], preferred_element_type=_F32) / l
            o_ref[qs, :] = (o * z_ref[qs, :]).astype(o_ref.dtype)
        return carry

    assert (rows // NA_GROUP) % NA_BATCH == 0
    lax.fori_loop(0, rows // NA_GROUP // NA_BATCH, row_groups, 0, unroll=NA_UNROLL)


def _neighbourhood(qkv, zf, bias_tab, batch, seq, off_q, off_z):
    rows = seq // GRID_W
    assert rows >= NA_SUPER and rows % NA_GROUP == 0
    blk = lambda off: pl.BlockSpec((seq, HD), lambda b, h: (b, off // HD + h))
    return pl.pallas_call(
        functools.partial(_na_kernel, rows=rows),
        grid=(batch, A_HEADS),
        in_specs=[blk(off_q["a_q"]), blk(off_q["a_k"]), blk(off_q["a_v"]), blk(off_z["a_z"]),
                  pl.BlockSpec((None, NA_SUPER, GRID_W, NA_SUPER * GRID_W), lambda b, h: (h, 0, 0, 0))],
        out_specs=pl.BlockSpec((seq, HD), lambda b, h: (b, h)),
        out_shape=jax.ShapeDtypeStruct((batch * seq, A_W), _BF16),
        compiler_params=_cparams("parallel", "parallel"),
        name="neighbourhood_attention",
    )(qkv, qkv, qkv, zf, bias_tab)


def _gqa_kernel(q_ref, k_ref, v_ref, z_ref, o_ref, vext_ref):
    n_heads = q_ref.shape[1] // HD
    n_kv = k_ref.shape[1] // HD
    group = n_heads // n_kv
    head_slice = lambda h: slice(h * HD, (h + 1) * HD)

    @pl.when(pl.program_id(1) == 0)
    def _():
        for hk in range(n_kv):
            vext_ref[:, 2 * hk * HD:(2 * hk + 1) * HD] = v_ref[:, head_slice(hk)]
            vext_ref[:, (2 * hk + 1) * HD:(2 * hk + 2) * HD] = jnp.ones((v_ref.shape[0], HD), vext_ref.dtype)

    scores = lambda h: _qk_t(q_ref[:, head_slice(h)], k_ref[:, head_slice(h // group)])
    s_next = scores(0)
    for h in range(n_heads):
        s = s_next
        if h + 1 < n_heads:
            s_next = scores(h + 1)
        p = jnp.exp2(s - jnp.max(s, axis=-1, keepdims=True)).astype(_BF16)
        hk = h // group
        acc = jnp.dot(p, vext_ref[:, 2 * hk * HD:(2 * hk + 2) * HD], preferred_element_type=_F32)
        o = acc[:, :HD] / acc[:, HD:HD + 1]
        o_ref[:, head_slice(h)] = (o * z_ref[:, head_slice(h)]).astype(o_ref.dtype)


def _gqa(qkv, zf, batch, seq, off_q, off_z):
    nq = seq // TQ_GQA
    assert off_q["b_q"] % B_QW == 0 and off_z["b_z"] % B_QW == 0 and off_q["b_k"] % B_KVW == 0
    q_spec = lambda off: pl.BlockSpec((TQ_GQA, B_QW), lambda b, i: (b * nq + i, off // B_QW))
    kv_spec = lambda off: pl.BlockSpec((seq, B_KVW), lambda b, i: (b, off // B_KVW))
    return pl.pallas_call(
        _gqa_kernel,
        grid=(batch, nq),
        in_specs=[q_spec(off_q["b_q"]), kv_spec(off_q["b_k"]), kv_spec(off_q["b_v"]), q_spec(off_z["b_z"])],
        out_specs=pl.BlockSpec((TQ_GQA, B_QW), lambda b, i: (b * nq + i, 0)),
        out_shape=jax.ShapeDtypeStruct((batch * seq, B_QW), _BF16),
        scratch_shapes=[pltpu.VMEM((seq, 2 * B_KVW), _BF16)],
        compiler_params=_cparams("parallel", "arbitrary"),
        name="gqa_attention",
    )(qkv, qkv, qkv, zf)


def _store_transposed(vt_ref, v_ref):
    seq, d = v_ref.shape
    for c in range(seq // KV_CHUNK):
        cs = slice(c * KV_CHUNK, (c + 1) * KV_CHUNK)
        vt_ref[:d, cs] = v_ref[cs, :].astype(_F32).T.astype(vt_ref.dtype)
    vt_ref[d:, :] = jnp.ones((vt_ref.shape[0] - d, seq), vt_ref.dtype)


class _OnlineSoftmax:
    def __init__(self, d_v, n_q):
        self.d_v = d_v
        self.m = jnp.full((1, n_q), NEG_INF, _F32)
        self.acc = jnp.zeros((d_v + ONES_ROWS, n_q), _F32)

    def update(self, s, vt_chunk):
        m_new = jnp.maximum(self.m, jnp.max(s, axis=0, keepdims=True))
        p = jnp.exp2(s - m_new).astype(_BF16)
        self.acc = (jnp.exp2(self.m - m_new) * self.acc
                    + jnp.dot(vt_chunk, p, preferred_element_type=_F32))
        self.m = m_new

    def result(self):
        return self.acc[:self.d_v] / self.acc[self.d_v:self.d_v + 1]


def _diff_kernel(slope_ref, q_ref, k_ref, v_ref, z_ref, lam_ref, g_ref, o_ref, vt_ref, pen_ref, *, lam_init):
    h, i = pl.program_id(1), pl.program_id(2)
    seq = k_ref.shape[0]
    tq, hw = q_ref.shape
    key0 = seq - tq

    @pl.when(i == 0)
    def _():
        _store_transposed(vt_ref, v_ref)
        slope = slope_ref[h] * LOG2E
        n_rows = pen_ref.shape[0]
        for y0 in range(0, n_rows, KV_CHUNK):
            size = min(KV_CHUNK, n_rows - y0)
            row_minus_col = (lax.broadcasted_iota(jnp.int32, (size, tq), 0)
                             - lax.broadcasted_iota(jnp.int32, (size, tq), 1))
            pen_ref[y0:y0 + size, :] = slope * jnp.abs(row_minus_col + (y0 - key0)).astype(_F32)

    lp = lam_ref[...]
    lam = (jnp.exp(jnp.sum(lp[0:1] * lp[1:2], axis=-1, keepdims=True))
           - jnp.exp(jnp.sum(lp[2:3] * lp[3:4], axis=-1, keepdims=True)) + lam_init)

    states = [_OnlineSoftmax(hw, tq) for _ in range(2)]
    pen0 = pl.multiple_of(key0 - i * tq, tq)
    for c in range(seq // KV_CHUNK):
        cs = slice(c * KV_CHUNK, (c + 1) * KV_CHUNK)
        pen = pen_ref[pl.ds(pen0 + c * KV_CHUNK, KV_CHUNK), :]
        for mi, state in enumerate(states):
            sl = slice(mi * HD, (mi + 1) * HD)
            state.update(_qk_t(k_ref[cs, sl], q_ref[:, sl]) - pen, vt_ref[:, cs])
    o = (states[0].result() - lam * states[1].result()).T
    ms = jnp.mean(o * o, axis=-1, keepdims=True)
    o = o * lax.rsqrt(ms + RMS_EPS) * g_ref[...] * (1.0 - lam_init)
    o_ref[...] = (o * z_ref[...]).astype(o_ref.dtype)


def _diff_attention(qkv, zf, diff_lambda, subln_g, lam_init, batch, seq, off_q, off_z):
    hw = 2 * HD
    nq = seq // TQ
    slopes = jnp.asarray(2.0 ** (-8.0 * np.arange(1, C_HEADS + 1) / C_HEADS), dtype=_F32)
    q_spec = lambda off: pl.BlockSpec((TQ, hw), lambda b, h, i, sl: (b * nq + i, off // hw + h))
    kv_spec = lambda off: pl.BlockSpec((seq, hw), lambda b, h, i, sl: (b, off // hw + h))
    return pl.pallas_call(
        functools.partial(_diff_kernel, lam_init=lam_init),
        grid_spec=pltpu.PrefetchScalarGridSpec(
            num_scalar_prefetch=1,
            grid=(batch, C_HEADS, nq),
            in_specs=[q_spec(off_q["c_q"]), kv_spec(off_q["c_k"]), kv_spec(off_q["c_v"]), q_spec(off_z["c_z"]),
                      pl.BlockSpec((4, HD), lambda b, h, i, sl: (0, 0)),
                      pl.BlockSpec((1, hw), lambda b, h, i, sl: (0, 0))],
            out_specs=pl.BlockSpec((TQ, hw), lambda b, h, i, sl: (b * nq + i, h)),
            scratch_shapes=[pltpu.VMEM((hw + ONES_ROWS, seq), _BF16),
                            pltpu.VMEM((2 * seq - TQ, TQ), _F32)],
        ),
        out_shape=jax.ShapeDtypeStruct((batch * seq, C_VW), _BF16),
        compiler_params=_cparams("parallel", "parallel", "arbitrary"),
        name="diff_attention",
    )(slopes, qkv, qkv, qkv, zf, diff_lambda.astype(_F32), subln_g.astype(_F32).reshape(1, hw))


def _dilated_group(gi, window, dil, slope, q_ref, k_ref, v_ref, og_ref, lse_ref, bias_ref):
    seq = q_ref.shape[0]
    length = seq // dil
    tq = min(D_TQ, length)
    span = min(D_SPAN, length)
    n_tiles = length // tq
    n_side = window // (2 * dil)
    half = (span - tq) // 2
    unit_pen = slope * (float(dil) * LOG2E)
    rel0 = (lax.broadcasted_iota(jnp.int32, (tq, span), 1)
            - lax.broadcasted_iota(jnp.int32, (tq, span), 0))
    for vi in range(3 if n_tiles > 1 else 1):
        rel = jnp.abs(rel0 - vi * half)
        bias_ref[vi, :tq, :span] = jnp.where(rel <= n_side, -unit_pen * rel.astype(_F32), NEG_INF)

    def rows(start, size):
        return pl.ds(start, size) if dil == 1 else pl.ds(start, size, stride=dil)

    def tile_group(gidx, carry):
        tiles = []
        for k in range(D_GROUP):
            idx = gidx * D_GROUP + k
            r = lax.div(idx, n_tiles)
            u0 = lax.rem(idx, n_tiles) * tq
            ks = jnp.clip(u0 - half, 0, length - span)
            vi = lax.div(u0 - ks, half) if n_tiles > 1 else 0
            tiles.append((rows(r + dil * u0, tq), rows(r + dil * ks, span), vi))
        scores = [_qk_t(q_ref[q_rows, :].astype(_BF16), k_ref[k_rows, :].astype(_BF16)) + bias_ref[vi, :tq, :span]
                  for q_rows, k_rows, vi in tiles]
        stats = []
        for s in scores:
            m = jnp.max(s, axis=-1, keepdims=True)
            stats.append((m, jnp.exp2(s - m).astype(_BF16)))
        ones = jnp.ones((span, HD), _BF16)
        for (q_rows, k_rows, _), (m, p) in zip(tiles, stats):
            v_ext = jnp.concatenate([v_ref[k_rows, :].astype(_BF16), ones], axis=1)
            acc = jnp.dot(p, v_ext, preferred_element_type=_F32)
            l = acc[:, HD:HD + 1]
            og_ref[gi, q_rows, :] = acc[:, :HD] / l
            lse_ref[gi, q_rows, :] = jnp.broadcast_to(m + jnp.log2(l), (tq, LANES))
        return carry

    assert (dil * n_tiles) % D_GROUP == 0
    lax.fori_loop(0, dil * n_tiles // D_GROUP, tile_group, 0, unroll=D_UNROLL)


def _dilated_kernel(slope_ref, q_ref, k_ref, v_ref, z_ref, o_ref, og_ref, lse_ref, bias_ref):
    h, g = pl.program_id(1), pl.program_id(2)
    for gi, (window, dil) in enumerate(D_PATTERNS):
        @pl.when(g == gi)
        def _(gi=gi, window=window, dil=dil):
            _dilated_group(gi, window, dil, slope_ref[gi * D_HEADS_PER_GROUP + h],
                           q_ref, k_ref, v_ref, og_ref, lse_ref, bias_ref)

    @pl.when(g == D_GROUPS - 1)
    def _():
        seq = o_ref.shape[0]
        chunk = min(512, seq)

        def merge(ci, carry):
            rs = pl.ds(pl.multiple_of(ci * chunk, chunk), chunk)
            lses = [lse_ref[gi, rs, :] for gi in range(D_GROUPS)]
            top = functools.reduce(jnp.maximum, lses)
            ws = [jnp.exp2(v - top) for v in lses]
            num = functools.reduce(jnp.add, [w * og_ref[gi, rs, :] for gi, w in enumerate(ws)])
            out = num / functools.reduce(jnp.add, ws)
            o_ref[rs, :] = (out * z_ref[rs, :]).astype(o_ref.dtype)
            return carry

        lax.fori_loop(0, seq // chunk, merge, 0)


def _dilated(zf, batch, seq, off_z):
    n = D_GROUPS * D_HEADS_PER_GROUP
    slopes = jnp.asarray(2.0 ** (-8.0 * np.arange(1, n + 1) / n), dtype=_F32)
    for window, dil in D_PATTERNS:
        assert seq % dil == 0 and (seq // dil) % min(D_TQ, seq // dil) == 0
        assert D_SPAN - D_TQ >= 2 * (window // (2 * dil)) or seq // dil <= D_SPAN
    qkv_spec = lambda off: pl.BlockSpec(
        (seq, HD), lambda b, h, g, sl: (b, off // HD + g * D_HEADS_PER_GROUP + h))
    return pl.pallas_call(
        _dilated_kernel,
        grid_spec=pltpu.PrefetchScalarGridSpec(
            num_scalar_prefetch=1,
            grid=(batch, D_HEADS_PER_GROUP, D_GROUPS),
            in_specs=[qkv_spec(off_z["d_q"]), qkv_spec(off_z["d_k"]), qkv_spec(off_z["d_v"]),
                      pl.BlockSpec((seq, HD), lambda b, h, g, sl: (b, off_z["d_z"] // HD + h))],
            out_specs=pl.BlockSpec((seq, HD), lambda b, h, g, sl: (b, h)),
            scratch_shapes=[pltpu.VMEM((D_GROUPS, seq, HD), _F32),
                            pltpu.VMEM((D_GROUPS, seq, LANES), _F32),
                            pltpu.VMEM((3, D_TQ, D_SPAN), _F32)],
        ),
        out_shape=jax.ShapeDtypeStruct((batch * seq, D_OW), _BF16),
        compiler_params=_cparams("parallel", "parallel", "arbitrary"),
        name="dilated_attention",
    )(slopes, zf, zf, zf, zf)


def _merge_kernel(x_ref, wg_ref, *refs, n_br):
    y_refs, wb_refs = refs[:n_br], refs[n_br:2 * n_br]
    o_ref, acc_ref = refs[2 * n_br:]
    br = pl.program_id(2)
    tm = x_ref.shape[0]

    @pl.when(br == 0)
    def _():
        acc_ref[...] = jnp.zeros_like(acc_ref)

    def add_branch(y_ref, wb_ref):
        for c in range(tm // ROW_CHUNK):
            rs = slice(c * ROW_CHUNK, (c + 1) * ROW_CHUNK)
            gate = _sigmoid(jnp.dot(x_ref[rs, :], wg_ref[...], preferred_element_type=_F32))
            acc_ref[rs, :] += gate * jnp.dot(y_ref[rs, :], wb_ref[...], preferred_element_type=_F32)

    for b in range(n_br):
        pl.when(br == b)(functools.partial(add_branch, y_refs[b], wb_refs[b]))

    @pl.when(br == n_br - 1)
    def _():
        o_ref[...] = acc_ref[...].astype(o_ref.dtype)


def _gated_merge(xn, w_gate, layer, ys, wbs):
    m, d = xn.shape
    n_br = len(ys)
    assert d % TN == 0 and w_gate.shape == (d, n_br * d)
    tiles_per_branch = d // TN
    once = dict(pipeline_mode=pl.Buffered(1))
    y_specs = [pl.BlockSpec((TM_MERGE, y.shape[1]), lambda i, j, br: (i, 0), **once) for y in ys]
    wb_specs = [pl.BlockSpec((None, w.shape[1], TN), lambda i, j, br: (layer, 0, j)) for w in wbs]
    return pl.pallas_call(
        functools.partial(_merge_kernel, n_br=n_br),
        grid=(m // TM_MERGE, d // TN, n_br),
        in_specs=[
            pl.BlockSpec((TM_MERGE, d), lambda i, j, br: (i, 0), **once),
            pl.BlockSpec((d, TN), lambda i, j, br: (0, br * tiles_per_branch + j)),
            *y_specs, *wb_specs,
        ],
        out_specs=pl.BlockSpec((TM_MERGE, TN), lambda i, j, br: (i, j)),
        out_shape=jax.ShapeDtypeStruct((m, d), _BF16),
        scratch_shapes=[pltpu.VMEM((TM_MERGE, TN), _F32)],
        compiler_params=_cparams("parallel", "parallel", "arbitrary"),
        name="gated_merge",
    )(xn, w_gate, *ys, *wbs)


def _out_kernel(m_ref, w32_ref, x_ref, o_ref, w_ref):
    @pl.when(pl.program_id(1) == 0)
    def _():
        w_ref[...] = w32_ref[...].astype(w_ref.dtype)

    for c in range(o_ref.shape[0] // ROW_CHUNK):
        rs = slice(c * ROW_CHUNK, (c + 1) * ROW_CHUNK)
        o_ref[rs, :] = x_ref[rs, :] + jnp.dot(m_ref[rs, :], w_ref[...], preferred_element_type=_F32)


def _out_projection(merged, w_out, layer, x):
    m, d = x.shape
    return pl.pallas_call(
        _out_kernel,
        grid=(d // TN, m // TM),
        in_specs=[pl.BlockSpec((TM, d), lambda j, i: (i, 0)),
                  pl.BlockSpec((None, d, TN), lambda j, i: (layer, 0, j)),
                  pl.BlockSpec((TM, TN), lambda j, i: (i, j))],
        out_specs=pl.BlockSpec((TM, TN), lambda j, i: (i, j)),
        out_shape=jax.ShapeDtypeStruct((m, d), _F32),
        scratch_shapes=[pltpu.VMEM((d, TN), _BF16)],
        compiler_params=_cparams("parallel", "arbitrary"),
        name="out_projection",
    )(merged, w_out, x)


def kernel(x, norm_g, w_in, qk_gain, na_rel_bias, diff_lambda, diff_subln_g,
           w_branch_a, w_branch_b, w_branch_c, w_branch_d, w_out):
    batch, seq, d_model = x.shape
    depth = w_in.shape[0]
    m = batch * seq
    assert w_in.shape[2] == GATE_START + N_BRANCHES * d_model
    assert m % TM == 0 and seq % TM == 0 and seq % TQ == 0 and seq % GRID_W == 0 and d_model % TN == 0

    wbs = tuple(w.astype(_BF16) for w in (w_branch_a, w_branch_b, w_branch_c, w_branch_d))
    cos_t, sin_t = _rope_tables(seq)
    src_b, modes_b, gkeys_b, off_q = _PLAN_BF16
    src_f, modes_f, gkeys_f, off_z = _PLAN_F32

    xf = x.reshape(m, d_model)
    for l in range(depth):
        xn = _rmsnorm(xf, norm_g[l])
        qkv = _projection(xn, w_in, l, jnp.asarray(src_b), jnp.asarray(modes_b),
                          _gain_table(qk_gain[l], gkeys_b), cos_t, sin_t, _BF16, seq,
                          tuple(sorted(set(modes_b.tolist()))))
        zf, w_gate_b = _projection(xn, w_in, l, jnp.asarray(src_f), jnp.asarray(modes_f),
                                   _gain_table(qk_gain[l], gkeys_f), cos_t, sin_t, _F32, seq,
                                   tuple(sorted(set(modes_f.tolist()))), round_gates=True)
        lam_init = 0.8 - 0.6 * math.exp(-0.3 * l)
        y_a = _neighbourhood(qkv, zf, _na_bias_table(na_rel_bias[l], seq // GRID_W), batch, seq, off_q, off_z)
        y_b = _gqa(qkv, zf, batch, seq, off_q, off_z)
        y_c = _diff_attention(qkv, zf, diff_lambda[l], diff_subln_g[l], lam_init, batch, seq, off_q, off_z)
        y_d = _dilated(zf, batch, seq, off_z)
        merged = _gated_merge(xn, w_gate_b, l, (y_a, y_b, y_c, y_d), wbs)
        xf = _out_projection(merged, w_out, l, xf)
    return xf.reshape(batch, seq, d_model)
```
